```python
import functools
import jax, jax.numpy as jnp
from jax import lax
import numpy as np

D_MODEL = 2048
BATCH = 4
SEQ = 2048
DEPTH = 1
DEC_BATCH = 128
DEC_SEQ = 8
PAST_LEN = 8192
PAGE_SIZE = 128

N_HEADS = 16
QK_NOPE = 128
QK_ROPE = 64
QK_DIM = QK_NOPE + QK_ROPE
V_DIM = 128
KV_LORA = 512
ROPE_BASE = 10000.0
ATTN_SCALE = QK_DIM ** -0.5
Q_BLOCK = 128
POOL_WINDOWS = (2, 4, 8, 16)
N_POOL_GROUPS = len(POOL_WINDOWS)
POOL_WIDTH = D_MODEL // 2
POOL_GROUP = POOL_WIDTH // N_POOL_GROUPS
POOL_BUF = max(POOL_WINDOWS) - 1
D_FF = 5632
CONV_W = 3
EPS = 1e-6
Q_COLS = N_HEADS * QK_DIM
N_BRANCHES = 2
SPLITS = (Q_COLS, Q_COLS + KV_LORA, Q_COLS + KV_LORA + QK_ROPE, Q_COLS + KV_LORA + QK_ROPE + POOL_WIDTH)
IN_WIDTH = SPLITS[-1] + N_BRANCHES * D_MODEL

kernel_name = 'mla_pool_gated_hybrid_step'


def rmsnorm(x, g):
    xf = x.astype(jnp.float32)
    y = xf * lax.rsqrt(jnp.mean(xf * xf, axis=-1, keepdims=True) + EPS)
    return (y * g.astype(jnp.float32)).astype(x.dtype)


def rope(x, pos):
    half = QK_ROPE // 2
    inv = ROPE_BASE ** (-jnp.arange(half, dtype=jnp.float32) / half)
    ang = pos[:, None] * inv[None, :]
    shape = (1, pos.shape[0]) + (1,) * (x.ndim - 3) + (half,)
    c = jnp.cos(ang).reshape(shape)
    s = jnp.sin(ang).reshape(shape)
    xf = x.astype(jnp.float32)
    x1, x2 = xf[..., :half], xf[..., half:]
    return jnp.concatenate([x1 * c - x2 * s, x1 * s + x2 * c], axis=-1).astype(x.dtype)


def mla_prompt(q_nope, q_rope, c_kv, k_rope, w_uk, w_uv):
    B, S = q_nope.shape[:2]
    k_nope = jnp.einsum('bsr,rhd->bshd', c_kv, w_uk)
    v = jnp.einsum('bsr,rhd->bshd', c_kv, w_uv)
    nb = S // Q_BLOCK
    qn_b = q_nope.reshape(B, nb, Q_BLOCK, N_HEADS, QK_NOPE).transpose(1, 0, 2, 3, 4)
    qr_b = q_rope.reshape(B, nb, Q_BLOCK, N_HEADS, QK_ROPE).transpose(1, 0, 2, 3, 4)
    kpos = jnp.arange(S)

    def one_block(args):
        i, qn, qr = args
        s = (jnp.einsum('bqhd,bkhd->bhqk', qn, k_nope)
             + jnp.einsum('bqhr,bkr->bhqk', qr, k_rope)).astype(jnp.float32) * ATTN_SCALE
        qpos = i * Q_BLOCK + jnp.arange(Q_BLOCK)
        mask = kpos[None, :] <= qpos[:, None]
        s = jnp.where(mask[None, None], s, -jnp.inf)
        p = jax.nn.softmax(s, axis=-1).astype(v.dtype)
        return jnp.einsum('bhqk,bkhd->bqhd', p, v)

    o = lax.map(one_block, (jnp.arange(nb), qn_b, qr_b))
    return o.transpose(1, 0, 2, 3, 4).reshape(B, S, N_HEADS, V_DIM)


def mla_sample(q_nope, q_rope, c_new, kr_new, w_uk, w_uv, cache_c, cache_kr, page_table):
    T = q_nope.shape[1]
    n_past = page_table.shape[1] * PAGE_SIZE
    lpos = jnp.arange(n_past + T)
    tpos = n_past + jnp.arange(T)
    mask = lpos[None, :] <= tpos[:, None]

    def one_seq(args):
        qn, qr, cn, krn, pages = args
        c_all = jnp.concatenate([cache_c[pages].reshape(n_past, KV_LORA), cn.astype(cache_c.dtype)], axis=0)
        kr_all = jnp.concatenate([cache_kr[pages].reshape(n_past, QK_ROPE), krn.astype(cache_kr.dtype)], axis=0)
        q_lat = jnp.einsum('thd,rhd->thr', qn, w_uk)
        s = (jnp.einsum('thr,lr->htl', q_lat, c_all)
             + jnp.einsum('thr,lr->htl', qr, kr_all)).astype(jnp.float32) * ATTN_SCALE
        s = jnp.where(mask[None], s, -jnp.inf)
        p = jax.nn.softmax(s, axis=-1).astype(c_all.dtype)
        o_lat = jnp.einsum('htl,lr->thr', p, c_all)
        return jnp.einsum('thr,rhd->thd', o_lat, w_uv)

    return lax.map(one_seq, (q_nope, q_rope, c_new, kr_new, page_table))


def pool_mix(u, buf, pos0, w_pool_group, pool_scale):
    B, T, C = u.shape
    ext = jnp.concatenate([buf.astype(u.dtype), u], axis=1)
    cs = jnp.cumsum(ext.astype(jnp.float32), axis=1)
    cs = jnp.concatenate([jnp.zeros((B, 1, C), jnp.float32), cs], axis=1)
    end = cs[:, POOL_BUF + 1:]
    pos = pos0 + jnp.arange(T, dtype=jnp.float32)
    outs = []
    for g, w in enumerate(POOL_WINDOWS):
        sl = slice(g * POOL_GROUP, (g + 1) * POOL_GROUP)
        start = cs[:, POOL_BUF + 1 - w: POOL_BUF + 1 - w + T, sl]
        cnt = jnp.minimum(pos + 1.0, float(w))[None, :, None]
        outs.append((end[..., sl] - start) / cnt - u[..., sl].astype(jnp.float32))
    pooled = jnp.stack(outs, axis=2).astype(u.dtype)
    y = jnp.einsum('btgc,gcd->btgd', pooled, w_pool_group).reshape(B, T, C) * pool_scale
    return y, ext[:, -POOL_BUF:]


def conv_ffn(h, buf, w_up, conv_w, conv_b, w_down):
    T = h.shape[1]
    z = h @ w_up
    a, v = z[..., :D_FF], z[..., D_FF:]
    ext = jnp.concatenate([buf.astype(a.dtype), a], axis=1)
    conv = conv_b
    for k in range(CONV_W):
        conv = conv + conv_w[k] * ext[:, k:k + T]
    hid = jax.nn.silu(conv) * v
    return hid @ w_down, ext[:, -(CONV_W - 1):]


def trunk_layer(x, pos0, attend, pool_buf, conv_buf, norm_attn_g, w_in, b_gates, kv_norm_g, w_uk, w_uv,
                w_attn_out, w_pool_group, pool_scale, w_pool_out, w_o, norm_ffn_g, w_up, conv_w, conv_b, w_down):
    B, S, _ = x.shape
    h = rmsnorm(x, norm_attn_g)
    z = h @ w_in
    q, c_kv, k_rope, u_pool, z_gate = jnp.split(z, SPLITS, axis=-1)
    pos = pos0 + jnp.arange(S, dtype=jnp.float32)
    q = q.reshape(B, S, N_HEADS, QK_DIM)
    q_nope = q[..., :QK_NOPE]
    q_rope = rope(q[..., QK_NOPE:], pos)
    c_kv = rmsnorm(c_kv, kv_norm_g)
    k_rope = rope(k_rope, pos)
    o_attn = attend(q_nope, q_rope, c_kv, k_rope, w_uk, w_uv).reshape(B, S, N_HEADS * V_DIM)
    o_pool, new_pool = pool_mix(u_pool, pool_buf, pos0, w_pool_group, pool_scale)
    gates = jax.nn.sigmoid((z_gate.reshape(B, S, N_BRANCHES, D_MODEL) + b_gates).astype(jnp.float32)).astype(x.dtype)
    merged = gates[:, :, 0] * (o_attn @ w_attn_out) + gates[:, :, 1] * (o_pool @ w_pool_out)
    x = x + merged @ w_o
    f, new_conv = conv_ffn(rmsnorm(x, norm_ffn_g), conv_buf, w_up, conv_w, conv_b, w_down)
    x = x + f
    return x, c_kv, k_rope, new_pool, new_conv


def setup_inputs(seed: int = 0) -> dict:
    key = jax.random.key(seed)
    ks = jax.random.split(key, 32)
    f32 = jnp.float32

    def nrm(k, shape, scale):
        return jax.random.normal(k, shape, f32) * scale

    n_pages = PAST_LEN // PAGE_SIZE
    n_used = DEC_BATCH * n_pages
    n_phys = (n_used * 5) // 4
    page_table = jax.random.permutation(ks[0], n_phys)[:n_used].reshape(DEC_BATCH, n_pages).astype(jnp.int32)
    L = DEPTH
    return {
        'x_prompt': nrm(ks[1], (BATCH, SEQ, D_MODEL), 1.0),
        'x_sample': nrm(ks[2], (DEC_BATCH, DEC_SEQ, D_MODEL), 1.0),
        'cache_kv_latent': nrm(ks[3], (L, n_phys, PAGE_SIZE, KV_LORA), 1.0),
        'cache_k_rope': nrm(ks[4], (L, n_phys, PAGE_SIZE, QK_ROPE), 1.0),
        'page_table': page_table,
        'state_pool': nrm(ks[5], (L, DEC_BATCH, POOL_BUF, POOL_WIDTH), 1.0),
        'state_conv': nrm(ks[6], (L, DEC_BATCH, CONV_W - 1, D_FF), 1.0),
        'norm_attn_g': 1.0 + nrm(ks[7], (L, D_MODEL), 0.02),
        'w_in': nrm(ks[8], (L, D_MODEL, IN_WIDTH), D_MODEL ** -0.5),
        'b_gates': nrm(ks[9], (L, N_BRANCHES, D_MODEL), 0.02),
        'kv_norm_g': 1.0 + nrm(ks[10], (L, KV_LORA), 0.02),
        'w_uk': nrm(ks[11], (L, KV_LORA, N_HEADS, QK_NOPE), KV_LORA ** -0.5),
        'w_uv': nrm(ks[12], (L, KV_LORA, N_HEADS, V_DIM), KV_LORA ** -0.5),
        'w_attn_out': nrm(ks[13], (L, N_HEADS * V_DIM, D_MODEL), (N_HEADS * V_DIM) ** -0.5),
        'w_pool_group': nrm(ks[14], (L, N_POOL_GROUPS, POOL_GROUP, POOL_GROUP), POOL_GROUP ** -0.5),
        'pool_scale': 1.0 + nrm(ks[15], (L, POOL_WIDTH), 0.02),
        'w_pool_out': nrm(ks[16], (L, POOL_WIDTH, D_MODEL), POOL_WIDTH ** -0.5),
        'w_o': nrm(ks[17], (L, D_MODEL, D_MODEL), D_MODEL ** -0.5),
        'norm_ffn_g': 1.0 + nrm(ks[18], (L, D_MODEL), 0.02),
        'w_up': nrm(ks[19], (L, D_MODEL, 2 * D_FF), D_MODEL ** -0.5),
        'conv_w': nrm(ks[20], (L, CONV_W, D_FF), CONV_W ** -0.5),
        'conv_b': nrm(ks[21], (L, D_FF), 0.02),
        'w_down': nrm(ks[22], (L, D_FF, D_MODEL), D_FF ** -0.5),
        'norm_final_g': 1.0 + nrm(ks[23], (D_MODEL,), 0.02),
    }


def reference(x_prompt, x_sample, cache_kv_latent, cache_k_rope, page_table, state_pool, state_conv,
              norm_attn_g, w_in, b_gates, kv_norm_g, w_uk, w_uv, w_attn_out, w_pool_group, pool_scale,
              w_pool_out, w_o, norm_ffn_g, w_up, conv_w, conv_b, w_down, norm_final_g):
    yp, ys = x_prompt, x_sample
    n_past = page_table.shape[1] * PAGE_SIZE
    kvp, krp, kvs, krs, plp, pls, cvp, cvs = [], [], [], [], [], [], [], []
    for l in range(DEPTH):
        lw = (norm_attn_g[l], w_in[l], b_gates[l], kv_norm_g[l], w_uk[l], w_uv[l], w_attn_out[l],
              w_pool_group[l], pool_scale[l], w_pool_out[l], w_o[l], norm_ffn_g[l], w_up[l], conv_w[l],
              conv_b[l], w_down[l])
        B = yp.shape[0]
        pool0 = jnp.zeros((B, POOL_BUF, POOL_WIDTH), yp.dtype)
        conv0 = jnp.zeros((B, CONV_W - 1, D_FF), yp.dtype)
        yp, c_p, r_p, pool_p, conv_p = trunk_layer(yp, 0, mla_prompt, pool0, conv0, *lw)
        attend_s = functools.partial(mla_sample, cache_c=cache_kv_latent[l], cache_kr=cache_k_rope[l],
                                     page_table=page_table)
        ys, c_s, r_s, pool_s, conv_s = trunk_layer(ys, n_past, attend_s, state_pool[l], state_conv[l], *lw)
        kvp.append(c_p); krp.append(r_p); kvs.append(c_s); krs.append(r_s)
        plp.append(pool_p); pls.append(pool_s); cvp.append(conv_p); cvs.append(conv_s)
    y_prompt = rmsnorm(yp, norm_final_g)
    y_sample = rmsnorm(ys, norm_final_g)
    return (y_prompt, y_sample, jnp.stack(kvp), jnp.stack(krp), jnp.stack(kvs), jnp.stack(krs),
            jnp.stack(plp), jnp.stack(pls), jnp.stack(cvp), jnp.stack(cvs))
```

```python
import functools

import jax
import jax.numpy as jnp
from jax import lax
from jax.experimental import pallas as pl
from jax.experimental.pallas import tpu as pltpu

F32 = jnp.float32
BF16 = jnp.bfloat16

N_HEADS = 16
QK_NOPE = 128
QK_ROPE = 64
V_DIM = 128
KV_LORA = 512
ROPE_BASE = 10000.0
ATTN_SCALE = (QK_NOPE + QK_ROPE) ** -0.5
PAGE_SIZE = 128
POOL_WINDOWS = (2, 4, 8, 16)
POOL_BUF = max(POOL_WINDOWS) - 1
CONV_W = 3
EPS = 1e-6

LANES = 128
SUBLANES = 8
HEAD_PAD = 2 * LANES
VMEM_LIMIT = 48 * 1024 * 1024

ROW_TILE = 512


def _cparams(*sem):
    return pltpu.CompilerParams(dimension_semantics=sem, vmem_limit_bytes=VMEM_LIMIT)


def _dot(a, b):
    return jnp.dot(a, b, preferred_element_type=F32)


def _dot_nt(a, b):
    return lax.dot_general(a, b, (((1,), (1,)), ((), ())), preferred_element_type=F32)


def _rms(x, g):
    ms = jnp.mean(x * x, axis=-1, keepdims=True)
    return x * lax.rsqrt(ms + EPS) * g


def _rope128(x, ct, st):
    lane = lax.broadcasted_iota(jnp.int32, x.shape, 1)
    other = jnp.where(lane < QK_ROPE // 2, pltpu.roll(x, LANES - QK_ROPE // 2, 1), pltpu.roll(x, QK_ROPE // 2, 1))
    return x * ct + other * st


def _norm_kernel(x_ref, g_ref, o_ref):
    o_ref[...] = _rms(x_ref[...], g_ref[...]).astype(o_ref.dtype)


def _norm(x, g):
    m, d = x.shape
    return pl.pallas_call(
        _norm_kernel,
        grid=(m // ROW_TILE,),
        in_specs=[pl.BlockSpec((ROW_TILE, d), lambda i: (i, 0)), pl.BlockSpec((1, d), lambda i: (0, 0))],
        out_specs=pl.BlockSpec((ROW_TILE, d), lambda i: (i, 0)),
        out_shape=jax.ShapeDtypeStruct((m, d), BF16),
        compiler_params=_cparams("parallel"),
        name="rmsnorm",
    )(x, g.reshape(1, d))


def _mm_kernel(*refs, act, has_bias, has_res):
    h_ref, w_ref = refs[0], refs[1]
    k = 2
    r = _dot(h_ref[...], w_ref[...])
    if has_bias:
        r = r + refs[k][...]
        k += 1
    if act == "sigmoid":
        r = jax.nn.sigmoid(r)
    if has_res:
        r = refs[k][...] + r
        k += 1
    o_ref = refs[k]
    o_ref[...] = r.astype(o_ref.dtype)


def _mm(h, w, *, bias=None, res=None, act=None, out_dtype=F32, tn=512, name="mm"):
    m, kd = h.shape
    n = w.shape[1]
    ins = [h, w]
    specs = [pl.BlockSpec((ROW_TILE, kd), lambda i, j: (i, 0)), pl.BlockSpec((kd, tn), lambda i, j: (0, j))]
    if bias is not None:
        ins.append(bias.reshape(1, n))
        specs.append(pl.BlockSpec((1, tn), lambda i, j: (0, j)))
    if res is not None:
        ins.append(res)
        specs.append(pl.BlockSpec((ROW_TILE, tn), lambda i, j: (i, j)))
    return pl.pallas_call(
        functools.partial(_mm_kernel, act=act, has_bias=bias is not None, has_res=res is not None),
        grid=(m // ROW_TILE, n // tn),
        in_specs=specs,
        out_specs=pl.BlockSpec((ROW_TILE, tn), lambda i, j: (i, j)),
        out_shape=jax.ShapeDtypeStruct((m, n), out_dtype),
        compiler_params=_cparams("parallel", "parallel"),
        name=name,
    )(*ins)


def _q_kernel(h_ref, w_ref, ct_ref, st_ref, o_ref):
    r = _dot(h_ref[...], w_ref[...])
    ct, st = ct_ref[...], st_ref[...]
    for hh in range(r.shape[1] // HEAD_PAD):
        c0 = hh * HEAD_PAD
        o_ref[:, c0:c0 + LANES] = r[:, c0:c0 + LANES].astype(o_ref.dtype)
        o_ref[:, c0 + LANES:c0 + HEAD_PAD] = _rope128(r[:, c0 + LANES:c0 + HEAD_PAD], ct, st).astype(o_ref.dtype)


def _q_proj(h, wq, ct, st, tn=512):
    m, kd = h.shape
    n = wq.shape[1]
    return pl.pallas_call(
        _q_kernel,
        grid=(m // ROW_TILE, n // tn),
        in_specs=[
            pl.BlockSpec((ROW_TILE, kd), lambda i, j: (i, 0)),
            pl.BlockSpec((kd, tn), lambda i, j: (0, j)),
            pl.BlockSpec((ROW_TILE, LANES), lambda i, j: (i, 0)),
            pl.BlockSpec((ROW_TILE, LANES), lambda i, j: (i, 0)),
        ],
        out_specs=pl.BlockSpec((ROW_TILE, tn), lambda i, j: (i, j)),
        out_shape=jax.ShapeDtypeStruct((m, n), BF16),
        compiler_params=_cparams("parallel", "parallel"),
        name="q_proj",
    )(h, wq, ct, st)


def _ckv_kernel(h_ref, w_ref, g_ref, ct_ref, st_ref, c_ref, kr_ref):
    r = _dot(h_ref[...], w_ref[...])
    c_ref[...] = _rms(r[:, :KV_LORA], g_ref[...])
    kr_ref[...] = _rope128(r[:, KV_LORA:], ct_ref[...], st_ref[...])


def _ckv_proj(h, wck, g, ct, st):
    m, kd = h.shape
    n = wck.shape[1]
    return pl.pallas_call(
        _ckv_kernel,
        grid=(m // ROW_TILE,),
        in_specs=[
            pl.BlockSpec((ROW_TILE, kd), lambda i: (i, 0)),
            pl.BlockSpec((kd, n), lambda i: (0, 0)),
            pl.BlockSpec((1, KV_LORA), lambda i: (0, 0)),
            pl.BlockSpec((ROW_TILE, LANES), lambda i: (i, 0)),
            pl.BlockSpec((ROW_TILE, LANES), lambda i: (i, 0)),
        ],
        out_specs=[pl.BlockSpec((ROW_TILE, KV_LORA), lambda i: (i, 0)), pl.BlockSpec((ROW_TILE, LANES), lambda i: (i, 0))],
        out_shape=[jax.ShapeDtypeStruct((m, KV_LORA), F32), jax.ShapeDtypeStruct((m, LANES), F32)],
        compiler_params=_cparams("parallel"),
        name="ckv_proj",
    )(h, wck, g.reshape(1, KV_LORA), ct, st)


def _kvup_kernel(c_ref, kr_ref, w_ref, k_ref, v_ref):
    r = _dot(c_ref[...].astype(BF16), w_ref[...])
    nk = N_HEADS * QK_NOPE
    v_ref[...] = r[:, nk:].astype(v_ref.dtype)
    krb = kr_ref[...].astype(k_ref.dtype)
    for hh in range(N_HEADS):
        k_ref[:, hh * HEAD_PAD:hh * HEAD_PAD + LANES] = r[:, hh * QK_NOPE:(hh + 1) * QK_NOPE].astype(k_ref.dtype)
        k_ref[:, hh * HEAD_PAD + LANES:(hh + 1) * HEAD_PAD] = krb


def _kv_up(ckv, kr, w_ukv, rows, tm=256):
    n = w_ukv.shape[1]
    return pl.pallas_call(
        _kvup_kernel,
        grid=(rows // tm,),
        in_specs=[
            pl.BlockSpec((tm, KV_LORA), lambda i: (i, 0)),
            pl.BlockSpec((tm, LANES), lambda i: (i, 0)),
            pl.BlockSpec((KV_LORA, n), lambda i: (0, 0)),
        ],
        out_specs=[pl.BlockSpec((tm, N_HEADS * HEAD_PAD), lambda i: (i, 0)), pl.BlockSpec((tm, N_HEADS * V_DIM), lambda i: (i, 0))],
        out_shape=[jax.ShapeDtypeStruct((rows, N_HEADS * HEAD_PAD), BF16), jax.ShapeDtypeStruct((rows, N_HEADS * V_DIM), BF16)],
        compiler_params=_cparams("parallel"),
        name="kv_up",
    )(ckv, kr, w_ukv)


def _pattn_kernel(q_ref, k_ref, v_ref, o_ref, m_ref, l_ref, acc_ref):
    qi, ki = pl.program_id(2), pl.program_id(3)

    @pl.when(ki == 0)
    def _():
        m_ref[...] = jnp.full(m_ref.shape, -jnp.inf, F32)
        l_ref[...] = jnp.zeros(l_ref.shape, F32)
        acc_ref[...] = jnp.zeros(acc_ref.shape, F32)

    @pl.when(ki <= qi)
    def _():
        s = _dot_nt(q_ref[...], k_ref[...]) * ATTN_SCALE
        row = lax.broadcasted_iota(jnp.int32, s.shape, 0)
        col = lax.broadcasted_iota(jnp.int32, s.shape, 1)
        s = jnp.where((ki < qi) | (col <= row), s, -jnp.inf)
        m_prev = m_ref[...]
        m_new = jnp.maximum(m_prev, jnp.max(s, axis=-1, keepdims=True))
        alpha = jnp.exp(m_prev - m_new)
        p = jnp.exp(s - m_new)
        l_ref[...] = alpha * l_ref[...] + jnp.sum(p, axis=-1, keepdims=True)
        acc_ref[...] = alpha * acc_ref[...] + _dot(p.astype(BF16), v_ref[...])
        m_ref[...] = m_new

    @pl.when(ki == pl.num_programs(3) - 1)
    def _():
        o_ref[...] = (acc_ref[...] / l_ref[...]).astype(o_ref.dtype)


def _prompt_attn(q, k, v, batch, seq, tq=512):
    nq = seq // tq
    return pl.pallas_call(
        _pattn_kernel,
        grid=(batch, N_HEADS, nq, nq),
        in_specs=[
            pl.BlockSpec((tq, HEAD_PAD), lambda b, h, qi, ki: (b * nq + qi, h)),
            pl.BlockSpec((tq, HEAD_PAD), lambda b, h, qi, ki: (b * nq + jnp.minimum(ki, qi), h)),
            pl.BlockSpec((tq, V_DIM), lambda b, h, qi, ki: (b * nq + jnp.minimum(ki, qi), h)),
        ],
        out_specs=pl.BlockSpec((tq, V_DIM), lambda b, h, qi, ki: (b * nq + qi, h)),
        out_shape=jax.ShapeDtypeStruct((batch * seq, N_HEADS * V_DIM), BF16),
        scratch_shapes=[pltpu.VMEM((tq, 1), F32), pltpu.VMEM((tq, 1), F32), pltpu.VMEM((tq, V_DIM), F32)],
        compiler_params=_cparams("parallel", "parallel", "parallel", "arbitrary"),
        name="prompt_attn",
    )(q, k, v)


def _qlat_kernel(q_ref, w_ref, ql_ref, qr_ref):
    q = q_ref[...]
    ql_ref[...] = _dot(q[:, :QK_NOPE], w_ref[...])
    qr_ref[...] = q[:, LANES:].astype(F32)


def _q_latent(q, w_ukt, row0, rows):
    blk = row0 // rows
    return pl.pallas_call(
        _qlat_kernel,
        grid=(N_HEADS,),
        in_specs=[
            pl.BlockSpec((rows, HEAD_PAD), lambda h: (blk, h)),
            pl.BlockSpec((None, QK_NOPE, KV_LORA), lambda h: (h, 0, 0)),
        ],
        out_specs=[
            pl.BlockSpec((None, rows, KV_LORA), lambda h: (h, 0, 0)),
            pl.BlockSpec((None, rows, LANES), lambda h: (h, 0, 0)),
        ],
        out_shape=[jax.ShapeDtypeStruct((N_HEADS, rows, KV_LORA), F32), jax.ShapeDtypeStruct((N_HEADS, rows, LANES), F32)],
        compiler_params=_cparams("parallel"),
        name="q_latent",
    )(q, w_ukt)


KCAT = KV_LORA + LANES


def _sattn_kernel(pt_ref, ql_ref, qr_ref, cn_ref, krn_ref, *rest, pp, t_new):
    c_refs, kr_refs = rest[:pp], rest[pp:2 * pp]
    o_ref = rest[2 * pp]
    qcat, kcat, ncat, m_ref, l_ref, acc_ref = rest[2 * pp + 1:]
    c = pl.program_id(1)
    rows = N_HEADS * t_new

    def update(s, vals, first):
        if first:
            m_new = jnp.max(s, axis=-1, keepdims=True)
            p = jnp.exp(s - m_new)
            l_ref[...] = jnp.sum(p, axis=-1, keepdims=True)
            acc_ref[...] = _dot(p.astype(BF16), vals)
        else:
            m_prev = m_ref[...]
            m_new = jnp.maximum(m_prev, jnp.max(s, axis=-1, keepdims=True))
            alpha = jnp.exp(m_prev - m_new)
            p = jnp.exp(s - m_new)
            l_ref[...] = alpha * l_ref[...] + jnp.sum(p, axis=-1, keepdims=True)
            acc_ref[...] = alpha * acc_ref[...] + _dot(p.astype(BF16), vals)
        m_ref[...] = m_new

    @pl.when(c == 0)
    def _():
        qcat[:, :KV_LORA] = ql_ref[...].reshape(rows, KV_LORA).astype(BF16)
        qcat[:, KV_LORA:] = qr_ref[...].reshape(rows, LANES).astype(BF16)
        kcat[:, KV_LORA + QK_ROPE:] = jnp.zeros((kcat.shape[0], LANES - QK_ROPE), BF16)
        ncat[...] = jnp.zeros(ncat.shape, BF16)
        ncat[0:t_new, :KV_LORA] = cn_ref[...].astype(BF16)
        ncat[0:t_new, KV_LORA:] = krn_ref[...].astype(BF16)
        s = _dot_nt(qcat[...], ncat[...]) * ATTN_SCALE
        row = lax.broadcasted_iota(jnp.int32, s.shape, 0)
        col = lax.broadcasted_iota(jnp.int32, s.shape, 1)
        s = jnp.where(col <= row % t_new, s, -jnp.inf)
        update(s, ncat[:, :KV_LORA], True)

    for k in range(pp):
        kcat[k * PAGE_SIZE:(k + 1) * PAGE_SIZE, :KV_LORA] = c_refs[k][...].astype(BF16)
        kcat[k * PAGE_SIZE:(k + 1) * PAGE_SIZE, KV_LORA:KV_LORA + QK_ROPE] = kr_refs[k][...].astype(BF16)
    s = _dot_nt(qcat[...], kcat[...]) * ATTN_SCALE
    update(s, kcat[:, :KV_LORA], False)

    @pl.when(c == pl.num_programs(1) - 1)
    def _():
        o_ref[...] = (acc_ref[...] / l_ref[...]).reshape(o_ref.shape)


def _sample_attn(qlat, qr, ckv, kr, cache_c, cache_kr, page_table, row0, t_new, pp=8):
    n_seq, n_pages = page_table.shape
    rows = N_HEADS * t_new
    blk0 = row0 // t_new

    def page_map(k):
        return lambda s, c, pt: (pt[s, c * pp + k], 0, 0)

    in_specs = [
        pl.BlockSpec((N_HEADS, t_new, KV_LORA), lambda s, c, pt: (0, s, 0)),
        pl.BlockSpec((N_HEADS, t_new, LANES), lambda s, c, pt: (0, s, 0)),
        pl.BlockSpec((t_new, KV_LORA), lambda s, c, pt: (blk0 + s, 0)),
        pl.BlockSpec((t_new, LANES), lambda s, c, pt: (blk0 + s, 0)),
    ]
    in_specs += [pl.BlockSpec((None, PAGE_SIZE, KV_LORA), page_map(k)) for k in range(pp)]
    in_specs += [pl.BlockSpec((None, PAGE_SIZE, QK_ROPE), page_map(k)) for k in range(pp)]
    grid_spec = pltpu.PrefetchScalarGridSpec(
        num_scalar_prefetch=1,
        grid=(n_seq, n_pages // pp),
        in_specs=in_specs,
        out_specs=pl.BlockSpec((N_HEADS, t_new, KV_LORA), lambda s, c, pt: (0, s, 0)),
        scratch_shapes=[
            pltpu.VMEM((rows, KCAT), BF16),
            pltpu.VMEM((pp * PAGE_SIZE, KCAT), BF16),
            pltpu.VMEM((PAGE_SIZE, KCAT), BF16),
            pltpu.VMEM((rows, 1), F32),
            pltpu.VMEM((rows, 1), F32),
            pltpu.VMEM((rows, KV_LORA), F32),
        ],
    )
    return pl.pallas_call(
        functools.partial(_sattn_kernel, pp=pp, t_new=t_new),
        grid_spec=grid_spec,
        out_shape=jax.ShapeDtypeStruct((N_HEADS, n_seq * t_new, KV_LORA), F32),
        compiler_params=_cparams("parallel", "arbitrary"),
        name="sample_attn",
    )(page_table, qlat, qr, ckv, kr, *([cache_c] * pp), *([cache_kr] * pp))


def _ouv_kernel(o_ref, w_ref, out_ref):
    out_ref[...] = _dot(o_ref[...].astype(BF16), w_ref[...]).astype(out_ref.dtype)


def _o_uv(olat, w_uv2):
    _, rows, _ = olat.shape
    return pl.pallas_call(
        _ouv_kernel,
        grid=(N_HEADS,),
        in_specs=[
            pl.BlockSpec((None, rows, KV_LORA), lambda h: (h, 0, 0)),
            pl.BlockSpec((KV_LORA, V_DIM), lambda h: (0, h)),
        ],
        out_specs=pl.BlockSpec((rows, V_DIM), lambda h: (0, h)),
        out_shape=jax.ShapeDtypeStruct((rows, N_HEADS * V_DIM), BF16),
        compiler_params=_cparams("parallel"),
        name="o_uv",
    )(olat, w_uv2)


HALO = 2 * SUBLANES


def _pool_groups(load, u_of, cnt_of, wg_ref, ps_ref, store):
    group = wg_ref.shape[1]
    for g, w in enumerate(POOL_WINDOWS):
        sl = slice(g * group, (g + 1) * group)
        ws = load(0, sl)
        for k in range(1, w):
            ws = ws + load(k, sl)
        pooled = ws / cnt_of(w) - u_of(sl)
        pooled = pooled.reshape(-1, group).astype(BF16)
        store(sl, (_dot(pooled, wg_ref[g]) * ps_ref[:, sl]))


def _pool_prompt_kernel(u_ref, wg_ref, ps_ref, y_ref, ext_ref, *, tiles_per_seq):
    i = pl.program_id(0)
    tm = u_ref.shape[0]
    it = i % tiles_per_seq

    @pl.when(it == 0)
    def _():
        ext_ref[0:HALO, :] = jnp.zeros((HALO, ext_ref.shape[1]), F32)

    @pl.when(it != 0)
    def _():
        ext_ref[0:HALO, :] = ext_ref[tm:tm + HALO, :]

    ext_ref[HALO:HALO + tm, :] = u_ref[...]
    pos = it * tm + lax.broadcasted_iota(jnp.int32, (tm, 1), 0)

    def store(sl, val):
        y_ref[:, sl] = val.astype(y_ref.dtype)

    _pool_groups(
        lambda k, sl: ext_ref[HALO - k:HALO - k + tm, sl],
        lambda sl: u_ref[:, sl],
        lambda w: jnp.minimum(pos + 1, w).astype(F32),
        wg_ref, ps_ref, store)


def _pool_prompt(u, wg, ps, rows, seq):
    width = wg.shape[0] * wg.shape[1]
    tm = ROW_TILE
    return pl.pallas_call(
        functools.partial(_pool_prompt_kernel, tiles_per_seq=seq // tm),
        grid=(rows // tm,),
        in_specs=[
            pl.BlockSpec((tm, width), lambda i: (i, 0)),
            pl.BlockSpec(wg.shape, lambda i: (0, 0, 0)),
            pl.BlockSpec((1, width), lambda i: (0, 0)),
        ],
        out_specs=pl.BlockSpec((tm, width), lambda i: (i, 0)),
        out_shape=jax.ShapeDtypeStruct((rows, width), BF16),
        scratch_shapes=[pltpu.VMEM((HALO + tm, width), F32)],
        compiler_params=_cparams("arbitrary"),
        name="pool_prompt",
    )(u, wg, ps)


def _pool_sample_kernel(u_ref, st_ref, wg_ref, ps_ref, y_ref, ext_ref, *, pos0):
    t_new = u_ref.shape[1]
    ext_ref[:, 0:HALO, :] = st_ref[...]
    ext_ref[:, HALO:HALO + t_new, :] = u_ref[...]
    pos = pos0 + lax.broadcasted_iota(jnp.int32, (1, t_new, 1), 1)

    def store(sl, val):
        y_ref[:, sl] = val.astype(y_ref.dtype)

    _pool_groups(
        lambda k, sl: ext_ref[:, HALO - k:HALO - k + t_new, sl],
        lambda sl: u_ref[:, :, sl],
        lambda w: jnp.minimum(pos + 1, w).astype(F32),
        wg_ref, ps_ref, store)


def _pool_sample(u3, st, wg, ps, pos0, ns=64):
    n_seq, t_new, width = u3.shape
    return pl.pallas_call(
        functools.partial(_pool_sample_kernel, pos0=pos0),
        grid=(n_seq // ns,),
        in_specs=[
            pl.BlockSpec((ns, t_new, width), lambda i: (i, 0, 0)),
            pl.BlockSpec((ns, HALO, width), lambda i: (i, 0, 0)),
            pl.BlockSpec(wg.shape, lambda i: (0, 0, 0)),
            pl.BlockSpec((1, width), lambda i: (0, 0)),
        ],
        out_specs=pl.BlockSpec((ns * t_new, width), lambda i: (i, 0)),
        out_shape=jax.ShapeDtypeStruct((n_seq * t_new, width), BF16),
        scratch_shapes=[pltpu.VMEM((ns, HALO + t_new, width), F32)],
        compiler_params=_cparams("parallel"),
        name="pool_sample",
    )(u3, st, wg, ps)


def _merge_kernel(oa_ref, yp_ref, wa_ref, wp_ref, ga_ref, gb_ref, o_ref):
    a = _dot(oa_ref[...], wa_ref[...])
    p = _dot(yp_ref[...], wp_ref[...])
    o_ref[...] = (ga_ref[...] * a + gb_ref[...] * p).astype(o_ref.dtype)


def _merge(oa, yp, wa, wp, gates, tn=512):
    m, ka = oa.shape
    kp = yp.shape[1]
    n = wa.shape[1]
    nb = n // tn
    return pl.pallas_call(
        _merge_kernel,
        grid=(m // ROW_TILE, nb),
        in_specs=[
            pl.BlockSpec((ROW_TILE, ka), lambda i, j: (i, 0)),
            pl.BlockSpec((ROW_TILE, kp), lambda i, j: (i, 0)),
            pl.BlockSpec((ka, tn), lambda i, j: (0, j)),
            pl.BlockSpec((kp, tn), lambda i, j: (0, j)),
            pl.BlockSpec((ROW_TILE, tn), lambda i, j: (i, j)),
            pl.BlockSpec((ROW_TILE, tn), lambda i, j: (i, nb + j)),
        ],
        out_specs=pl.BlockSpec((ROW_TILE, tn), lambda i, j: (i, j)),
        out_shape=jax.ShapeDtypeStruct((m, n), BF16),
        compiler_params=_cparams("parallel", "parallel"),
        name="merge",
    )(oa, yp, wa, wp, gates, gates)


def _ffn_kernel(*refs, sample, tiles_per_seq, t_new):
    if sample:
        (x_ref, g_ref, wa_ref, wv_ref, cw_ref, cb_ref, wd_ref, gf_ref, p1_ref, p2_ref,
         y_ref, a_ref, hn_ref, acc_ref, ext_ref) = refs
    else:
        (x_ref, g_ref, wa_ref, wv_ref, cw_ref, cb_ref, wd_ref, gf_ref,
         y_ref, tail_ref, hn_ref, acc_ref, ext_ref, carry_ref) = refs
    i, f = pl.program_id(0), pl.program_id(1)
    tm = x_ref.shape[0]

    @pl.when(f == 0)
    def _():
        hn_ref[...] = _rms(x_ref[...], g_ref[...]).astype(hn_ref.dtype)
        acc_ref[...] = jnp.zeros(acc_ref.shape, F32)

    hn = hn_ref[...]
    a = _dot(hn, wa_ref[...])
    v = _dot(hn, wv_ref[...])
    ext_ref[SUBLANES:SUBLANES + tm, :] = a
    if sample:
        ext_ref[0:SUBLANES, :] = jnp.zeros((SUBLANES, ext_ref.shape[1]), F32)
        rmod = lax.broadcasted_iota(jnp.int32, (tm, 1), 0) % t_new
        s1 = jnp.where(rmod >= 1, ext_ref[SUBLANES - 1:SUBLANES - 1 + tm, :], p1_ref[...])
        s2 = jnp.where(rmod >= 2, ext_ref[SUBLANES - 2:SUBLANES - 2 + tm, :], p2_ref[...])
        a_ref[...] = a
    else:
        first = i % tiles_per_seq == 0

        @pl.when(first)
        def _():
            ext_ref[0:SUBLANES, :] = jnp.zeros((SUBLANES, ext_ref.shape[1]), F32)

        @pl.when(jnp.logical_not(first))
        def _():
            ext_ref[0:SUBLANES, :] = carry_ref[f]

        tail = a[tm - SUBLANES:, :]
        carry_ref[f] = tail
        tail_ref[...] = tail
        s1 = ext_ref[SUBLANES - 1:SUBLANES - 1 + tm, :]
        s2 = ext_ref[SUBLANES - 2:SUBLANES - 2 + tm, :]
    conv = cb_ref[...] + cw_ref[0:1, :] * s2
    conv = conv + cw_ref[1:2, :] * s1
    conv = conv + cw_ref[2:3, :] * a
    hid = (jax.nn.silu(conv) * v).astype(BF16)
    acc_ref[...] += _dot(hid, wd_ref[...])

    @pl.when(f == pl.num_programs(1) - 1)
    def _():
        x2 = x_ref[...] + acc_ref[...]
        y_ref[...] = _rms(x2, gf_ref[...])


def _ffn(x, g, w_up, conv_w, conv_b, w_down, gf, *, seq=None, prev=None, t_new=None, tf=512):
    m, d = x.shape
    d_ff = w_down.shape[0]
    nf = d_ff // tf
    tm = ROW_TILE
    sample = prev is not None
    in_specs = [
        pl.BlockSpec((tm, d), lambda i, f: (i, 0)),
        pl.BlockSpec((1, d), lambda i, f: (0, 0)),
        pl.BlockSpec((d, tf), lambda i, f: (0, f)),
        pl.BlockSpec((d, tf), lambda i, f: (0, nf + f)),
        pl.BlockSpec((CONV_W, tf), lambda i, f: (0, f)),
        pl.BlockSpec((1, tf), lambda i, f: (0, f)),
        pl.BlockSpec((tf, d), lambda i, f: (f, 0)),
        pl.BlockSpec((1, d), lambda i, f: (0, 0)),
    ]
    ins = [x, g.reshape(1, d), w_up, w_up, conv_w, conv_b.reshape(1, d_ff), w_down, gf.reshape(1, d)]
    scratch = [pltpu.VMEM((tm, d), BF16), pltpu.VMEM((tm, d), F32), pltpu.VMEM((SUBLANES + tm, tf), F32)]
    y_spec = pl.BlockSpec((tm, d), lambda i, f: (i, 0))
    y_shape = jax.ShapeDtypeStruct((m, d), F32)
    if sample:
        in_specs += [pl.BlockSpec((tm, tf), lambda i, f: (i, f))] * 2
        ins += list(prev)
        out_specs = [y_spec, pl.BlockSpec((tm, tf), lambda i, f: (i, f))]
        out_shape = [y_shape, jax.ShapeDtypeStruct((m, d_ff), F32)]
        tiles_per_seq = None
    else:
        out_specs = [y_spec, pl.BlockSpec((None, SUBLANES, tf), lambda i, f: (i, 0, f))]
        out_shape = [y_shape, jax.ShapeDtypeStruct((m // tm, SUBLANES, d_ff), F32)]
        scratch.append(pltpu.VMEM((nf, SUBLANES, tf), F32))
        tiles_per_seq = seq // tm
    return pl.pallas_call(
        functools.partial(_ffn_kernel, sample=sample, tiles_per_seq=tiles_per_seq, t_new=t_new),
        grid=(m // tm, nf),
        in_specs=in_specs,
        out_specs=out_specs,
        out_shape=out_shape,
        scratch_shapes=scratch,
        compiler_params=_cparams("arbitrary", "arbitrary"),
        name="ffn_sample" if sample else "ffn_prompt",
    )(*ins)


def _rope_tables(pos):
    half = QK_ROPE // 2
    inv = ROPE_BASE ** (-jnp.arange(half, dtype=F32) / half)
    ang = pos[:, None] * inv[None, :]
    c, s = jnp.cos(ang), jnp.sin(ang)
    n = pos.shape[0]
    ct = jnp.concatenate([c, c, jnp.ones((n, LANES - QK_ROPE), F32)], axis=1)
    st = jnp.concatenate([-s, s, jnp.zeros((n, LANES - QK_ROPE), F32)], axis=1)
    return ct, st


def _layer(x, ct, st, dims, cache_c, cache_kr, page_table, state_pool, state_conv,
           norm_attn_g, w_in, b_gates, kv_norm_g, w_uk, w_uv, w_attn_out, w_pool_group, pool_scale,
           w_pool_out, w_o, norm_ffn_g, w_up, conv_w, conv_b, w_down, norm_final_g):
    batch, seq, n_seq, t_new, n_past = dims
    mp = batch * seq
    d = x.shape[1]
    q_cols = N_HEADS * (QK_NOPE + QK_ROPE)
    pool_w = w_pool_out.shape[0]
    c0, c1, c2 = q_cols, q_cols + KV_LORA, q_cols + KV_LORA + QK_ROPE
    c3 = c2 + pool_w

    wq = jnp.pad(w_in[:, :c0].reshape(d, N_HEADS, QK_NOPE + QK_ROPE), ((0, 0), (0, 0), (0, HEAD_PAD - QK_NOPE - QK_ROPE)))
    wq = wq.reshape(d, N_HEADS * HEAD_PAD).astype(BF16)
    wck = jnp.pad(w_in[:, c0:c2], ((0, 0), (0, LANES - QK_ROPE))).astype(BF16)
    wu = w_in[:, c2:c3].astype(BF16)
    wg = w_in[:, c3:].astype(BF16)
    w_ukv = jnp.concatenate([w_uk.reshape(KV_LORA, -1), w_uv.reshape(KV_LORA, -1)], axis=1).astype(BF16)
    w_ukt = jnp.transpose(w_uk, (1, 2, 0)).astype(BF16)
    w_uv2 = w_uv.reshape(KV_LORA, -1).astype(BF16)

    h = _norm(x, norm_attn_g)
    q = _q_proj(h, wq, ct, st)
    ckv, kr = _ckv_proj(h, wck, kv_norm_g, ct, st)
    u = _mm(h, wu, name="pool_proj")
    gates = _mm(h, wg, bias=b_gates.reshape(-1), act="sigmoid", name="gate_proj")

    kp, vp = _kv_up(ckv, kr, w_ukv, mp)
    o_p = _prompt_attn(q, kp, vp, batch, seq)
    qlat, qr = _q_latent(q, w_ukt, mp, n_seq * t_new)
    olat = _sample_attn(qlat, qr, ckv, kr, cache_c, cache_kr, page_table, mp, t_new)
    o_s = _o_uv(olat, w_uv2)
    o_attn = jnp.concatenate([o_p, o_s], axis=0)

    wpg = w_pool_group.astype(BF16)
    ps = pool_scale.reshape(1, pool_w)
    y_p = _pool_prompt(u, wpg, ps, mp, seq)
    u_s = u[mp:].reshape(n_seq, t_new, pool_w)
    st_pad = jnp.pad(state_pool, ((0, 0), (HALO - POOL_BUF, 0), (0, 0)))
    y_s = _pool_sample(u_s, st_pad, wpg, ps, n_past)
    y_pool = jnp.concatenate([y_p, y_s], axis=0)

    merged = _merge(o_attn, y_pool, w_attn_out.astype(BF16), w_pool_out.astype(BF16), gates)
    x1 = _mm(merged, w_o.astype(BF16), res=x, name="o_proj")

    w_up_b = w_up.astype(BF16)
    w_down_b = w_down.astype(BF16)
    y_prompt, tails = _ffn(x1[:mp], norm_ffn_g, w_up_b, conv_w, conv_b, w_down_b, norm_final_g, seq=seq)
    d_ff = w_down.shape[0]
    prev1 = jnp.pad(state_conv[:, CONV_W - 2:CONV_W - 1], ((0, 0), (0, t_new - 1), (0, 0))).reshape(n_seq * t_new, d_ff)
    prev2 = jnp.pad(state_conv, ((0, 0), (0, t_new - (CONV_W - 1)), (0, 0))).reshape(n_seq * t_new, d_ff)
    y_sample, a_s = _ffn(x1[mp:], norm_ffn_g, w_up_b, conv_w, conv_b, w_down_b, norm_final_g,
                              prev=(prev1, prev2), t_new=t_new)

    tiles_per_seq = seq // ROW_TILE
    conv_p = tails[tiles_per_seq - 1::tiles_per_seq, SUBLANES - (CONV_W - 1):]
    conv_s = a_s.reshape(n_seq, t_new, d_ff)[:, t_new - (CONV_W - 1):]
    pool_p = u[:mp].reshape(batch, seq, pool_w)[:, seq - POOL_BUF:]
    pool_s = jnp.concatenate([state_pool, u_s], axis=1)[:, -POOL_BUF:]
    return (y_prompt, y_sample, ckv[:mp].reshape(batch, seq, KV_LORA), kr[:mp, :QK_ROPE].reshape(batch, seq, QK_ROPE),
            ckv[mp:].reshape(n_seq, t_new, KV_LORA), kr[mp:, :QK_ROPE].reshape(n_seq, t_new, QK_ROPE),
            pool_p, pool_s, conv_p, conv_s)


def kernel(x_prompt, x_sample, cache_kv_latent, cache_k_rope, page_table, state_pool, state_conv, norm_attn_g, w_in, b_gates, kv_norm_g, w_uk, w_uv, w_attn_out, w_pool_group, pool_scale, w_pool_out, w_o, norm_ffn_g, w_up, conv_w, conv_b, w_down, norm_final_g):
    batch, seq, d = x_prompt.shape
    n_seq, t_new, _ = x_sample.shape
    depth = w_in.shape[0]
    assert depth == 1, "the stacked-row pipeline below is written for a single layer"
    n_past = page_table.shape[1] * PAGE_SIZE
    dims = (batch, seq, n_seq, t_new, n_past)
    x = jnp.concatenate([x_prompt.reshape(batch * seq, d), x_sample.reshape(n_seq * t_new, d)], axis=0)
    pos = jnp.concatenate([jnp.tile(jnp.arange(seq, dtype=F32), batch),
                           jnp.tile(n_past + jnp.arange(t_new, dtype=F32), n_seq)])
    ct, st = _rope_tables(pos)
    l = 0
    cache_c = cache_kv_latent.reshape(cache_kv_latent.shape[1:])
    cache_kr = cache_k_rope.reshape(cache_k_rope.shape[1:])
    outs = _layer(x, ct, st, dims, cache_c, cache_kr, page_table, state_pool[l], state_conv[l],
                  norm_attn_g[l], w_in[l], b_gates[l], kv_norm_g[l], w_uk[l], w_uv[l], w_attn_out[l],
                  w_pool_group[l], pool_scale[l], w_pool_out[l], w_o[l], norm_ffn_g[l], w_up[l], conv_w[l],
                  conv_b[l], w_down[l], norm_final_g)
    y_p, y_s = outs[0].reshape(batch, seq, d), outs[1].reshape(n_seq, t_new, d)
    return (y_p, y_s) + tuple(o[None] for o in outs[2:])
```

```python
import functools
import math

import jax
import jax.numpy as jnp
from jax import lax
from jax.experimental import pallas as pl
from jax.experimental.pallas import tpu as pltpu

F32 = jnp.float32
BF16 = jnp.bfloat16

N_HEADS = 16
QK_NOPE = 128
QK_ROPE = 64
V_DIM = 128
KV_LORA = 512
ROPE_BASE = 10000.0
ATTN_SCALE = (QK_NOPE + QK_ROPE) ** -0.5
EXP2_SCALE = ATTN_SCALE * math.log2(math.e)
PAGE_SIZE = 128
POOL_WINDOWS = (2, 4, 8, 16)
POOL_BUF = max(POOL_WINDOWS) - 1
CONV_W = 3
EPS = 1e-6

LANES = 128
SUBLANES = 8
HEAD_PAD = 2 * LANES
VMEM_LIMIT = 48 * 1024 * 1024

ROW_TILE = 512


def _cparams(*sem):
    return pltpu.CompilerParams(dimension_semantics=sem, vmem_limit_bytes=VMEM_LIMIT)


def _dot(a, b):
    return jnp.dot(a, b, preferred_element_type=F32)


def _dot_nt(a, b):
    return lax.dot_general(a, b, (((1,), (1,)), ((), ())), preferred_element_type=F32)


def _rms(x, g):
    ms = jnp.mean(x * x, axis=-1, keepdims=True)
    return x * lax.rsqrt(ms + EPS) * g


def _rope128(x, ct, st):
    lane = lax.broadcasted_iota(jnp.int32, x.shape, 1)
    other = jnp.where(lane < QK_ROPE // 2, pltpu.roll(x, LANES - QK_ROPE // 2, 1), pltpu.roll(x, QK_ROPE // 2, 1))
    return x * ct + other * st


def _softmax_step(s, vals, state):
    bm = jnp.max(s, axis=-1, keepdims=True)
    if state is None:
        m_new = bm
        p = jnp.exp2((s - m_new) * EXP2_SCALE)
        l_new = jnp.sum(p, axis=-1, keepdims=True)
        acc_new = _dot(p.astype(BF16), vals)
    else:
        m, l, acc = state
        m_new = jnp.maximum(m, bm)
        alpha = jnp.exp2((m - m_new) * EXP2_SCALE)
        p = jnp.exp2((s - m_new) * EXP2_SCALE)
        l_new = alpha * l + jnp.sum(p, axis=-1, keepdims=True)
        acc_new = alpha * acc + _dot(p.astype(BF16), vals)
    return m_new, l_new, acc_new


def _prompt_rows(n_prompt):
    return lambda i: jnp.minimum(i, n_prompt - 1)


def _sample_rows(n_prompt):
    return lambda i: jnp.maximum(i - n_prompt, 0)


def _norm_kernel(xp_ref, xs_ref, g_ref, o_ref, *, n_prompt):
    i = pl.program_id(0)

    @pl.when(i < n_prompt)
    def _():
        o_ref[...] = _rms(xp_ref[...], g_ref[...]).astype(o_ref.dtype)

    @pl.when(i >= n_prompt)
    def _():
        o_ref[...] = _rms(xs_ref[...], g_ref[...]).astype(o_ref.dtype)


def _norm(xp, xs, g):
    d = xp.shape[1]
    n_prompt, n_sample = xp.shape[0] // ROW_TILE, xs.shape[0] // ROW_TILE
    pr, sr = _prompt_rows(n_prompt), _sample_rows(n_prompt)
    return pl.pallas_call(
        functools.partial(_norm_kernel, n_prompt=n_prompt),
        grid=(n_prompt + n_sample,),
        in_specs=[
            pl.BlockSpec((ROW_TILE, d), lambda i: (pr(i), 0)),
            pl.BlockSpec((ROW_TILE, d), lambda i: (sr(i), 0)),
            pl.BlockSpec((1, d), lambda i: (0, 0)),
        ],
        out_specs=pl.BlockSpec((ROW_TILE, d), lambda i: (i, 0)),
        out_shape=jax.ShapeDtypeStruct((xp.shape[0] + xs.shape[0], d), BF16),
        compiler_params=_cparams("arbitrary"),
        name="rmsnorm",
    )(xp, xs, g.reshape(1, d))


def _mm_kernel(*refs, n_prompt, has_res):
    h_ref, w_ref = refs[0], refs[1]
    o_ref = refs[-1]
    r = _dot(h_ref[...], w_ref[...])
    if has_res:
        i = pl.program_id(1)

        @pl.when(i < n_prompt)
        def _():
            o_ref[...] = refs[2][...] + r

        @pl.when(i >= n_prompt)
        def _():
            o_ref[...] = refs[3][...] + r
    else:
        o_ref[...] = r.astype(o_ref.dtype)


def _mm(h, w, *, res=None, n_prompt=None, out_dtype=F32, tn=512, name="mm"):
    m, kd = h.shape
    n = w.shape[1]
    ins = [h, w]
    specs = [pl.BlockSpec((ROW_TILE, kd), lambda j, i: (i, 0)), pl.BlockSpec((kd, tn), lambda j, i: (0, j))]
    if res is not None:
        pr, sr = _prompt_rows(n_prompt), _sample_rows(n_prompt)
        ins += list(res)
        specs += [pl.BlockSpec((ROW_TILE, tn), lambda j, i: (pr(i), j)), pl.BlockSpec((ROW_TILE, tn), lambda j, i: (sr(i), j))]
    return pl.pallas_call(
        functools.partial(_mm_kernel, n_prompt=n_prompt, has_res=res is not None),
        grid=(n // tn, m // ROW_TILE),
        in_specs=specs,
        out_specs=pl.BlockSpec((ROW_TILE, tn), lambda j, i: (i, j)),
        out_shape=jax.ShapeDtypeStruct((m, n), out_dtype),
        compiler_params=_cparams("arbitrary", "arbitrary"),
        name=name,
    )(*ins)


def _q_kernel(h_ref, w_ref, tab_ref, o_ref):
    r = _dot(h_ref[...], w_ref[...])
    ct, st = tab_ref[:, :LANES], tab_ref[:, LANES:]
    for hh in range(r.shape[1] // HEAD_PAD):
        c0 = hh * HEAD_PAD
        o_ref[:, c0:c0 + LANES] = r[:, c0:c0 + LANES].astype(o_ref.dtype)
        o_ref[:, c0 + LANES:c0 + HEAD_PAD] = _rope128(r[:, c0 + LANES:c0 + HEAD_PAD], ct, st).astype(o_ref.dtype)


def _q_proj(h, wq, tab, tab_row, tn=1024):
    m, kd = h.shape
    n = wq.shape[1]
    return pl.pallas_call(
        _q_kernel,
        grid=(n // tn, m // ROW_TILE),
        in_specs=[
            pl.BlockSpec((ROW_TILE, kd), lambda j, i: (i, 0)),
            pl.BlockSpec((kd, tn), lambda j, i: (0, j)),
            pl.BlockSpec((ROW_TILE, 2 * LANES), lambda j, i: (tab_row(i), 0)),
        ],
        out_specs=pl.BlockSpec((ROW_TILE, tn), lambda j, i: (i, j)),
        out_shape=jax.ShapeDtypeStruct((m, n), BF16),
        compiler_params=_cparams("arbitrary", "arbitrary"),
        name="q_proj",
    )(h, wq, tab)


def _ckv_kernel(h_ref, w_ref, g_ref, tab_ref, cp_ref, cs_ref, krp_ref, krs_ref, *, n_prompt):
    i = pl.program_id(0)
    r = _dot(h_ref[...], w_ref[...])
    c = _rms(r[:, :KV_LORA], g_ref[...])
    kr = _rope128(r[:, KV_LORA:], tab_ref[:, :LANES], tab_ref[:, LANES:])

    @pl.when(i < n_prompt)
    def _():
        cp_ref[...] = c
        krp_ref[...] = kr

    @pl.when(i >= n_prompt)
    def _():
        cs_ref[...] = c
        krs_ref[...] = kr


def _ckv_proj(h, wck, g, tab, tab_row, mp, ms):
    m, kd = h.shape
    n = wck.shape[1]
    n_prompt = mp // ROW_TILE
    pr, sr = _prompt_rows(n_prompt), _sample_rows(n_prompt)
    return pl.pallas_call(
        functools.partial(_ckv_kernel, n_prompt=n_prompt),
        grid=(m // ROW_TILE,),
        in_specs=[
            pl.BlockSpec((ROW_TILE, kd), lambda i: (i, 0)),
            pl.BlockSpec((kd, n), lambda i: (0, 0)),
            pl.BlockSpec((1, KV_LORA), lambda i: (0, 0)),
            pl.BlockSpec((ROW_TILE, 2 * LANES), lambda i: (tab_row(i), 0)),
        ],
        out_specs=[
            pl.BlockSpec((ROW_TILE, KV_LORA), lambda i: (pr(i), 0)),
            pl.BlockSpec((ROW_TILE, KV_LORA), lambda i: (sr(i), 0)),
            pl.BlockSpec((ROW_TILE, LANES), lambda i: (pr(i), 0)),
            pl.BlockSpec((ROW_TILE, LANES), lambda i: (sr(i), 0)),
        ],
        out_shape=[
            jax.ShapeDtypeStruct((mp, KV_LORA), F32), jax.ShapeDtypeStruct((ms, KV_LORA), F32),
            jax.ShapeDtypeStruct((mp, LANES), F32), jax.ShapeDtypeStruct((ms, LANES), F32),
        ],
        compiler_params=_cparams("arbitrary"),
        name="ckv_proj",
    )(h, wck, g.reshape(1, KV_LORA), tab)


def _kvup_kernel(c_ref, kr_ref, w_ref, k_ref, v_ref):
    r = _dot(c_ref[...].astype(BF16), w_ref[...])
    nk = N_HEADS * QK_NOPE
    v_ref[...] = r[:, nk:].astype(v_ref.dtype)
    krb = kr_ref[...].astype(k_ref.dtype)
    for hh in range(N_HEADS):
        k_ref[:, hh * HEAD_PAD:hh * HEAD_PAD + LANES] = r[:, hh * QK_NOPE:(hh + 1) * QK_NOPE].astype(k_ref.dtype)
        k_ref[:, hh * HEAD_PAD + LANES:(hh + 1) * HEAD_PAD] = krb


def _kv_up(ckv, kr, w_ukv, tm=256):
    rows = ckv.shape[0]
    n = w_ukv.shape[1]
    return pl.pallas_call(
        _kvup_kernel,
        grid=(rows // tm,),
        in_specs=[
            pl.BlockSpec((tm, KV_LORA), lambda i: (i, 0)),
            pl.BlockSpec((tm, LANES), lambda i: (i, 0)),
            pl.BlockSpec((KV_LORA, n), lambda i: (0, 0)),
        ],
        out_specs=[pl.BlockSpec((tm, N_HEADS * HEAD_PAD), lambda i: (i, 0)), pl.BlockSpec((tm, N_HEADS * V_DIM), lambda i: (i, 0))],
        out_shape=[jax.ShapeDtypeStruct((rows, N_HEADS * HEAD_PAD), BF16), jax.ShapeDtypeStruct((rows, N_HEADS * V_DIM), BF16)],
        compiler_params=_cparams("arbitrary"),
        name="kv_up",
    )(ckv, kr, w_ukv)


def _pattn_kernel(q_ref, k_ref, v_ref, o_ref, *, tq):
    nq = q_ref.shape[0] // tq
    row = lax.broadcasted_iota(jnp.int32, (tq, tq), 0)
    col = lax.broadcasted_iota(jnp.int32, (tq, tq), 1)
    for qi in range(nq):
        q = q_ref[qi * tq:(qi + 1) * tq, :]
        state = None
        for ki in range(qi + 1):
            s = _dot_nt(q, k_ref[ki * tq:(ki + 1) * tq, :])
            if ki == qi:
                s = jnp.where(col <= row, s, -jnp.inf)
            state = _softmax_step(s, v_ref[ki * tq:(ki + 1) * tq, :], state)
        _, l, acc = state
        o_ref[qi * tq:(qi + 1) * tq, :] = (acc / l).astype(o_ref.dtype)


def _prompt_attn(q, k, v, batch, seq, tq=512):
    return pl.pallas_call(
        functools.partial(_pattn_kernel, tq=tq),
        grid=(batch, N_HEADS),
        in_specs=[
            pl.BlockSpec((seq, HEAD_PAD), lambda b, h: (b, h)),
            pl.BlockSpec((seq, HEAD_PAD), lambda b, h: (b, h)),
            pl.BlockSpec((seq, V_DIM), lambda b, h: (b, h)),
        ],
        out_specs=pl.BlockSpec((seq, V_DIM), lambda b, h: (b, h)),
        out_shape=jax.ShapeDtypeStruct((batch * seq, N_HEADS * V_DIM), BF16),
        compiler_params=_cparams("arbitrary", "arbitrary"),
        name="prompt_attn",
    )(q, k, v)


def _qlat_kernel(q_ref, w_ref, ql_ref, qr_ref):
    q = q_ref[...]
    ql_ref[...] = _dot(q[:, :QK_NOPE], w_ref[...])
    qr_ref[...] = q[:, LANES:].astype(F32)


def _q_latent(q, w_ukt, row0, rows):
    blk = row0 // rows
    return pl.pallas_call(
        _qlat_kernel,
        grid=(N_HEADS,),
        in_specs=[
            pl.BlockSpec((rows, HEAD_PAD), lambda h: (blk, h)),
            pl.BlockSpec((None, QK_NOPE, KV_LORA), lambda h: (h, 0, 0)),
        ],
        out_specs=[
            pl.BlockSpec((None, rows, KV_LORA), lambda h: (h, 0, 0)),
            pl.BlockSpec((None, rows, LANES), lambda h: (h, 0, 0)),
        ],
        out_shape=[jax.ShapeDtypeStruct((N_HEADS, rows, KV_LORA), F32), jax.ShapeDtypeStruct((N_HEADS, rows, LANES), F32)],
        compiler_params=_cparams("arbitrary"),
        name="q_latent",
    )(q, w_ukt)


KCAT = KV_LORA + LANES


def _sattn_kernel(pt_ref, ql_ref, qr_ref, cn_ref, krn_ref, *rest, pp, sub, t_new):
    c_refs, kr_refs = rest[:pp], rest[pp:2 * pp]
    o_ref = rest[2 * pp]
    qcat, ncat, m_ref, l_ref, acc_ref = rest[2 * pp + 1:]
    step = pl.program_id(1)
    rows = N_HEADS * t_new

    @pl.when(step == 0)
    def _():
        qcat[:, :KV_LORA] = ql_ref[...].reshape(rows, KV_LORA).astype(BF16)
        qcat[:, KV_LORA:] = qr_ref[...].reshape(rows, LANES).astype(BF16)
        ncat[...] = jnp.zeros(ncat.shape, BF16)
        ncat[0:t_new, :KV_LORA] = cn_ref[...].astype(BF16)
        ncat[0:t_new, KV_LORA:] = krn_ref[...].astype(BF16)
        s = _dot_nt(qcat[...], ncat[...])
        row = lax.broadcasted_iota(jnp.int32, s.shape, 0)
        col = lax.broadcasted_iota(jnp.int32, s.shape, 1)
        s = jnp.where(col <= row % t_new, s, -jnp.inf)
        m0, l0, acc0 = _softmax_step(s, ncat[:, :KV_LORA], None)
        m_ref[...] = m0
        l_ref[...] = l0
        acc_ref[...] = acc0

    ql = qcat[:, :KV_LORA]
    qr = qcat[:, KV_LORA:KV_LORA + QK_ROPE]
    state = (m_ref[...], l_ref[...], acc_ref[...])
    for sc in range(pp // sub):
        pages = range(sc * sub, (sc + 1) * sub)
        kc = jnp.concatenate([c_refs[k][...].astype(BF16) for k in pages], axis=0)
        krt = jnp.concatenate([kr_refs[k][...].astype(BF16) for k in pages], axis=1)
        s = _dot_nt(ql, kc) + _dot(qr, krt)
        state = _softmax_step(s, kc, state)
    m_ref[...], l_ref[...], acc_ref[...] = state

    @pl.when(step == pl.num_programs(1) - 1)
    def _():
        o_ref[...] = (acc_ref[...] / l_ref[...]).reshape(o_ref.shape)


def _sample_attn(qlat, qr, ckv_s, kr_s, cache_c, cache_krt, page_table, t_new, pp=16, sub=4):
    n_seq, n_pages = page_table.shape
    rows = N_HEADS * t_new

    def page_map(k):
        return lambda s, c, pt: (pt[s, c * pp + k], 0, 0)

    in_specs = [
        pl.BlockSpec((N_HEADS, t_new, KV_LORA), lambda s, c, pt: (0, s, 0)),
        pl.BlockSpec((N_HEADS, t_new, LANES), lambda s, c, pt: (0, s, 0)),
        pl.BlockSpec((t_new, KV_LORA), lambda s, c, pt: (s, 0)),
        pl.BlockSpec((t_new, LANES), lambda s, c, pt: (s, 0)),
    ]
    in_specs += [pl.BlockSpec((None, PAGE_SIZE, KV_LORA), page_map(k)) for k in range(pp)]
    in_specs += [pl.BlockSpec((None, QK_ROPE, PAGE_SIZE), page_map(k)) for k in range(pp)]
    grid_spec = pltpu.PrefetchScalarGridSpec(
        num_scalar_prefetch=1,
        grid=(n_seq, n_pages // pp),
        in_specs=in_specs,
        out_specs=pl.BlockSpec((N_HEADS, t_new, KV_LORA), lambda s, c, pt: (0, s, 0)),
        scratch_shapes=[
            pltpu.VMEM((rows, KCAT), BF16),
            pltpu.VMEM((PAGE_SIZE, KCAT), BF16),
            pltpu.VMEM((rows, 1), F32),
            pltpu.VMEM((rows, 1), F32),
            pltpu.VMEM((rows, KV_LORA), F32),
        ],
    )
    return pl.pallas_call(
        functools.partial(_sattn_kernel, pp=pp, sub=sub, t_new=t_new),
        grid_spec=grid_spec,
        out_shape=jax.ShapeDtypeStruct((N_HEADS, n_seq * t_new, KV_LORA), F32),
        compiler_params=_cparams("arbitrary", "arbitrary"),
        name="sample_attn",
    )(page_table, qlat, qr, ckv_s, kr_s, *([cache_c] * pp), *([cache_krt] * pp))


def _ouv_kernel(o_ref, w_ref, out_ref):
    out_ref[...] = _dot(o_ref[...].astype(BF16), w_ref[...]).astype(out_ref.dtype)


def _o_uv(olat, w_uv2):
    _, rows, _ = olat.shape
    return pl.pallas_call(
        _ouv_kernel,
        grid=(N_HEADS,),
        in_specs=[
            pl.BlockSpec((None, rows, KV_LORA), lambda h: (h, 0, 0)),
            pl.BlockSpec((KV_LORA, V_DIM), lambda h: (0, h)),
        ],
        out_specs=pl.BlockSpec((rows, V_DIM), lambda h: (0, h)),
        out_shape=jax.ShapeDtypeStruct((rows, N_HEADS * V_DIM), BF16),
        compiler_params=_cparams("arbitrary"),
        name="o_uv",
    )(olat, w_uv2)


HALO = 2 * SUBLANES


def _pool_groups(load, u_of, cnt_of, wg_ref, ps_ref, store):
    group = wg_ref.shape[1]
    for g, w in enumerate(POOL_WINDOWS):
        sl = slice(g * group, (g + 1) * group)
        ws = load(0, sl)
        for k in range(1, w):
            ws = ws + load(k, sl)
        pooled = ws / cnt_of(w) - u_of(sl)
        pooled = pooled.reshape(-1, group).astype(BF16)
        store(sl, (_dot(pooled, wg_ref[g]) * ps_ref[:, sl]))


def _pool_prompt_kernel(u_ref, wg_ref, ps_ref, y_ref, ext_ref, *, tiles_per_seq):
    i = pl.program_id(0)
    tm = u_ref.shape[0]
    it = i % tiles_per_seq

    @pl.when(it == 0)
    def _():
        ext_ref[0:HALO, :] = jnp.zeros((HALO, ext_ref.shape[1]), F32)

    @pl.when(it != 0)
    def _():
        ext_ref[0:HALO, :] = ext_ref[tm:tm + HALO, :]

    ext_ref[HALO:HALO + tm, :] = u_ref[...]
    pos = it * tm + lax.broadcasted_iota(jnp.int32, (tm, 1), 0)

    def store(sl, val):
        y_ref[:, sl] = val.astype(y_ref.dtype)

    _pool_groups(
        lambda k, sl: ext_ref[HALO - k:HALO - k + tm, sl],
        lambda sl: u_ref[:, sl],
        lambda w: jnp.minimum(pos + 1, w).astype(F32),
        wg_ref, ps_ref, store)


def _pool_prompt(u, wg, ps, rows, seq):
    width = wg.shape[0] * wg.shape[1]
    tm = ROW_TILE
    return pl.pallas_call(
        functools.partial(_pool_prompt_kernel, tiles_per_seq=seq // tm),
        grid=(rows // tm,),
        in_specs=[
            pl.BlockSpec((tm, width), lambda i: (i, 0)),
            pl.BlockSpec(wg.shape, lambda i: (0, 0, 0)),
            pl.BlockSpec((1, width), lambda i: (0, 0)),
        ],
        out_specs=pl.BlockSpec((tm, width), lambda i: (i, 0)),
        out_shape=jax.ShapeDtypeStruct((rows, width), BF16),
        scratch_shapes=[pltpu.VMEM((HALO + tm, width), F32)],
        compiler_params=_cparams("arbitrary"),
        name="pool_prompt",
    )(u, wg, ps)


def _pool_sample_kernel(u_ref, st_ref, wg_ref, ps_ref, y_ref, ext_ref, *, pos0):
    t_new = u_ref.shape[1]
    ext_ref[:, 0:HALO, :] = st_ref[...]
    ext_ref[:, HALO:HALO + t_new, :] = u_ref[...]
    pos = pos0 + lax.broadcasted_iota(jnp.int32, (1, t_new, 1), 1)

    def store(sl, val):
        y_ref[:, sl] = val.astype(y_ref.dtype)

    _pool_groups(
        lambda k, sl: ext_ref[:, HALO - k:HALO - k + t_new, sl],
        lambda sl: u_ref[:, :, sl],
        lambda w: jnp.minimum(pos + 1, w).astype(F32),
        wg_ref, ps_ref, store)


def _pool_sample(u3, st, wg, ps, pos0, seq0, ns=64):
    n_seq = st.shape[0]
    _, t_new, width = u3.shape
    blk0 = seq0 // ns
    return pl.pallas_call(
        functools.partial(_pool_sample_kernel, pos0=pos0),
        grid=(n_seq // ns,),
        in_specs=[
            pl.BlockSpec((ns, t_new, width), lambda i: (blk0 + i, 0, 0)),
            pl.BlockSpec((ns, HALO, width), lambda i: (i, 0, 0)),
            pl.BlockSpec(wg.shape, lambda i: (0, 0, 0)),
            pl.BlockSpec((1, width), lambda i: (0, 0)),
        ],
        out_specs=pl.BlockSpec((ns * t_new, width), lambda i: (i, 0)),
        out_shape=jax.ShapeDtypeStruct((n_seq * t_new, width), BF16),
        scratch_shapes=[pltpu.VMEM((ns, HALO + t_new, width), F32)],
        compiler_params=_cparams("arbitrary"),
        name="pool_sample",
    )(u3, st, wg, ps)


def _merge_kernel(h_ref, oap_ref, oas_ref, ypp_ref, yps_ref, wga_ref, wgb_ref, ba_ref, bb_ref, wa_ref, wp_ref, o_ref,
                  *, n_prompt):
    i = pl.program_id(1)
    h = h_ref[...]
    ga = jax.nn.sigmoid(_dot(h, wga_ref[...]) + ba_ref[...])
    gb = jax.nn.sigmoid(_dot(h, wgb_ref[...]) + bb_ref[...])

    def finish(oa_ref, yp_ref):
        a = _dot(oa_ref[...], wa_ref[...])
        p = _dot(yp_ref[...], wp_ref[...])
        o_ref[...] = (ga * a + gb * p).astype(o_ref.dtype)

    @pl.when(i < n_prompt)
    def _():
        finish(oap_ref, ypp_ref)

    @pl.when(i >= n_prompt)
    def _():
        finish(oas_ref, yps_ref)


def _merge(h, oa_p, oa_s, yp_p, yp_s, wg, b_gates, wa, wp, tn=512):
    m, d = h.shape
    ka, kp = wa.shape[0], wp.shape[0]
    n = wa.shape[1]
    nb = n // tn
    n_prompt = oa_p.shape[0] // ROW_TILE
    pr, sr = _prompt_rows(n_prompt), _sample_rows(n_prompt)
    bg = b_gates.reshape(1, -1)
    return pl.pallas_call(
        functools.partial(_merge_kernel, n_prompt=n_prompt),
        grid=(nb, m // ROW_TILE),
        in_specs=[
            pl.BlockSpec((ROW_TILE, d), lambda j, i: (i, 0)),
            pl.BlockSpec((ROW_TILE, ka), lambda j, i: (pr(i), 0)),
            pl.BlockSpec((ROW_TILE, ka), lambda j, i: (sr(i), 0)),
            pl.BlockSpec((ROW_TILE, kp), lambda j, i: (pr(i), 0)),
            pl.BlockSpec((ROW_TILE, kp), lambda j, i: (sr(i), 0)),
            pl.BlockSpec((d, tn), lambda j, i: (0, j)),
            pl.BlockSpec((d, tn), lambda j, i: (0, nb + j)),
            pl.BlockSpec((1, tn), lambda j, i: (0, j)),
            pl.BlockSpec((1, tn), lambda j, i: (0, nb + j)),
            pl.BlockSpec((ka, tn), lambda j, i: (0, j)),
            pl.BlockSpec((kp, tn), lambda j, i: (0, j)),
        ],
        out_specs=pl.BlockSpec((ROW_TILE, tn), lambda j, i: (i, j)),
        out_shape=jax.ShapeDtypeStruct((m, n), BF16),
        compiler_params=_cparams("arbitrary", "arbitrary"),
        name="merge",
    )(h, oa_p, oa_s, yp_p, yp_s, wg, wg, bg, bg, wa, wp)


def _ffn_kernel(*refs, sample, tiles_per_seq, t_new):
    if sample:
        (x_ref, g_ref, wa_ref, wv_ref, cw_ref, cb_ref, wd_ref, gf_ref, p1_ref, p2_ref,
         y_ref, a_ref, hn_ref, acc_ref, ext_ref) = refs
    else:
        (x_ref, g_ref, wa_ref, wv_ref, cw_ref, cb_ref, wd_ref, gf_ref,
         y_ref, tail_ref, hn_ref, acc_ref, ext_ref, carry_ref) = refs
    i, f = pl.program_id(0), pl.program_id(1)
    tm = x_ref.shape[0]

    @pl.when(f == 0)
    def _():
        hn_ref[...] = _rms(x_ref[...], g_ref[...]).astype(hn_ref.dtype)
        acc_ref[...] = jnp.zeros(acc_ref.shape, F32)

    hn = hn_ref[...]
    a = _dot(hn, wa_ref[...])
    v = _dot(hn, wv_ref[...])
    ext_ref[SUBLANES:SUBLANES + tm, :] = a
    if sample:
        ext_ref[0:SUBLANES, :] = jnp.zeros((SUBLANES, ext_ref.shape[1]), F32)
        rmod = lax.broadcasted_iota(jnp.int32, (tm, 1), 0) % t_new
        s1 = jnp.where(rmod >= 1, ext_ref[SUBLANES - 1:SUBLANES - 1 + tm, :], p1_ref[...])
        s2 = jnp.where(rmod >= 2, ext_ref[SUBLANES - 2:SUBLANES - 2 + tm, :], p2_ref[...])
        a_ref[...] = a
    else:
        first = i % tiles_per_seq == 0

        @pl.when(first)
        def _():
            ext_ref[0:SUBLANES, :] = jnp.zeros((SUBLANES, ext_ref.shape[1]), F32)

        @pl.when(jnp.logical_not(first))
        def _():
            ext_ref[0:SUBLANES, :] = carry_ref[f]

        tail = a[tm - SUBLANES:, :]
        carry_ref[f] = tail
        tail_ref[...] = tail
        s1 = ext_ref[SUBLANES - 1:SUBLANES - 1 + tm, :]
        s2 = ext_ref[SUBLANES - 2:SUBLANES - 2 + tm, :]
    conv = cb_ref[...] + cw_ref[0:1, :] * s2
    conv = conv + cw_ref[1:2, :] * s1
    conv = conv + cw_ref[2:3, :] * a
    hid = (jax.nn.silu(conv) * v).astype(BF16)
    acc_ref[...] += _dot(hid, wd_ref[...])

    @pl.when(f == pl.num_programs(1) - 1)
    def _():
        x2 = x_ref[...] + acc_ref[...]
        y_ref[...] = _rms(x2, gf_ref[...])


def _ffn(x, row0, rows, g, w_up, conv_w, conv_b, w_down, gf, *, seq=None, prev=None, t_new=None, tf=512):
    d = x.shape[1]
    d_ff = w_down.shape[0]
    nf = d_ff // tf
    tm = ROW_TILE
    blk0 = row0 // tm
    sample = prev is not None
    in_specs = [
        pl.BlockSpec((tm, d), lambda i, f: (blk0 + i, 0)),
        pl.BlockSpec((1, d), lambda i, f: (0, 0)),
        pl.BlockSpec((d, tf), lambda i, f: (0, f)),
        pl.BlockSpec((d, tf), lambda i, f: (0, nf + f)),
        pl.BlockSpec((CONV_W, tf), lambda i, f: (0, f)),
        pl.BlockSpec((1, tf), lambda i, f: (0, f)),
        pl.BlockSpec((tf, d), lambda i, f: (f, 0)),
        pl.BlockSpec((1, d), lambda i, f: (0, 0)),
    ]
    ins = [x, g.reshape(1, d), w_up, w_up, conv_w, conv_b.reshape(1, d_ff), w_down, gf.reshape(1, d)]
    scratch = [pltpu.VMEM((tm, d), BF16), pltpu.VMEM((tm, d), F32), pltpu.VMEM((SUBLANES + tm, tf), F32)]
    y_spec = pl.BlockSpec((tm, d), lambda i, f: (i, 0))
    y_shape = jax.ShapeDtypeStruct((rows, d), F32)
    if sample:
        in_specs += [pl.BlockSpec((tm, tf), lambda i, f: (i, f))] * 2
        ins += list(prev)
        out_specs = [y_spec, pl.BlockSpec((tm, tf), lambda i, f: (i, f))]
        out_shape = [y_shape, jax.ShapeDtypeStruct((rows, d_ff), F32)]
        tiles_per_seq = None
    else:
        out_specs = [y_spec, pl.BlockSpec((None, SUBLANES, tf), lambda i, f: (i, 0, f))]
        out_shape = [y_shape, jax.ShapeDtypeStruct((rows // tm, SUBLANES, d_ff), F32)]
        scratch.append(pltpu.VMEM((nf, SUBLANES, tf), F32))
        tiles_per_seq = seq // tm
    return pl.pallas_call(
        functools.partial(_ffn_kernel, sample=sample, tiles_per_seq=tiles_per_seq, t_new=t_new),
        grid=(rows // tm, nf),
        in_specs=in_specs,
        out_specs=out_specs,
        out_shape=out_shape,
        scratch_shapes=scratch,
        compiler_params=_cparams("arbitrary", "arbitrary"),
        name="ffn_sample" if sample else "ffn_prompt",
    )(*ins)


def _rope_table(pos):
    half = QK_ROPE // 2
    inv = ROPE_BASE ** (-jnp.arange(half, dtype=F32) / half)
    ang = pos[:, None] * inv[None, :]
    c, s = jnp.cos(ang), jnp.sin(ang)
    n = pos.shape[0]
    one, zero = jnp.ones((n, LANES - QK_ROPE), F32), jnp.zeros((n, LANES - QK_ROPE), F32)
    return jnp.concatenate([c, c, one, -s, s, zero], axis=1)


def _layer(xp, xs, dims, cache_c, cache_krt, page_table, state_pool, state_conv,
           norm_attn_g, w_in, b_gates, kv_norm_g, w_uk, w_uv, w_attn_out, w_pool_group, pool_scale,
           w_pool_out, w_o, norm_ffn_g, w_up, conv_w, conv_b, w_down, norm_final_g):
    batch, seq, n_seq, t_new, n_past = dims
    mp, ms = xp.shape[0], xs.shape[0]
    d = xp.shape[1]
    n_prompt = mp // ROW_TILE
    tiles_per_seq = seq // ROW_TILE
    q_cols = N_HEADS * (QK_NOPE + QK_ROPE)
    pool_w = w_pool_out.shape[0]
    c0, c2 = q_cols, q_cols + KV_LORA + QK_ROPE
    c3 = c2 + pool_w

    pos = jnp.concatenate([jnp.arange(seq, dtype=F32), jnp.tile(n_past + jnp.arange(t_new, dtype=F32), ROW_TILE // t_new)])
    tab = _rope_table(pos)

    def tab_row(i):
        return jnp.where(i < n_prompt, i % tiles_per_seq, tiles_per_seq)

    wq = jnp.pad(w_in[:, :c0].reshape(d, N_HEADS, QK_NOPE + QK_ROPE), ((0, 0), (0, 0), (0, HEAD_PAD - QK_NOPE - QK_ROPE)))
    wq = wq.reshape(d, N_HEADS * HEAD_PAD).astype(BF16)
    wck = jnp.pad(w_in[:, c0:c2], ((0, 0), (0, LANES - QK_ROPE))).astype(BF16)
    wu = w_in[:, c2:c3].astype(BF16)
    wg = w_in[:, c3:].astype(BF16)
    w_ukv = jnp.concatenate([w_uk.reshape(KV_LORA, -1), w_uv.reshape(KV_LORA, -1)], axis=1).astype(BF16)
    w_ukt = jnp.transpose(w_uk, (1, 2, 0)).astype(BF16)
    w_uv2 = w_uv.reshape(KV_LORA, -1).astype(BF16)

    h = _norm(xp, xs, norm_attn_g)
    q = _q_proj(h, wq, tab, tab_row)
    ckv_p, ckv_s, kr_p, kr_s = _ckv_proj(h, wck, kv_norm_g, tab, tab_row, mp, ms)
    u = _mm(h, wu, name="pool_proj")

    kp, vp = _kv_up(ckv_p, kr_p, w_ukv)
    o_p = _prompt_attn(q, kp, vp, batch, seq)
    qlat, qr = _q_latent(q, w_ukt, mp, ms)
    olat = _sample_attn(qlat, qr, ckv_s, kr_s, cache_c, cache_krt, page_table, t_new)
    o_s = _o_uv(olat, w_uv2)

    wpg = w_pool_group.astype(BF16)
    ps = pool_scale.reshape(1, pool_w)
    y_p = _pool_prompt(u, wpg, ps, mp, seq)
    u3 = u.reshape((mp + ms) // t_new, t_new, pool_w)
    st_pad = jnp.pad(state_pool, ((0, 0), (HALO - POOL_BUF, 0), (0, 0)))
    y_s = _pool_sample(u3, st_pad, wpg, ps, n_past, mp // t_new)

    merged = _merge(h, o_p, o_s, y_p, y_s, wg, b_gates, w_attn_out.astype(BF16), w_pool_out.astype(BF16))
    x1 = _mm(merged, w_o.astype(BF16), res=(xp, xs), n_prompt=n_prompt, name="o_proj")

    w_up_b = w_up.astype(BF16)
    w_down_b = w_down.astype(BF16)
    y_prompt, tails = _ffn(x1, 0, mp, norm_ffn_g, w_up_b, conv_w, conv_b, w_down_b, norm_final_g, seq=seq)
    d_ff = w_down.shape[0]
    prev1 = jnp.pad(state_conv[:, CONV_W - 2:CONV_W - 1], ((0, 0), (0, t_new - 1), (0, 0))).reshape(ms, d_ff)
    prev2 = jnp.pad(state_conv, ((0, 0), (0, t_new - (CONV_W - 1)), (0, 0))).reshape(ms, d_ff)
    y_sample, a_s = _ffn(x1, mp, ms, norm_ffn_g, w_up_b, conv_w, conv_b, w_down_b, norm_final_g,
                         prev=(prev1, prev2), t_new=t_new)

    conv_p = tails[tiles_per_seq - 1::tiles_per_seq, SUBLANES - (CONV_W - 1):]
    conv_s = a_s.reshape(n_seq, t_new, d_ff)[:, t_new - (CONV_W - 1):]
    pool_p = u[:mp].reshape(batch, seq, pool_w)[:, seq - POOL_BUF:]
    u_s = u[mp:].reshape(n_seq, t_new, pool_w)
    pool_s = jnp.concatenate([state_pool, u_s], axis=1)[:, -POOL_BUF:]
    return (y_prompt, y_sample, ckv_p.reshape(batch, seq, KV_LORA), kr_p[:, :QK_ROPE].reshape(batch, seq, QK_ROPE),
            ckv_s.reshape(n_seq, t_new, KV_LORA), kr_s[:, :QK_ROPE].reshape(n_seq, t_new, QK_ROPE),
            pool_p, pool_s, conv_p, conv_s)


def kernel(x_prompt, x_sample, cache_kv_latent, cache_k_rope, page_table, state_pool, state_conv, norm_attn_g, w_in, b_gates, kv_norm_g, w_uk, w_uv, w_attn_out, w_pool_group, pool_scale, w_pool_out, w_o, norm_ffn_g, w_up, conv_w, conv_b, w_down, norm_final_g):
    batch, seq, d = x_prompt.shape
    n_seq, t_new, _ = x_sample.shape
    depth = w_in.shape[0]
    assert depth == 1, "the stacked-row pipeline below is written for a single layer"
    n_past = page_table.shape[1] * PAGE_SIZE
    dims = (batch, seq, n_seq, t_new, n_past)
    xp = x_prompt.reshape(batch * seq, d)
    xs = x_sample.reshape(n_seq * t_new, d)
    l = 0
    cache_c = cache_kv_latent.reshape(cache_kv_latent.shape[1:])
    cache_krt = jnp.swapaxes(cache_k_rope.reshape(cache_k_rope.shape[1:]), 1, 2)
    outs = _layer(xp, xs, dims, cache_c, cache_krt, page_table, state_pool[l], state_conv[l],
                  norm_attn_g[l], w_in[l], b_gates[l], kv_norm_g[l], w_uk[l], w_uv[l], w_attn_out[l],
                  w_pool_group[l], pool_scale[l], w_pool_out[l], w_o[l], norm_ffn_g[l], w_up[l], conv_w[l],
                  conv_b[l], w_down[l], norm_final_g)
    y_p, y_s = outs[0].reshape(batch, seq, d), outs[1].reshape(n_seq, t_new, d)
    return (y_p, y_s) + tuple(o[None] for o in outs[2:])
```

```python
import functools
import math

import jax
import jax.numpy as jnp
from jax import lax
from jax.experimental import pallas as pl
from jax.experimental.pallas import tpu as pltpu

F32 = jnp.float32
BF16 = jnp.bfloat16

N_HEADS = 16
QK_NOPE = 128
QK_ROPE = 64
V_DIM = 128
KV_LORA = 512
ROPE_BASE = 10000.0
ATTN_SCALE = (QK_NOPE + QK_ROPE) ** -0.5
EXP2_SCALE = ATTN_SCALE * math.log2(math.e)
PAGE_SIZE = 128
POOL_WINDOWS = (2, 4, 8, 16)
POOL_BUF = max(POOL_WINDOWS) - 1
CONV_W = 3
EPS = 1e-6

LANES = 128
SUBLANES = 8
HEAD_PAD = 2 * LANES
VMEM_LIMIT = 48 * 1024 * 1024

ROW_TILE = 512


def _cparams(*sem):
    return pltpu.CompilerParams(dimension_semantics=sem, vmem_limit_bytes=VMEM_LIMIT)


def _dot(a, b):
    return jnp.dot(a, b, preferred_element_type=F32)


def _dot_nt(a, b):
    return lax.dot_general(a, b, (((1,), (1,)), ((), ())), preferred_element_type=F32)


def _rms(x, g):
    ms = jnp.mean(x * x, axis=-1, keepdims=True)
    return x * lax.rsqrt(ms + EPS) * g


def _rope128(x, ct, st):
    lane = lax.broadcasted_iota(jnp.int32, x.shape, 1)
    other = jnp.where(lane < QK_ROPE // 2, pltpu.roll(x, LANES - QK_ROPE // 2, 1), pltpu.roll(x, QK_ROPE // 2, 1))
    return x * ct + other * st


def _softmax_step(s, vals, state):
    bm = jnp.max(s, axis=-1, keepdims=True)
    if state is None:
        m_new = bm
        p = jnp.exp2((s - m_new) * EXP2_SCALE)
        l_new = jnp.sum(p, axis=-1, keepdims=True)
        acc_new = _dot(p.astype(BF16), vals)
    else:
        m, l, acc = state
        m_new = jnp.maximum(m, bm)
        alpha = jnp.exp2((m - m_new) * EXP2_SCALE)
        p = jnp.exp2((s - m_new) * EXP2_SCALE)
        l_new = alpha * l + jnp.sum(p, axis=-1, keepdims=True)
        acc_new = alpha * acc + _dot(p.astype(BF16), vals)
    return m_new, l_new, acc_new


def _softmax_merge(states):
    m_all = states[0][0]
    for m, _, _ in states[1:]:
        m_all = jnp.maximum(m_all, m)
    l_all = acc_all = None
    for m, l, acc in states:
        w = jnp.exp2((m - m_all) * EXP2_SCALE)
        l_all = w * l if l_all is None else l_all + w * l
        acc_all = w * acc if acc_all is None else acc_all + w * acc
    return m_all, l_all, acc_all


def _prompt_rows(n_prompt):
    return lambda i: jnp.minimum(i, n_prompt - 1)


def _sample_rows(n_prompt):
    return lambda i: jnp.maximum(i - n_prompt, 0)


def _norm_kernel(xp_ref, xs_ref, g_ref, o_ref, *, n_prompt):
    i = pl.program_id(0)

    @pl.when(i < n_prompt)
    def _():
        o_ref[...] = _rms(xp_ref[...], g_ref[...]).astype(o_ref.dtype)

    @pl.when(i >= n_prompt)
    def _():
        o_ref[...] = _rms(xs_ref[...], g_ref[...]).astype(o_ref.dtype)


def _norm(xp, xs, g):
    d = xp.shape[1]
    n_prompt, n_sample = xp.shape[0] // ROW_TILE, xs.shape[0] // ROW_TILE
    pr, sr = _prompt_rows(n_prompt), _sample_rows(n_prompt)
    return pl.pallas_call(
        functools.partial(_norm_kernel, n_prompt=n_prompt),
        grid=(n_prompt + n_sample,),
        in_specs=[
            pl.BlockSpec((ROW_TILE, d), lambda i: (pr(i), 0)),
            pl.BlockSpec((ROW_TILE, d), lambda i: (sr(i), 0)),
            pl.BlockSpec((1, d), lambda i: (0, 0)),
        ],
        out_specs=pl.BlockSpec((ROW_TILE, d), lambda i: (i, 0)),
        out_shape=jax.ShapeDtypeStruct((xp.shape[0] + xs.shape[0], d), BF16),
        compiler_params=_cparams("arbitrary"),
        name="rmsnorm",
    )(xp, xs, g.reshape(1, d))


def _mm_kernel(*refs, n_prompt, has_res):
    h_ref, w_ref = refs[0], refs[1]
    o_ref = refs[-1]
    r = _dot(h_ref[...], w_ref[...])
    if has_res:
        i = pl.program_id(1)

        @pl.when(i < n_prompt)
        def _():
            o_ref[...] = refs[2][...] + r

        @pl.when(i >= n_prompt)
        def _():
            o_ref[...] = refs[3][...] + r
    else:
        o_ref[...] = r.astype(o_ref.dtype)


def _mm(h, w, *, res=None, n_prompt=None, out_dtype=F32, tn=512, name="mm"):
    m, kd = h.shape
    n = w.shape[1]
    ins = [h, w]
    specs = [pl.BlockSpec((ROW_TILE, kd), lambda j, i: (i, 0)), pl.BlockSpec((kd, tn), lambda j, i: (0, j))]
    if res is not None:
        pr, sr = _prompt_rows(n_prompt), _sample_rows(n_prompt)
        ins += list(res)
        specs += [pl.BlockSpec((ROW_TILE, tn), lambda j, i: (pr(i), j)), pl.BlockSpec((ROW_TILE, tn), lambda j, i: (sr(i), j))]
    return pl.pallas_call(
        functools.partial(_mm_kernel, n_prompt=n_prompt, has_res=res is not None),
        grid=(n // tn, m // ROW_TILE),
        in_specs=specs,
        out_specs=pl.BlockSpec((ROW_TILE, tn), lambda j, i: (i, j)),
        out_shape=jax.ShapeDtypeStruct((m, n), out_dtype),
        compiler_params=_cparams("arbitrary", "arbitrary"),
        name=name,
    )(*ins)


def _q_kernel(h_ref, w_ref, tab_ref, o_ref):
    r = _dot(h_ref[...], w_ref[...])
    ct, st = tab_ref[:, :LANES], tab_ref[:, LANES:]
    for hh in range(r.shape[1] // HEAD_PAD):
        c0 = hh * HEAD_PAD
        o_ref[:, c0:c0 + LANES] = r[:, c0:c0 + LANES].astype(o_ref.dtype)
        o_ref[:, c0 + LANES:c0 + HEAD_PAD] = _rope128(r[:, c0 + LANES:c0 + HEAD_PAD], ct, st).astype(o_ref.dtype)


def _q_proj(h, wq, tab, tab_row, tn=1024):
    m, kd = h.shape
    n = wq.shape[1]
    return pl.pallas_call(
        _q_kernel,
        grid=(n // tn, m // ROW_TILE),
        in_specs=[
            pl.BlockSpec((ROW_TILE, kd), lambda j, i: (i, 0)),
            pl.BlockSpec((kd, tn), lambda j, i: (0, j)),
            pl.BlockSpec((ROW_TILE, 2 * LANES), lambda j, i: (tab_row(i), 0)),
        ],
        out_specs=pl.BlockSpec((ROW_TILE, tn), lambda j, i: (i, j)),
        out_shape=jax.ShapeDtypeStruct((m, n), BF16),
        compiler_params=_cparams("arbitrary", "arbitrary"),
        name="q_proj",
    )(h, wq, tab)


def _ckv_kernel(h_ref, w_ref, g_ref, tab_ref, cp_ref, cs_ref, krp_ref, krs_ref, *, n_prompt):
    i = pl.program_id(0)
    r = _dot(h_ref[...], w_ref[...])
    c = _rms(r[:, :KV_LORA], g_ref[...])
    kr = _rope128(r[:, KV_LORA:], tab_ref[:, :LANES], tab_ref[:, LANES:])

    @pl.when(i < n_prompt)
    def _():
        cp_ref[...] = c
        krp_ref[...] = kr

    @pl.when(i >= n_prompt)
    def _():
        cs_ref[...] = c
        krs_ref[...] = kr


def _ckv_proj(h, wck, g, tab, tab_row, mp, ms):
    m, kd = h.shape
    n = wck.shape[1]
    n_prompt = mp // ROW_TILE
    pr, sr = _prompt_rows(n_prompt), _sample_rows(n_prompt)
    return pl.pallas_call(
        functools.partial(_ckv_kernel, n_prompt=n_prompt),
        grid=(m // ROW_TILE,),
        in_specs=[
            pl.BlockSpec((ROW_TILE, kd), lambda i: (i, 0)),
            pl.BlockSpec((kd, n), lambda i: (0, 0)),
            pl.BlockSpec((1, KV_LORA), lambda i: (0, 0)),
            pl.BlockSpec((ROW_TILE, 2 * LANES), lambda i: (tab_row(i), 0)),
        ],
        out_specs=[
            pl.BlockSpec((ROW_TILE, KV_LORA), lambda i: (pr(i), 0)),
            pl.BlockSpec((ROW_TILE, KV_LORA), lambda i: (sr(i), 0)),
            pl.BlockSpec((ROW_TILE, LANES), lambda i: (pr(i), 0)),
            pl.BlockSpec((ROW_TILE, LANES), lambda i: (sr(i), 0)),
        ],
        out_shape=[
            jax.ShapeDtypeStruct((mp, KV_LORA), F32), jax.ShapeDtypeStruct((ms, KV_LORA), F32),
            jax.ShapeDtypeStruct((mp, LANES), F32), jax.ShapeDtypeStruct((ms, LANES), F32),
        ],
        compiler_params=_cparams("arbitrary"),
        name="ckv_proj",
    )(h, wck, g.reshape(1, KV_LORA), tab)


def _kvup_kernel(c_ref, kr_ref, w_ref, k_ref, v_ref):
    r = _dot(c_ref[...].astype(BF16), w_ref[...])
    nk = N_HEADS * QK_NOPE
    v_ref[...] = r[:, nk:].astype(v_ref.dtype)
    krb = kr_ref[...].astype(k_ref.dtype)
    for hh in range(N_HEADS):
        k_ref[:, hh * HEAD_PAD:hh * HEAD_PAD + LANES] = r[:, hh * QK_NOPE:(hh + 1) * QK_NOPE].astype(k_ref.dtype)
        k_ref[:, hh * HEAD_PAD + LANES:(hh + 1) * HEAD_PAD] = krb


def _kv_up(ckv, kr, w_ukv, tm=256):
    rows = ckv.shape[0]
    n = w_ukv.shape[1]
    return pl.pallas_call(
        _kvup_kernel,
        grid=(rows // tm,),
        in_specs=[
            pl.BlockSpec((tm, KV_LORA), lambda i: (i, 0)),
            pl.BlockSpec((tm, LANES), lambda i: (i, 0)),
            pl.BlockSpec((KV_LORA, n), lambda i: (0, 0)),
        ],
        out_specs=[pl.BlockSpec((tm, N_HEADS * HEAD_PAD), lambda i: (i, 0)), pl.BlockSpec((tm, N_HEADS * V_DIM), lambda i: (i, 0))],
        out_shape=[jax.ShapeDtypeStruct((rows, N_HEADS * HEAD_PAD), BF16), jax.ShapeDtypeStruct((rows, N_HEADS * V_DIM), BF16)],
        compiler_params=_cparams("arbitrary"),
        name="kv_up",
    )(ckv, kr, w_ukv)


def _pattn_kernel(q_ref, k_ref, v_ref, o_ref, *, tq):
    nq = q_ref.shape[0] // tq
    row = lax.broadcasted_iota(jnp.int32, (tq, tq), 0)
    col = lax.broadcasted_iota(jnp.int32, (tq, tq), 1)
    for qi in range(nq):
        q = q_ref[qi * tq:(qi + 1) * tq, :]
        state = None
        for ki in range(qi + 1):
            s = _dot_nt(q, k_ref[ki * tq:(ki + 1) * tq, :])
            if ki == qi:
                s = jnp.where(col <= row, s, -jnp.inf)
            state = _softmax_step(s, v_ref[ki * tq:(ki + 1) * tq, :], state)
        _, l, acc = state
        o_ref[qi * tq:(qi + 1) * tq, :] = (acc / l).astype(o_ref.dtype)


def _prompt_attn(q, k, v, batch, seq, tq=512):
    return pl.pallas_call(
        functools.partial(_pattn_kernel, tq=tq),
        grid=(batch, N_HEADS),
        in_specs=[
            pl.BlockSpec((seq, HEAD_PAD), lambda b, h: (b, h)),
            pl.BlockSpec((seq, HEAD_PAD), lambda b, h: (b, h)),
            pl.BlockSpec((seq, V_DIM), lambda b, h: (b, h)),
        ],
        out_specs=pl.BlockSpec((seq, V_DIM), lambda b, h: (b, h)),
        out_shape=jax.ShapeDtypeStruct((batch * seq, N_HEADS * V_DIM), BF16),
        compiler_params=_cparams("arbitrary", "arbitrary"),
        name="prompt_attn",
    )(q, k, v)


def _qlat_kernel(q_ref, w_ref, ql_ref, qr_ref):
    q = q_ref[...]
    ql_ref[...] = _dot(q[:, :QK_NOPE], w_ref[...])
    qr_ref[...] = q[:, LANES:].astype(F32)


def _q_latent(q, w_ukt, row0, rows):
    blk = row0 // rows
    return pl.pallas_call(
        _qlat_kernel,
        grid=(N_HEADS,),
        in_specs=[
            pl.BlockSpec((rows, HEAD_PAD), lambda h: (blk, h)),
            pl.BlockSpec((None, QK_NOPE, KV_LORA), lambda h: (h, 0, 0)),
        ],
        out_specs=[
            pl.BlockSpec((None, rows, KV_LORA), lambda h: (h, 0, 0)),
            pl.BlockSpec((None, rows, LANES), lambda h: (h, 0, 0)),
        ],
        out_shape=[jax.ShapeDtypeStruct((N_HEADS, rows, KV_LORA), F32), jax.ShapeDtypeStruct((N_HEADS, rows, LANES), F32)],
        compiler_params=_cparams("arbitrary"),
        name="q_latent",
    )(q, w_ukt)


KCAT = KV_LORA + LANES


def _sattn_kernel(pt_ref, ql_ref, qr_ref, cn_ref, krn_ref, *rest, pp, sub, t_new):
    c_refs, kr_refs = rest[:pp], rest[pp:2 * pp]
    o_ref = rest[2 * pp]
    qcat, ncat, m_ref, l_ref, acc_ref = rest[2 * pp + 1:]
    step = pl.program_id(1)
    rows = N_HEADS * t_new

    @pl.when(step == 0)
    def _():
        qcat[:, :KV_LORA] = ql_ref[...].reshape(rows, KV_LORA).astype(BF16)
        qcat[:, KV_LORA:] = qr_ref[...].reshape(rows, LANES).astype(BF16)
        ncat[...] = jnp.zeros(ncat.shape, BF16)
        ncat[0:t_new, :KV_LORA] = cn_ref[...].astype(BF16)
        ncat[0:t_new, KV_LORA:] = krn_ref[...].astype(BF16)
        s = _dot_nt(qcat[...], ncat[...])
        row = lax.broadcasted_iota(jnp.int32, s.shape, 0)
        col = lax.broadcasted_iota(jnp.int32, s.shape, 1)
        s = jnp.where(col <= row % t_new, s, -jnp.inf)
        m0, l0, acc0 = _softmax_step(s, ncat[:, :KV_LORA], None)
        m_ref[...] = m0
        l_ref[...] = l0
        acc_ref[...] = acc0

    ql = qcat[:, :KV_LORA]
    qr = qcat[:, KV_LORA:KV_LORA + QK_ROPE]
    states = [(m_ref[...], l_ref[...], acc_ref[...])]
    kcs, ss = [], []
    for sc in range(pp // sub):
        pages = range(sc * sub, (sc + 1) * sub)
        kc = jnp.concatenate([c_refs[k][...].astype(BF16) for k in pages], axis=0)
        krt = jnp.concatenate([kr_refs[k][...].astype(BF16) for k in pages], axis=1)
        kcs.append(kc)
        ss.append(_dot_nt(ql, kc) + _dot(qr, krt))
    ms = [jnp.max(s, axis=-1, keepdims=True) for s in ss]
    ps = [jnp.exp2((s - m) * EXP2_SCALE) for s, m in zip(ss, ms)]
    ls = [jnp.sum(p, axis=-1, keepdims=True) for p in ps]
    os_ = [_dot(p.astype(BF16), kc) for p, kc in zip(ps, kcs)]
    states += list(zip(ms, ls, os_))
    m_ref[...], l_ref[...], acc_ref[...] = _softmax_merge(states)

    @pl.when(step == pl.num_programs(1) - 1)
    def _():
        o_ref[...] = (acc_ref[...] / l_ref[...]).reshape(o_ref.shape)


def _sample_attn(qlat, qr, ckv_s, kr_s, cache_c, cache_krt, page_table, t_new, pp=16, sub=8):
    n_seq, n_pages = page_table.shape
    rows = N_HEADS * t_new

    def page_map(k):
        return lambda s, c, pt: (pt[s, c * pp + k], 0, 0)

    in_specs = [
        pl.BlockSpec((N_HEADS, t_new, KV_LORA), lambda s, c, pt: (0, s, 0)),
        pl.BlockSpec((N_HEADS, t_new, LANES), lambda s, c, pt: (0, s, 0)),
        pl.BlockSpec((t_new, KV_LORA), lambda s, c, pt: (s, 0)),
        pl.BlockSpec((t_new, LANES), lambda s, c, pt: (s, 0)),
    ]
    in_specs += [pl.BlockSpec((None, PAGE_SIZE, KV_LORA), page_map(k)) for k in range(pp)]
    in_specs += [pl.BlockSpec((None, QK_ROPE, PAGE_SIZE), page_map(k)) for k in range(pp)]
    grid_spec = pltpu.PrefetchScalarGridSpec(
        num_scalar_prefetch=1,
        grid=(n_seq, n_pages // pp),
        in_specs=in_specs,
        out_specs=pl.BlockSpec((N_HEADS, t_new, KV_LORA), lambda s, c, pt: (0, s, 0)),
        scratch_shapes=[
            pltpu.VMEM((rows, KCAT), BF16),
            pltpu.VMEM((PAGE_SIZE, KCAT), BF16),
            pltpu.VMEM((rows, 1), F32),
            pltpu.VMEM((rows, 1), F32),
            pltpu.VMEM((rows, KV_LORA), F32),
        ],
    )
    return pl.pallas_call(
        functools.partial(_sattn_kernel, pp=pp, sub=sub, t_new=t_new),
        grid_spec=grid_spec,
        out_shape=jax.ShapeDtypeStruct((N_HEADS, n_seq * t_new, KV_LORA), F32),
        compiler_params=_cparams("arbitrary", "arbitrary"),
        name="sample_attn",
    )(page_table, qlat, qr, ckv_s, kr_s, *([cache_c] * pp), *([cache_krt] * pp))


def _ouv_kernel(o_ref, w_ref, out_ref):
    out_ref[...] = _dot(o_ref[...].astype(BF16), w_ref[...]).astype(out_ref.dtype)


def _o_uv(olat, w_uv2):
    _, rows, _ = olat.shape
    return pl.pallas_call(
        _ouv_kernel,
        grid=(N_HEADS,),
        in_specs=[
            pl.BlockSpec((None, rows, KV_LORA), lambda h: (h, 0, 0)),
            pl.BlockSpec((KV_LORA, V_DIM), lambda h: (0, h)),
        ],
        out_specs=pl.BlockSpec((rows, V_DIM), lambda h: (0, h)),
        out_shape=jax.ShapeDtypeStruct((rows, N_HEADS * V_DIM), BF16),
        compiler_params=_cparams("arbitrary"),
        name="o_uv",
    )(olat, w_uv2)


HALO = 2 * SUBLANES


def _pool_groups(load, u_of, cnt_of, wg_ref, ps_ref, store):
    group = wg_ref.shape[1]
    for g, w in enumerate(POOL_WINDOWS):
        sl = slice(g * group, (g + 1) * group)
        ws = load(0, sl)
        for k in range(1, w):
            ws = ws + load(k, sl)
        pooled = ws / cnt_of(w) - u_of(sl)
        pooled = pooled.reshape(-1, group).astype(BF16)
        store(sl, (_dot(pooled, wg_ref[g]) * ps_ref[:, sl]))


def _pool_prompt_kernel(u_ref, wg_ref, ps_ref, y_ref, ext_ref, *, tiles_per_seq):
    i = pl.program_id(0)
    tm = u_ref.shape[0]
    it = i % tiles_per_seq

    @pl.when(it == 0)
    def _():
        ext_ref[0:HALO, :] = jnp.zeros((HALO, ext_ref.shape[1]), F32)

    @pl.when(it != 0)
    def _():
        ext_ref[0:HALO, :] = ext_ref[tm:tm + HALO, :]

    ext_ref[HALO:HALO + tm, :] = u_ref[...]
    pos = it * tm + lax.broadcasted_iota(jnp.int32, (tm, 1), 0)

    def store(sl, val):
        y_ref[:, sl] = val.astype(y_ref.dtype)

    _pool_groups(
        lambda k, sl: ext_ref[HALO - k:HALO - k + tm, sl],
        lambda sl: u_ref[:, sl],
        lambda w: jnp.minimum(pos + 1, w).astype(F32),
        wg_ref, ps_ref, store)


def _pool_prompt(u, wg, ps, rows, seq):
    width = wg.shape[0] * wg.shape[1]
    tm = ROW_TILE
    return pl.pallas_call(
        functools.partial(_pool_prompt_kernel, tiles_per_seq=seq // tm),
        grid=(rows // tm,),
        in_specs=[
            pl.BlockSpec((tm, width), lambda i: (i, 0)),
            pl.BlockSpec(wg.shape, lambda i: (0, 0, 0)),
            pl.BlockSpec((1, width), lambda i: (0, 0)),
        ],
        out_specs=pl.BlockSpec((tm, width), lambda i: (i, 0)),
        out_shape=jax.ShapeDtypeStruct((rows, width), BF16),
        scratch_shapes=[pltpu.VMEM((HALO + tm, width), F32)],
        compiler_params=_cparams("arbitrary"),
        name="pool_prompt",
    )(u, wg, ps)


def _pool_sample_kernel(u_ref, st_ref, wg_ref, ps_ref, y_ref, ext_ref, *, pos0):
    t_new = u_ref.shape[1]
    ext_ref[:, 0:HALO, :] = st_ref[...]
    ext_ref[:, HALO:HALO + t_new, :] = u_ref[...]
    pos = pos0 + lax.broadcasted_iota(jnp.int32, (1, t_new, 1), 1)

    def store(sl, val):
        y_ref[:, sl] = val.astype(y_ref.dtype)

    _pool_groups(
        lambda k, sl: ext_ref[:, HALO - k:HALO - k + t_new, sl],
        lambda sl: u_ref[:, :, sl],
        lambda w: jnp.minimum(pos + 1, w).astype(F32),
        wg_ref, ps_ref, store)


def _pool_sample(u3, st, wg, ps, pos0, seq0, ns=64):
    n_seq = st.shape[0]
    _, t_new, width = u3.shape
    blk0 = seq0 // ns
    return pl.pallas_call(
        functools.partial(_pool_sample_kernel, pos0=pos0),
        grid=(n_seq // ns,),
        in_specs=[
            pl.BlockSpec((ns, t_new, width), lambda i: (blk0 + i, 0, 0)),
            pl.BlockSpec((ns, HALO, width), lambda i: (i, 0, 0)),
            pl.BlockSpec(wg.shape, lambda i: (0, 0, 0)),
            pl.BlockSpec((1, width), lambda i: (0, 0)),
        ],
        out_specs=pl.BlockSpec((ns * t_new, width), lambda i: (i, 0)),
        out_shape=jax.ShapeDtypeStruct((n_seq * t_new, width), BF16),
        scratch_shapes=[pltpu.VMEM((ns, HALO + t_new, width), F32)],
        compiler_params=_cparams("arbitrary"),
        name="pool_sample",
    )(u3, st, wg, ps)


def _merge_kernel(h_ref, oap_ref, oas_ref, ypp_ref, yps_ref, wga_ref, wgb_ref, ba_ref, bb_ref, wa_ref, wp_ref, o_ref,
                  *, n_prompt):
    i = pl.program_id(1)
    h = h_ref[...]
    ga = jax.nn.sigmoid(_dot(h, wga_ref[...]) + ba_ref[...])
    gb = jax.nn.sigmoid(_dot(h, wgb_ref[...]) + bb_ref[...])

    def finish(oa_ref, yp_ref):
        a = _dot(oa_ref[...], wa_ref[...])
        p = _dot(yp_ref[...], wp_ref[...])
        o_ref[...] = (ga * a + gb * p).astype(o_ref.dtype)

    @pl.when(i < n_prompt)
    def _():
        finish(oap_ref, ypp_ref)

    @pl.when(i >= n_prompt)
    def _():
        finish(oas_ref, yps_ref)


def _merge(h, oa_p, oa_s, yp_p, yp_s, wg, b_gates, wa, wp, tn=512):
    m, d = h.shape
    ka, kp = wa.shape[0], wp.shape[0]
    n = wa.shape[1]
    nb = n // tn
    n_prompt = oa_p.shape[0] // ROW_TILE
    pr, sr = _prompt_rows(n_prompt), _sample_rows(n_prompt)
    bg = b_gates.reshape(1, -1)
    return pl.pallas_call(
        functools.partial(_merge_kernel, n_prompt=n_prompt),
        grid=(nb, m // ROW_TILE),
        in_specs=[
            pl.BlockSpec((ROW_TILE, d), lambda j, i: (i, 0)),
            pl.BlockSpec((ROW_TILE, ka), lambda j, i: (pr(i), 0)),
            pl.BlockSpec((ROW_TILE, ka), lambda j, i: (sr(i), 0)),
            pl.BlockSpec((ROW_TILE, kp), lambda j, i: (pr(i), 0)),
            pl.BlockSpec((ROW_TILE, kp), lambda j, i: (sr(i), 0)),
            pl.BlockSpec((d, tn), lambda j, i: (0, j)),
            pl.BlockSpec((d, tn), lambda j, i: (0, nb + j)),
            pl.BlockSpec((1, tn), lambda j, i: (0, j)),
            pl.BlockSpec((1, tn), lambda j, i: (0, nb + j)),
            pl.BlockSpec((ka, tn), lambda j, i: (0, j)),
            pl.BlockSpec((kp, tn), lambda j, i: (0, j)),
        ],
        out_specs=pl.BlockSpec((ROW_TILE, tn), lambda j, i: (i, j)),
        out_shape=jax.ShapeDtypeStruct((m, n), BF16),
        compiler_params=_cparams("arbitrary", "arbitrary"),
        name="merge",
    )(h, oa_p, oa_s, yp_p, yp_s, wg, wg, bg, bg, wa, wp)


def _ffn_kernel(*refs, sample, tiles_per_seq, t_new):
    if sample:
        (x_ref, g_ref, wa_ref, wv_ref, cw_ref, cb_ref, wd_ref, gf_ref, p1_ref, p2_ref,
         y_ref, a_ref, hn_ref, acc_ref, ext_ref) = refs
    else:
        (x_ref, g_ref, wa_ref, wv_ref, cw_ref, cb_ref, wd_ref, gf_ref,
         y_ref, tail_ref, hn_ref, acc_ref, ext_ref, carry_ref) = refs
    i, f = pl.program_id(0), pl.program_id(1)
    tm = x_ref.shape[0]

    @pl.when(f == 0)
    def _():
        hn_ref[...] = _rms(x_ref[...], g_ref[...]).astype(hn_ref.dtype)
        acc_ref[...] = jnp.zeros(acc_ref.shape, F32)

    hn = hn_ref[...]
    a = _dot(hn, wa_ref[...])
    v = _dot(hn, wv_ref[...])
    ext_ref[SUBLANES:SUBLANES + tm, :] = a
    if sample:
        ext_ref[0:SUBLANES, :] = jnp.zeros((SUBLANES, ext_ref.shape[1]), F32)
        rmod = lax.broadcasted_iota(jnp.int32, (tm, 1), 0) % t_new
        s1 = jnp.where(rmod >= 1, ext_ref[SUBLANES - 1:SUBLANES - 1 + tm, :], p1_ref[...])
        s2 = jnp.where(rmod >= 2, ext_ref[SUBLANES - 2:SUBLANES - 2 + tm, :], p2_ref[...])
        a_ref[...] = a
    else:
        first = i % tiles_per_seq == 0

        @pl.when(first)
        def _():
            ext_ref[0:SUBLANES, :] = jnp.zeros((SUBLANES, ext_ref.shape[1]), F32)

        @pl.when(jnp.logical_not(first))
        def _():
            ext_ref[0:SUBLANES, :] = carry_ref[f]

        tail = a[tm - SUBLANES:, :]
        carry_ref[f] = tail
        tail_ref[...] = tail
        s1 = ext_ref[SUBLANES - 1:SUBLANES - 1 + tm, :]
        s2 = ext_ref[SUBLANES - 2:SUBLANES - 2 + tm, :]
    conv = cb_ref[...] + cw_ref[0:1, :] * s2
    conv = conv + cw_ref[1:2, :] * s1
    conv = conv + cw_ref[2:3, :] * a
    hid = (jax.nn.silu(conv) * v).astype(BF16)
    acc_ref[...] += _dot(hid, wd_ref[...])

    @pl.when(f == pl.num_programs(1) - 1)
    def _():
        x2 = x_ref[...] + acc_ref[...]
        y_ref[...] = _rms(x2, gf_ref[...])


def _ffn(x, row0, rows, g, w_up, conv_w, conv_b, w_down, gf, *, seq=None, prev=None, t_new=None, tf=512):
    d = x.shape[1]
    d_ff = w_down.shape[0]
    nf = d_ff // tf
    tm = ROW_TILE
    blk0 = row0 // tm
    sample = prev is not None
    in_specs = [
        pl.BlockSpec((tm, d), lambda i, f: (blk0 + i, 0)),
        pl.BlockSpec((1, d), lambda i, f: (0, 0)),
        pl.BlockSpec((d, tf), lambda i, f: (0, f)),
        pl.BlockSpec((d, tf), lambda i, f: (0, nf + f)),
        pl.BlockSpec((CONV_W, tf), lambda i, f: (0, f)),
        pl.BlockSpec((1, tf), lambda i, f: (0, f)),
        pl.BlockSpec((tf, d), lambda i, f: (f, 0)),
        pl.BlockSpec((1, d), lambda i, f: (0, 0)),
    ]
    ins = [x, g.reshape(1, d), w_up, w_up, conv_w, conv_b.reshape(1, d_ff), w_down, gf.reshape(1, d)]
    scratch = [pltpu.VMEM((tm, d), BF16), pltpu.VMEM((tm, d), F32), pltpu.VMEM((SUBLANES + tm, tf), F32)]
    y_spec = pl.BlockSpec((tm, d), lambda i, f: (i, 0))
    y_shape = jax.ShapeDtypeStruct((rows, d), F32)
    if sample:
        in_specs += [pl.BlockSpec((tm, tf), lambda i, f: (i, f))] * 2
        ins += list(prev)
        out_specs = [y_spec, pl.BlockSpec((tm, tf), lambda i, f: (i, f))]
        out_shape = [y_shape, jax.ShapeDtypeStruct((rows, d_ff), F32)]
        tiles_per_seq = None
    else:
        out_specs = [y_spec, pl.BlockSpec((None, SUBLANES, tf), lambda i, f: (i, 0, f))]
        out_shape = [y_shape, jax.ShapeDtypeStruct((rows // tm, SUBLANES, d_ff), F32)]
        scratch.append(pltpu.VMEM((nf, SUBLANES, tf), F32))
        tiles_per_seq = seq // tm
    return pl.pallas_call(
        functools.partial(_ffn_kernel, sample=sample, tiles_per_seq=tiles_per_seq, t_new=t_new),
        grid=(rows // tm, nf),
        in_specs=in_specs,
        out_specs=out_specs,
        out_shape=out_shape,
        scratch_shapes=scratch,
        compiler_params=_cparams("arbitrary", "arbitrary"),
        name="ffn_sample" if sample else "ffn_prompt",
    )(*ins)


def _rope_table(pos):
    half = QK_ROPE // 2
    inv = ROPE_BASE ** (-jnp.arange(half, dtype=F32) / half)
    ang = pos[:, None] * inv[None, :]
    c, s = jnp.cos(ang), jnp.sin(ang)
    n = pos.shape[0]
    one, zero = jnp.ones((n, LANES - QK_ROPE), F32), jnp.zeros((n, LANES - QK_ROPE), F32)
    return jnp.concatenate([c, c, one, -s, s, zero], axis=1)


def _layer(xp, xs, dims, cache_c, cache_krt, page_table, state_pool, state_conv,
           norm_attn_g, w_in, b_gates, kv_norm_g, w_uk, w_uv, w_attn_out, w_pool_group, pool_scale,
           w_pool_out, w_o, norm_ffn_g, w_up, conv_w, conv_b, w_down, norm_final_g):
    batch, seq, n_seq, t_new, n_past = dims
    mp, ms = xp.shape[0], xs.shape[0]
    d = xp.shape[1]
    n_prompt = mp // ROW_TILE
    tiles_per_seq = seq // ROW_TILE
    q_cols = N_HEADS * (QK_NOPE + QK_ROPE)
    pool_w = w_pool_out.shape[0]
    c0, c2 = q_cols, q_cols + KV_LORA + QK_ROPE
    c3 = c2 + pool_w

    pos = jnp.concatenate([jnp.arange(seq, dtype=F32), jnp.tile(n_past + jnp.arange(t_new, dtype=F32), ROW_TILE // t_new)])
    tab = _rope_table(pos)

    def tab_row(i):
        return jnp.where(i < n_prompt, i % tiles_per_seq, tiles_per_seq)

    wq = jnp.pad(w_in[:, :c0].reshape(d, N_HEADS, QK_NOPE + QK_ROPE), ((0, 0), (0, 0), (0, HEAD_PAD - QK_NOPE - QK_ROPE)))
    wq = wq.reshape(d, N_HEADS * HEAD_PAD).astype(BF16)
    wck = jnp.pad(w_in[:, c0:c2], ((0, 0), (0, LANES - QK_ROPE))).astype(BF16)
    wu = w_in[:, c2:c3].astype(BF16)
    wg = w_in[:, c3:].astype(BF16)
    w_ukv = jnp.concatenate([w_uk.reshape(KV_LORA, -1), w_uv.reshape(KV_LORA, -1)], axis=1).astype(BF16)
    w_ukt = jnp.transpose(w_uk, (1, 2, 0)).astype(BF16)
    w_uv2 = w_uv.reshape(KV_LORA, -1).astype(BF16)

    h = _norm(xp, xs, norm_attn_g)
    q = _q_proj(h, wq, tab, tab_row)
    ckv_p, ckv_s, kr_p, kr_s = _ckv_proj(h, wck, kv_norm_g, tab, tab_row, mp, ms)
    u = _mm(h, wu, name="pool_proj")

    kp, vp = _kv_up(ckv_p, kr_p, w_ukv)
    o_p = _prompt_attn(q, kp, vp, batch, seq)
    qlat, qr = _q_latent(q, w_ukt, mp, ms)
    olat = _sample_attn(qlat, qr, ckv_s, kr_s, cache_c, cache_krt, page_table, t_new)
    o_s = _o_uv(olat, w_uv2)

    wpg = w_pool_group.astype(BF16)
    ps = pool_scale.reshape(1, pool_w)
    y_p = _pool_prompt(u, wpg, ps, mp, seq)
    u3 = u.reshape((mp + ms) // t_new, t_new, pool_w)
    st_pad = jnp.pad(state_pool, ((0, 0), (HALO - POOL_BUF, 0), (0, 0)))
    y_s = _pool_sample(u3, st_pad, wpg, ps, n_past, mp // t_new)

    merged = _merge(h, o_p, o_s, y_p, y_s, wg, b_gates, w_attn_out.astype(BF16), w_pool_out.astype(BF16))
    x1 = _mm(merged, w_o.astype(BF16), res=(xp, xs), n_prompt=n_prompt, name="o_proj")

    w_up_b = w_up.astype(BF16)
    w_down_b = w_down.astype(BF16)
    y_prompt, tails = _ffn(x1, 0, mp, norm_ffn_g, w_up_b, conv_w, conv_b, w_down_b, norm_final_g, seq=seq)
    d_ff = w_down.shape[0]
    prev1 = jnp.pad(state_conv[:, CONV_W - 2:CONV_W - 1], ((0, 0), (0, t_new - 1), (0, 0))).reshape(ms, d_ff)
    prev2 = jnp.pad(state_conv, ((0, 0), (0, t_new - (CONV_W - 1)), (0, 0))).reshape(ms, d_ff)
    y_sample, a_s = _ffn(x1, mp, ms, norm_ffn_g, w_up_b, conv_w, conv_b, w_down_b, norm_final_g,
                         prev=(prev1, prev2), t_new=t_new)

    conv_p = tails[tiles_per_seq - 1::tiles_per_seq, SUBLANES - (CONV_W - 1):]
    conv_s = a_s.reshape(n_seq, t_new, d_ff)[:, t_new - (CONV_W - 1):]
    pool_p = jnp.stack([u[(b + 1) * seq - POOL_BUF:(b + 1) * seq] for b in range(batch)])
    u_s = u[mp:].reshape(n_seq, t_new, pool_w)
    pool_s = jnp.concatenate([state_pool, u_s], axis=1)[:, -POOL_BUF:]
    return (y_prompt, y_sample, ckv_p.reshape(batch, seq, KV_LORA), kr_p[:, :QK_ROPE].reshape(batch, seq, QK_ROPE),
            ckv_s.reshape(n_seq, t_new, KV_LORA), kr_s[:, :QK_ROPE].reshape(n_seq, t_new, QK_ROPE),
            pool_p, pool_s, conv_p, conv_s)


def kernel(x_prompt, x_sample, cache_kv_latent, cache_k_rope, page_table, state_pool, state_conv, norm_attn_g, w_in, b_gates, kv_norm_g, w_uk, w_uv, w_attn_out, w_pool_group, pool_scale, w_pool_out, w_o, norm_ffn_g, w_up, conv_w, conv_b, w_down, norm_final_g):
    batch, seq, d = x_prompt.shape
    n_seq, t_new, _ = x_sample.shape
    depth = w_in.shape[0]
    assert depth == 1, "the stacked-row pipeline below is written for a single layer"
    n_past = page_table.shape[1] * PAGE_SIZE
    dims = (batch, seq, n_seq, t_new, n_past)
    xp = x_prompt.reshape(batch * seq, d)
    xs = x_sample.reshape(n_seq * t_new, d)
    l = 0
    cache_c = cache_kv_latent.reshape(cache_kv_latent.shape[1:])
    cache_krt = jnp.swapaxes(cache_k_rope.reshape(cache_k_rope.shape[1:]), 1, 2)
    outs = _layer(xp, xs, dims, cache_c, cache_krt, page_table, state_pool[l], state_conv[l],
                  norm_attn_g[l], w_in[l], b_gates[l], kv_norm_g[l], w_uk[l], w_uv[l], w_attn_out[l],
                  w_pool_group[l], pool_scale[l], w_pool_out[l], w_o[l], norm_ffn_g[l], w_up[l], conv_w[l],
                  conv_b[l], w_down[l], norm_final_g)
    y_p, y_s = outs[0].reshape(batch, seq, d), outs[1].reshape(n_seq, t_new, d)
    return (y_p, y_s) + tuple(o[None] for o in outs[2:])
```

```python
import functools
import math

import jax
import jax.numpy as jnp
from jax import lax
from jax.experimental import pallas as pl
from jax.experimental.pallas import tpu as pltpu

F32 = jnp.float32
BF16 = jnp.bfloat16

N_HEADS = 16
QK_NOPE = 128
QK_ROPE = 64
V_DIM = 128
KV_LORA = 512
ROPE_BASE = 10000.0
ATTN_SCALE = (QK_NOPE + QK_ROPE) ** -0.5
EXP2_SCALE = ATTN_SCALE * math.log2(math.e)
PAGE_SIZE = 128
POOL_WINDOWS = (2, 4, 8, 16)
POOL_BUF = max(POOL_WINDOWS) - 1
CONV_W = 3
EPS = 1e-6

LANES = 128
SUBLANES = 8
HEAD_PAD = 2 * LANES
VMEM_LIMIT = 48 * 1024 * 1024

ROW_TILE = 512


def _cparams(*sem):
    return pltpu.CompilerParams(dimension_semantics=sem, vmem_limit_bytes=VMEM_LIMIT)


def _dot(a, b):
    return jnp.dot(a, b, preferred_element_type=F32)


def _dot_nt(a, b):
    return lax.dot_general(a, b, (((1,), (1,)), ((), ())), preferred_element_type=F32)


def _rms(x, g):
    ms = jnp.mean(x * x, axis=-1, keepdims=True)
    return x * lax.rsqrt(ms + EPS) * g


def _rope128(x, ct, st):
    lane = lax.broadcasted_iota(jnp.int32, x.shape, 1)
    other = jnp.where(lane < QK_ROPE // 2, pltpu.roll(x, LANES - QK_ROPE // 2, 1), pltpu.roll(x, QK_ROPE // 2, 1))
    return x * ct + other * st


def _softmax_step(s, vals, state):
    bm = jnp.max(s, axis=-1, keepdims=True)
    if state is None:
        m_new = bm
        p = jnp.exp2((s - m_new) * EXP2_SCALE)
        l_new = jnp.sum(p, axis=-1, keepdims=True)
        acc_new = _dot(p.astype(BF16), vals)
    else:
        m, l, acc = state
        m_new = jnp.maximum(m, bm)
        alpha = jnp.exp2((m - m_new) * EXP2_SCALE)
        p = jnp.exp2((s - m_new) * EXP2_SCALE)
        l_new = alpha * l + jnp.sum(p, axis=-1, keepdims=True)
        acc_new = alpha * acc + _dot(p.astype(BF16), vals)
    return m_new, l_new, acc_new


def _softmax_merge(states):
    m_all = states[0][0]
    for m, _, _ in states[1:]:
        m_all = jnp.maximum(m_all, m)
    l_all = acc_all = None
    for m, l, acc in states:
        w = jnp.exp2((m - m_all) * EXP2_SCALE)
        l_all = w * l if l_all is None else l_all + w * l
        acc_all = w * acc if acc_all is None else acc_all + w * acc
    return m_all, l_all, acc_all


def _prompt_rows(n_prompt):
    return lambda i: jnp.minimum(i, n_prompt - 1)


def _sample_rows(n_prompt):
    return lambda i: jnp.maximum(i - n_prompt, 0)


def _norm_kernel(xp_ref, xs_ref, g_ref, o_ref, *, n_prompt):
    i = pl.program_id(0)

    @pl.when(i < n_prompt)
    def _():
        o_ref[...] = _rms(xp_ref[...], g_ref[...]).astype(o_ref.dtype)

    @pl.when(i >= n_prompt)
    def _():
        o_ref[...] = _rms(xs_ref[...], g_ref[...]).astype(o_ref.dtype)


def _norm(xp, xs, g):
    d = xp.shape[1]
    n_prompt, n_sample = xp.shape[0] // ROW_TILE, xs.shape[0] // ROW_TILE
    pr, sr = _prompt_rows(n_prompt), _sample_rows(n_prompt)
    return pl.pallas_call(
        functools.partial(_norm_kernel, n_prompt=n_prompt),
        grid=(n_prompt + n_sample,),
        in_specs=[
            pl.BlockSpec((ROW_TILE, d), lambda i: (pr(i), 0)),
            pl.BlockSpec((ROW_TILE, d), lambda i: (sr(i), 0)),
            pl.BlockSpec((1, d), lambda i: (0, 0)),
        ],
        out_specs=pl.BlockSpec((ROW_TILE, d), lambda i: (i, 0)),
        out_shape=jax.ShapeDtypeStruct((xp.shape[0] + xs.shape[0], d), BF16),
        compiler_params=_cparams("arbitrary"),
        name="rmsnorm",
    )(xp, xs, g.reshape(1, d))


def _mm_kernel(*refs, n_prompt, has_res):
    h_ref, w_ref = refs[0], refs[1]
    o_ref = refs[-1]
    r = _dot(h_ref[...], w_ref[...])
    if has_res:
        i = pl.program_id(1)

        @pl.when(i < n_prompt)
        def _():
            o_ref[...] = refs[2][...] + r

        @pl.when(i >= n_prompt)
        def _():
            o_ref[...] = refs[3][...] + r
    else:
        o_ref[...] = r.astype(o_ref.dtype)


def _mm(h, w, *, res=None, n_prompt=None, out_dtype=F32, tn=512, name="mm"):
    m, kd = h.shape
    n = w.shape[1]
    ins = [h, w]
    specs = [pl.BlockSpec((ROW_TILE, kd), lambda j, i: (i, 0)), pl.BlockSpec((kd, tn), lambda j, i: (0, j))]
    if res is not None:
        pr, sr = _prompt_rows(n_prompt), _sample_rows(n_prompt)
        ins += list(res)
        specs += [pl.BlockSpec((ROW_TILE, tn), lambda j, i: (pr(i), j)), pl.BlockSpec((ROW_TILE, tn), lambda j, i: (sr(i), j))]
    return pl.pallas_call(
        functools.partial(_mm_kernel, n_prompt=n_prompt, has_res=res is not None),
        grid=(n // tn, m // ROW_TILE),
        in_specs=specs,
        out_specs=pl.BlockSpec((ROW_TILE, tn), lambda j, i: (i, j)),
        out_shape=jax.ShapeDtypeStruct((m, n), out_dtype),
        compiler_params=_cparams("arbitrary", "arbitrary"),
        name=name,
    )(*ins)


def _q_kernel(h_ref, w_ref, tab_ref, o_ref):
    r = _dot(h_ref[...], w_ref[...])
    ct, st = tab_ref[:, :LANES], tab_ref[:, LANES:]
    for hh in range(r.shape[1] // HEAD_PAD):
        c0 = hh * HEAD_PAD
        o_ref[:, c0:c0 + LANES] = r[:, c0:c0 + LANES].astype(o_ref.dtype)
        o_ref[:, c0 + LANES:c0 + HEAD_PAD] = _rope128(r[:, c0 + LANES:c0 + HEAD_PAD], ct, st).astype(o_ref.dtype)


def _q_proj(h, wq, tab, tab_row, tn=1024):
    m, kd = h.shape
    n = wq.shape[1]
    return pl.pallas_call(
        _q_kernel,
        grid=(n // tn, m // ROW_TILE),
        in_specs=[
            pl.BlockSpec((ROW_TILE, kd), lambda j, i: (i, 0)),
            pl.BlockSpec((kd, tn), lambda j, i: (0, j)),
            pl.BlockSpec((ROW_TILE, 2 * LANES), lambda j, i: (tab_row(i), 0)),
        ],
        out_specs=pl.BlockSpec((ROW_TILE, tn), lambda j, i: (i, j)),
        out_shape=jax.ShapeDtypeStruct((m, n), BF16),
        compiler_params=_cparams("arbitrary", "arbitrary"),
        name="q_proj",
    )(h, wq, tab)


def _ckv_kernel(h_ref, w_ref, g_ref, tab_ref, cp_ref, cs_ref, krp_ref, krs_ref, *, n_prompt):
    i = pl.program_id(0)
    r = _dot(h_ref[...], w_ref[...])
    c = _rms(r[:, :KV_LORA], g_ref[...])
    kr = _rope128(r[:, KV_LORA:], tab_ref[:, :LANES], tab_ref[:, LANES:])

    @pl.when(i < n_prompt)
    def _():
        cp_ref[...] = c
        krp_ref[...] = kr

    @pl.when(i >= n_prompt)
    def _():
        cs_ref[...] = c
        krs_ref[...] = kr


def _ckv_proj(h, wck, g, tab, tab_row, mp, ms):
    m, kd = h.shape
    n = wck.shape[1]
    n_prompt = mp // ROW_TILE
    pr, sr = _prompt_rows(n_prompt), _sample_rows(n_prompt)
    return pl.pallas_call(
        functools.partial(_ckv_kernel, n_prompt=n_prompt),
        grid=(m // ROW_TILE,),
        in_specs=[
            pl.BlockSpec((ROW_TILE, kd), lambda i: (i, 0)),
            pl.BlockSpec((kd, n), lambda i: (0, 0)),
            pl.BlockSpec((1, KV_LORA), lambda i: (0, 0)),
            pl.BlockSpec((ROW_TILE, 2 * LANES), lambda i: (tab_row(i), 0)),
        ],
        out_specs=[
            pl.BlockSpec((ROW_TILE, KV_LORA), lambda i: (pr(i), 0)),
            pl.BlockSpec((ROW_TILE, KV_LORA), lambda i: (sr(i), 0)),
            pl.BlockSpec((ROW_TILE, LANES), lambda i: (pr(i), 0)),
            pl.BlockSpec((ROW_TILE, LANES), lambda i: (sr(i), 0)),
        ],
        out_shape=[
            jax.ShapeDtypeStruct((mp, KV_LORA), F32), jax.ShapeDtypeStruct((ms, KV_LORA), F32),
            jax.ShapeDtypeStruct((mp, LANES), F32), jax.ShapeDtypeStruct((ms, LANES), F32),
        ],
        compiler_params=_cparams("arbitrary"),
        name="ckv_proj",
    )(h, wck, g.reshape(1, KV_LORA), tab)


def _kvup_kernel(c_ref, kr_ref, w_ref, k_ref, v_ref):
    r = _dot(c_ref[...].astype(BF16), w_ref[...])
    nk = N_HEADS * QK_NOPE
    v_ref[...] = r[:, nk:].astype(v_ref.dtype)
    krb = kr_ref[...].astype(k_ref.dtype)
    for hh in range(N_HEADS):
        k_ref[:, hh * HEAD_PAD:hh * HEAD_PAD + LANES] = r[:, hh * QK_NOPE:(hh + 1) * QK_NOPE].astype(k_ref.dtype)
        k_ref[:, hh * HEAD_PAD + LANES:(hh + 1) * HEAD_PAD] = krb


def _kv_up(ckv, kr, w_ukv, tm=256):
    rows = ckv.shape[0]
    n = w_ukv.shape[1]
    return pl.pallas_call(
        _kvup_kernel,
        grid=(rows // tm,),
        in_specs=[
            pl.BlockSpec((tm, KV_LORA), lambda i: (i, 0)),
            pl.BlockSpec((tm, LANES), lambda i: (i, 0)),
            pl.BlockSpec((KV_LORA, n), lambda i: (0, 0)),
        ],
        out_specs=[pl.BlockSpec((tm, N_HEADS * HEAD_PAD), lambda i: (i, 0)), pl.BlockSpec((tm, N_HEADS * V_DIM), lambda i: (i, 0))],
        out_shape=[jax.ShapeDtypeStruct((rows, N_HEADS * HEAD_PAD), BF16), jax.ShapeDtypeStruct((rows, N_HEADS * V_DIM), BF16)],
        compiler_params=_cparams("arbitrary"),
        name="kv_up",
    )(ckv, kr, w_ukv)


def _pattn_kernel(q_ref, k_ref, v_ref, o_ref, *, tq):
    nq = q_ref.shape[0] // tq
    row = lax.broadcasted_iota(jnp.int32, (tq, tq), 0)
    col = lax.broadcasted_iota(jnp.int32, (tq, tq), 1)
    for qi in range(nq):
        q = q_ref[qi * tq:(qi + 1) * tq, :]
        state = None
        for ki in range(qi + 1):
            s = _dot_nt(q, k_ref[ki * tq:(ki + 1) * tq, :])
            if ki == qi:
                s = jnp.where(col <= row, s, -jnp.inf)
            state = _softmax_step(s, v_ref[ki * tq:(ki + 1) * tq, :], state)
        _, l, acc = state
        o_ref[qi * tq:(qi + 1) * tq, :] = (acc / l).astype(o_ref.dtype)


def _prompt_attn(q, k, v, batch, seq, tq=512):
    return pl.pallas_call(
        functools.partial(_pattn_kernel, tq=tq),
        grid=(batch, N_HEADS),
        in_specs=[
            pl.BlockSpec((seq, HEAD_PAD), lambda b, h: (b, h)),
            pl.BlockSpec((seq, HEAD_PAD), lambda b, h: (b, h)),
            pl.BlockSpec((seq, V_DIM), lambda b, h: (b, h)),
        ],
        out_specs=pl.BlockSpec((seq, V_DIM), lambda b, h: (b, h)),
        out_shape=jax.ShapeDtypeStruct((batch * seq, N_HEADS * V_DIM), BF16),
        compiler_params=_cparams("arbitrary", "arbitrary"),
        name="prompt_attn",
    )(q, k, v)


def _qlat_kernel(q_ref, w_ref, ql_ref, qr_ref):
    q = q_ref[...]
    ql_ref[...] = _dot(q[:, :QK_NOPE], w_ref[...])
    qr_ref[...] = q[:, LANES:].astype(F32)


def _q_latent(q, w_ukt, row0, rows):
    blk = row0 // rows
    return pl.pallas_call(
        _qlat_kernel,
        grid=(N_HEADS,),
        in_specs=[
            pl.BlockSpec((rows, HEAD_PAD), lambda h: (blk, h)),
            pl.BlockSpec((None, QK_NOPE, KV_LORA), lambda h: (h, 0, 0)),
        ],
        out_specs=[
            pl.BlockSpec((None, rows, KV_LORA), lambda h: (h, 0, 0)),
            pl.BlockSpec((None, rows, LANES), lambda h: (h, 0, 0)),
        ],
        out_shape=[jax.ShapeDtypeStruct((N_HEADS, rows, KV_LORA), F32), jax.ShapeDtypeStruct((N_HEADS, rows, LANES), F32)],
        compiler_params=_cparams("arbitrary"),
        name="q_latent",
    )(q, w_ukt)


KCAT = KV_LORA + LANES


def _sattn_kernel(pt_ref, ql_ref, qr_ref, cn_ref, krn_ref, cache_c, cache_krt, o_ref,
                  qcat, ncat, cbuf, krbuf, kcb, krtb, sem, *, cp, n_chunks, t_new):
    seq = pl.program_id(0)
    n_seq = pl.num_programs(0)
    rows = N_HEADS * t_new
    assert n_chunks % 2 == 0 and n_chunks >= 2

    def copies(sq, chunk, slot):
        out = []
        for k in range(cp):
            page = pt_ref[sq, chunk * cp + k]
            out.append(pltpu.make_async_copy(cache_c.at[page], cbuf.at[slot, k], sem.at[slot]))
            out.append(pltpu.make_async_copy(cache_krt.at[page], krbuf.at[slot, k], sem.at[slot]))
        return out

    def start(sq, chunk, slot):
        for c in copies(sq, chunk, slot):
            c.start()

    def wait(sq, chunk, slot):
        for c in copies(sq, chunk, slot):
            c.wait()

    def scores(slot):
        for k in range(cp):
            kcb[slot, k * PAGE_SIZE:(k + 1) * PAGE_SIZE, :] = cbuf[slot, k].astype(BF16)
            krtb[slot, :, k * PAGE_SIZE:(k + 1) * PAGE_SIZE] = krbuf[slot, k].astype(BF16)
        return _dot_nt(qcat[:, :KV_LORA], kcb[slot]) + _dot(qcat[:, KV_LORA:KV_LORA + QK_ROPE], krtb[slot])

    @pl.when(seq == 0)
    def _():
        start(seq, 0, 0)

    start(seq, 1, 1)

    qcat[:, :KV_LORA] = ql_ref[...].reshape(rows, KV_LORA).astype(BF16)
    qcat[:, KV_LORA:] = qr_ref[...].reshape(rows, LANES).astype(BF16)
    ncat[...] = jnp.zeros(ncat.shape, BF16)
    ncat[0:t_new, :KV_LORA] = cn_ref[...].astype(BF16)
    ncat[0:t_new, KV_LORA:] = krn_ref[...].astype(BF16)
    s_new = _dot_nt(qcat[...], ncat[...])
    row = lax.broadcasted_iota(jnp.int32, s_new.shape, 0)
    col = lax.broadcasted_iota(jnp.int32, s_new.shape, 1)
    s_new = jnp.where(col <= row % t_new, s_new, -jnp.inf)
    state = _softmax_step(s_new, ncat[:, :KV_LORA], None)

    wait(seq, 0, 0)
    s_cur = scores(0)
    for c in range(n_chunks):
        slot = c % 2
        if c + 2 < n_chunks:
            start(seq, c + 2, slot)
        elif c + 2 == n_chunks:
            @pl.when(seq + 1 < n_seq)
            def _():
                start(seq + 1, 0, slot)
        if c + 1 < n_chunks:
            wait(seq, c + 1, 1 - slot)
            s_next = scores(1 - slot)
        state = _softmax_merge([state, _softmax_step(s_cur, kcb[slot], None)])
        if c + 1 < n_chunks:
            s_cur = s_next

    _, l, acc = state
    o_ref[...] = (acc / l).reshape(o_ref.shape)


def _sample_attn(qlat, qr, ckv_s, kr_s, cache_c, cache_krt, page_table, t_new, cp=8):
    n_seq, n_pages = page_table.shape
    rows = N_HEADS * t_new
    grid_spec = pltpu.PrefetchScalarGridSpec(
        num_scalar_prefetch=1,
        grid=(n_seq,),
        in_specs=[
            pl.BlockSpec((N_HEADS, t_new, KV_LORA), lambda s, pt: (0, s, 0)),
            pl.BlockSpec((N_HEADS, t_new, LANES), lambda s, pt: (0, s, 0)),
            pl.BlockSpec((t_new, KV_LORA), lambda s, pt: (s, 0)),
            pl.BlockSpec((t_new, LANES), lambda s, pt: (s, 0)),
            pl.BlockSpec(memory_space=pl.ANY),
            pl.BlockSpec(memory_space=pl.ANY),
        ],
        out_specs=pl.BlockSpec((N_HEADS, t_new, KV_LORA), lambda s, pt: (0, s, 0)),
        scratch_shapes=[
            pltpu.VMEM((rows, KCAT), BF16),
            pltpu.VMEM((PAGE_SIZE, KCAT), BF16),
            pltpu.VMEM((2, cp, PAGE_SIZE, KV_LORA), F32),
            pltpu.VMEM((2, cp, QK_ROPE, PAGE_SIZE), F32),
            pltpu.VMEM((2, cp * PAGE_SIZE, KV_LORA), BF16),
            pltpu.VMEM((2, QK_ROPE, cp * PAGE_SIZE), BF16),
            pltpu.SemaphoreType.DMA((2,)),
        ],
    )
    return pl.pallas_call(
        functools.partial(_sattn_kernel, cp=cp, n_chunks=n_pages // cp, t_new=t_new),
        grid_spec=grid_spec,
        out_shape=jax.ShapeDtypeStruct((N_HEADS, n_seq * t_new, KV_LORA), F32),
        compiler_params=_cparams("arbitrary"),
        name="sample_attn",
    )(page_table, qlat, qr, ckv_s, kr_s, cache_c, cache_krt)


def _ouv_kernel(o_ref, w_ref, out_ref):
    out_ref[...] = _dot(o_ref[...].astype(BF16), w_ref[...]).astype(out_ref.dtype)


def _o_uv(olat, w_uv2):
    _, rows, _ = olat.shape
    return pl.pallas_call(
        _ouv_kernel,
        grid=(N_HEADS,),
        in_specs=[
            pl.BlockSpec((None, rows, KV_LORA), lambda h: (h, 0, 0)),
            pl.BlockSpec((KV_LORA, V_DIM), lambda h: (0, h)),
        ],
        out_specs=pl.BlockSpec((rows, V_DIM), lambda h: (0, h)),
        out_shape=jax.ShapeDtypeStruct((rows, N_HEADS * V_DIM), BF16),
        compiler_params=_cparams("arbitrary"),
        name="o_uv",
    )(olat, w_uv2)


HALO = 2 * SUBLANES


def _pool_groups(load, u_of, cnt_of, wg_ref, ps_ref, store):
    group = wg_ref.shape[1]
    for g, w in enumerate(POOL_WINDOWS):
        sl = slice(g * group, (g + 1) * group)
        ws = load(0, sl)
        for k in range(1, w):
            ws = ws + load(k, sl)
        pooled = ws / cnt_of(w) - u_of(sl)
        pooled = pooled.reshape(-1, group).astype(BF16)
        store(sl, (_dot(pooled, wg_ref[g]) * ps_ref[:, sl]))


def _pool_prompt_kernel(u_ref, wg_ref, ps_ref, y_ref, ext_ref, *, tiles_per_seq):
    i = pl.program_id(0)
    tm = u_ref.shape[0]
    it = i % tiles_per_seq

    @pl.when(it == 0)
    def _():
        ext_ref[0:HALO, :] = jnp.zeros((HALO, ext_ref.shape[1]), F32)

    @pl.when(it != 0)
    def _():
        ext_ref[0:HALO, :] = ext_ref[tm:tm + HALO, :]

    ext_ref[HALO:HALO + tm, :] = u_ref[...]
    pos = it * tm + lax.broadcasted_iota(jnp.int32, (tm, 1), 0)

    def store(sl, val):
        y_ref[:, sl] = val.astype(y_ref.dtype)

    _pool_groups(
        lambda k, sl: ext_ref[HALO - k:HALO - k + tm, sl],
        lambda sl: u_ref[:, sl],
        lambda w: jnp.minimum(pos + 1, w).astype(F32),
        wg_ref, ps_ref, store)


def _pool_prompt(u, wg, ps, rows, seq):
    width = wg.shape[0] * wg.shape[1]
    tm = ROW_TILE
    return pl.pallas_call(
        functools.partial(_pool_prompt_kernel, tiles_per_seq=seq // tm),
        grid=(rows // tm,),
        in_specs=[
            pl.BlockSpec((tm, width), lambda i: (i, 0)),
            pl.BlockSpec(wg.shape, lambda i: (0, 0, 0)),
            pl.BlockSpec((1, width), lambda i: (0, 0)),
        ],
        out_specs=pl.BlockSpec((tm, width), lambda i: (i, 0)),
        out_shape=jax.ShapeDtypeStruct((rows, width), BF16),
        scratch_shapes=[pltpu.VMEM((HALO + tm, width), F32)],
        compiler_params=_cparams("arbitrary"),
        name="pool_prompt",
    )(u, wg, ps)


def _pool_sample_kernel(u_ref, st_ref, wg_ref, ps_ref, y_ref, ext_ref, *, pos0):
    t_new = u_ref.shape[1]
    ext_ref[:, 0:HALO, :] = st_ref[...]
    ext_ref[:, HALO:HALO + t_new, :] = u_ref[...]
    pos = pos0 + lax.broadcasted_iota(jnp.int32, (1, t_new, 1), 1)

    def store(sl, val):
        y_ref[:, sl] = val.astype(y_ref.dtype)

    _pool_groups(
        lambda k, sl: ext_ref[:, HALO - k:HALO - k + t_new, sl],
        lambda sl: u_ref[:, :, sl],
        lambda w: jnp.minimum(pos + 1, w).astype(F32),
        wg_ref, ps_ref, store)


def _pool_sample(u3, st, wg, ps, pos0, seq0, ns=64):
    n_seq = st.shape[0]
    _, t_new, width = u3.shape
    blk0 = seq0 // ns
    return pl.pallas_call(
        functools.partial(_pool_sample_kernel, pos0=pos0),
        grid=(n_seq // ns,),
        in_specs=[
            pl.BlockSpec((ns, t_new, width), lambda i: (blk0 + i, 0, 0)),
            pl.BlockSpec((ns, HALO, width), lambda i: (i, 0, 0)),
            pl.BlockSpec(wg.shape, lambda i: (0, 0, 0)),
            pl.BlockSpec((1, width), lambda i: (0, 0)),
        ],
        out_specs=pl.BlockSpec((ns * t_new, width), lambda i: (i, 0)),
        out_shape=jax.ShapeDtypeStruct((n_seq * t_new, width), BF16),
        scratch_shapes=[pltpu.VMEM((ns, HALO + t_new, width), F32)],
        compiler_params=_cparams("arbitrary"),
        name="pool_sample",
    )(u3, st, wg, ps)


def _merge_kernel(h_ref, oap_ref, oas_ref, ypp_ref, yps_ref, wga_ref, wgb_ref, ba_ref, bb_ref, wa_ref, wp_ref, o_ref,
                  *, n_prompt):
    i = pl.program_id(1)
    h = h_ref[...]
    ga = jax.nn.sigmoid(_dot(h, wga_ref[...]) + ba_ref[...])
    gb = jax.nn.sigmoid(_dot(h, wgb_ref[...]) + bb_ref[...])

    def finish(oa_ref, yp_ref):
        a = _dot(oa_ref[...], wa_ref[...])
        p = _dot(yp_ref[...], wp_ref[...])
        o_ref[...] = (ga * a + gb * p).astype(o_ref.dtype)

    @pl.when(i < n_prompt)
    def _():
        finish(oap_ref, ypp_ref)

    @pl.when(i >= n_prompt)
    def _():
        finish(oas_ref, yps_ref)


def _merge(h, oa_p, oa_s, yp_p, yp_s, wg, b_gates, wa, wp, tn=512):
    m, d = h.shape
    ka, kp = wa.shape[0], wp.shape[0]
    n = wa.shape[1]
    nb = n // tn
    n_prompt = oa_p.shape[0] // ROW_TILE
    pr, sr = _prompt_rows(n_prompt), _sample_rows(n_prompt)
    bg = b_gates.reshape(1, -1)
    return pl.pallas_call(
        functools.partial(_merge_kernel, n_prompt=n_prompt),
        grid=(nb, m // ROW_TILE),
        in_specs=[
            pl.BlockSpec((ROW_TILE, d), lambda j, i: (i, 0)),
            pl.BlockSpec((ROW_TILE, ka), lambda j, i: (pr(i), 0)),
            pl.BlockSpec((ROW_TILE, ka), lambda j, i: (sr(i), 0)),
            pl.BlockSpec((ROW_TILE, kp), lambda j, i: (pr(i), 0)),
            pl.BlockSpec((ROW_TILE, kp), lambda j, i: (sr(i), 0)),
            pl.BlockSpec((d, tn), lambda j, i: (0, j)),
            pl.BlockSpec((d, tn), lambda j, i: (0, nb + j)),
            pl.BlockSpec((1, tn), lambda j, i: (0, j)),
            pl.BlockSpec((1, tn), lambda j, i: (0, nb + j)),
            pl.BlockSpec((ka, tn), lambda j, i: (0, j)),
            pl.BlockSpec((kp, tn), lambda j, i: (0, j)),
        ],
        out_specs=pl.BlockSpec((ROW_TILE, tn), lambda j, i: (i, j)),
        out_shape=jax.ShapeDtypeStruct((m, n), BF16),
        compiler_params=_cparams("arbitrary", "arbitrary"),
        name="merge",
    )(h, oa_p, oa_s, yp_p, yp_s, wg, wg, bg, bg, wa, wp)


def _ffn_kernel(*refs, sample, tiles_per_seq, t_new):
    if sample:
        (x_ref, g_ref, wa_ref, wv_ref, cw_ref, cb_ref, wd_ref, gf_ref, p1_ref, p2_ref,
         y_ref, a_ref, hn_ref, acc_ref, ext_ref) = refs
    else:
        (x_ref, g_ref, wa_ref, wv_ref, cw_ref, cb_ref, wd_ref, gf_ref,
         y_ref, tail_ref, hn_ref, acc_ref, ext_ref, carry_ref) = refs
    i, f = pl.program_id(0), pl.program_id(1)
    tm = x_ref.shape[0]

    @pl.when(f == 0)
    def _():
        hn_ref[...] = _rms(x_ref[...], g_ref[...]).astype(hn_ref.dtype)
        acc_ref[...] = jnp.zeros(acc_ref.shape, F32)

    hn = hn_ref[...]
    a = _dot(hn, wa_ref[...])
    v = _dot(hn, wv_ref[...])
    ext_ref[SUBLANES:SUBLANES + tm, :] = a
    if sample:
        ext_ref[0:SUBLANES, :] = jnp.zeros((SUBLANES, ext_ref.shape[1]), F32)
        rmod = lax.broadcasted_iota(jnp.int32, (tm, 1), 0) % t_new
        s1 = jnp.where(rmod >= 1, ext_ref[SUBLANES - 1:SUBLANES - 1 + tm, :], p1_ref[...])
        s2 = jnp.where(rmod >= 2, ext_ref[SUBLANES - 2:SUBLANES - 2 + tm, :], p2_ref[...])
        a_ref[...] = a
    else:
        first = i % tiles_per_seq == 0

        @pl.when(first)
        def _():
            ext_ref[0:SUBLANES, :] = jnp.zeros((SUBLANES, ext_ref.shape[1]), F32)

        @pl.when(jnp.logical_not(first))
        def _():
            ext_ref[0:SUBLANES, :] = carry_ref[f]

        tail = a[tm - SUBLANES:, :]
        carry_ref[f] = tail
        tail_ref[...] = tail
        s1 = ext_ref[SUBLANES - 1:SUBLANES - 1 + tm, :]
        s2 = ext_ref[SUBLANES - 2:SUBLANES - 2 + tm, :]
    conv = cb_ref[...] + cw_ref[0:1, :] * s2
    conv = conv + cw_ref[1:2, :] * s1
    conv = conv + cw_ref[2:3, :] * a
    hid = (jax.nn.silu(conv) * v).astype(BF16)
    acc_ref[...] += _dot(hid, wd_ref[...])

    @pl.when(f == pl.num_programs(1) - 1)
    def _():
        x2 = x_ref[...] + acc_ref[...]
        y_ref[...] = _rms(x2, gf_ref[...])


def _ffn(x, row0, rows, g, w_up, conv_w, conv_b, w_down, gf, *, seq=None, prev=None, t_new=None, tf=512):
    d = x.shape[1]
    d_ff = w_down.shape[0]
    nf = d_ff // tf
    tm = ROW_TILE
    blk0 = row0 // tm
    sample = prev is not None
    in_specs = [
        pl.BlockSpec((tm, d), lambda i, f: (blk0 + i, 0)),
        pl.BlockSpec((1, d), lambda i, f: (0, 0)),
        pl.BlockSpec((d, tf), lambda i, f: (0, f)),
        pl.BlockSpec((d, tf), lambda i, f: (0, nf + f)),
        pl.BlockSpec((CONV_W, tf), lambda i, f: (0, f)),
        pl.BlockSpec((1, tf), lambda i, f: (0, f)),
        pl.BlockSpec((tf, d), lambda i, f: (f, 0)),
        pl.BlockSpec((1, d), lambda i, f: (0, 0)),
    ]
    ins = [x, g.reshape(1, d), w_up, w_up, conv_w, conv_b.reshape(1, d_ff), w_down, gf.reshape(1, d)]
    scratch = [pltpu.VMEM((tm, d), BF16), pltpu.VMEM((tm, d), F32), pltpu.VMEM((SUBLANES + tm, tf), F32)]
    y_spec = pl.BlockSpec((tm, d), lambda i, f: (i, 0))
    y_shape = jax.ShapeDtypeStruct((rows, d), F32)
    if sample:
        in_specs += [pl.BlockSpec((tm, tf), lambda i, f: (i, f))] * 2
        ins += list(prev)
        out_specs = [y_spec, pl.BlockSpec((tm, tf), lambda i, f: (i, f))]
        out_shape = [y_shape, jax.ShapeDtypeStruct((rows, d_ff), F32)]
        tiles_per_seq = None
    else:
        out_specs = [y_spec, pl.BlockSpec((None, SUBLANES, tf), lambda i, f: (i, 0, f))]
        out_shape = [y_shape, jax.ShapeDtypeStruct((rows // tm, SUBLANES, d_ff), F32)]
        scratch.append(pltpu.VMEM((nf, SUBLANES, tf), F32))
        tiles_per_seq = seq // tm
    return pl.pallas_call(
        functools.partial(_ffn_kernel, sample=sample, tiles_per_seq=tiles_per_seq, t_new=t_new),
        grid=(rows // tm, nf),
        in_specs=in_specs,
        out_specs=out_specs,
        out_shape=out_shape,
        scratch_shapes=scratch,
        compiler_params=_cparams("arbitrary", "arbitrary"),
        name="ffn_sample" if sample else "ffn_prompt",
    )(*ins)


def _rope_table(pos):
    half = QK_ROPE // 2
    inv = ROPE_BASE ** (-jnp.arange(half, dtype=F32) / half)
    ang = pos[:, None] * inv[None, :]
    c, s = jnp.cos(ang), jnp.sin(ang)
    n = pos.shape[0]
    one, zero = jnp.ones((n, LANES - QK_ROPE), F32), jnp.zeros((n, LANES - QK_ROPE), F32)
    return jnp.concatenate([c, c, one, -s, s, zero], axis=1)


def _layer(xp, xs, dims, cache_c, cache_krt, page_table, state_pool, state_conv,
           norm_attn_g, w_in, b_gates, kv_norm_g, w_uk, w_uv, w_attn_out, w_pool_group, pool_scale,
           w_pool_out, w_o, norm_ffn_g, w_up, conv_w, conv_b, w_down, norm_final_g):
    batch, seq, n_seq, t_new, n_past = dims
    mp, ms = xp.shape[0], xs.shape[0]
    d = xp.shape[1]
    n_prompt = mp // ROW_TILE
    tiles_per_seq = seq // ROW_TILE
    q_cols = N_HEADS * (QK_NOPE + QK_ROPE)
    pool_w = w_pool_out.shape[0]
    c0, c2 = q_cols, q_cols + KV_LORA + QK_ROPE
    c3 = c2 + pool_w

    pos = jnp.concatenate([jnp.arange(seq, dtype=F32), jnp.tile(n_past + jnp.arange(t_new, dtype=F32), ROW_TILE // t_new)])
    tab = _rope_table(pos)

    def tab_row(i):
        return jnp.where(i < n_prompt, i % tiles_per_seq, tiles_per_seq)

    wq = jnp.pad(w_in[:, :c0].reshape(d, N_HEADS, QK_NOPE + QK_ROPE), ((0, 0), (0, 0), (0, HEAD_PAD - QK_NOPE - QK_ROPE)))
    wq = wq.reshape(d, N_HEADS * HEAD_PAD).astype(BF16)
    wck = jnp.pad(w_in[:, c0:c2], ((0, 0), (0, LANES - QK_ROPE))).astype(BF16)
    wu = w_in[:, c2:c3].astype(BF16)
    wg = w_in[:, c3:].astype(BF16)
    w_ukv = jnp.concatenate([w_uk.reshape(KV_LORA, -1), w_uv.reshape(KV_LORA, -1)], axis=1).astype(BF16)
    w_ukt = jnp.transpose(w_uk, (1, 2, 0)).astype(BF16)
    w_uv2 = w_uv.reshape(KV_LORA, -1).astype(BF16)

    h = _norm(xp, xs, norm_attn_g)
    q = _q_proj(h, wq, tab, tab_row)
    ckv_p, ckv_s, kr_p, kr_s = _ckv_proj(h, wck, kv_norm_g, tab, tab_row, mp, ms)
    u = _mm(h, wu, tn=pool_w, name="pool_proj")

    kp, vp = _kv_up(ckv_p, kr_p, w_ukv)
    o_p = _prompt_attn(q, kp, vp, batch, seq)
    qlat, qr = _q_latent(q, w_ukt, mp, ms)
    olat = _sample_attn(qlat, qr, ckv_s, kr_s, cache_c, cache_krt, page_table, t_new)
    o_s = _o_uv(olat, w_uv2)

    wpg = w_pool_group.astype(BF16)
    ps = pool_scale.reshape(1, pool_w)
    y_p = _pool_prompt(u, wpg, ps, mp, seq)
    u3 = u.reshape((mp + ms) // t_new, t_new, pool_w)
    st_pad = jnp.pad(state_pool, ((0, 0), (HALO - POOL_BUF, 0), (0, 0)))
    y_s = _pool_sample(u3, st_pad, wpg, ps, n_past, mp // t_new)

    merged = _merge(h, o_p, o_s, y_p, y_s, wg, b_gates, w_attn_out.astype(BF16), w_pool_out.astype(BF16))
    x1 = _mm(merged, w_o.astype(BF16), res=(xp, xs), n_prompt=n_prompt, tn=d, name="o_proj")

    w_up_b = w_up.astype(BF16)
    w_down_b = w_down.astype(BF16)
    y_prompt, tails = _ffn(x1, 0, mp, norm_ffn_g, w_up_b, conv_w, conv_b, w_down_b, norm_final_g, seq=seq)
    d_ff = w_down.shape[0]
    prev1 = jnp.pad(state_conv[:, CONV_W - 2:CONV_W - 1], ((0, 0), (0, t_new - 1), (0, 0))).reshape(ms, d_ff)
    prev2 = jnp.pad(state_conv, ((0, 0), (0, t_new - (CONV_W - 1)), (0, 0))).reshape(ms, d_ff)
    y_sample, a_s = _ffn(x1, mp, ms, norm_ffn_g, w_up_b, conv_w, conv_b, w_down_b, norm_final_g,
                         prev=(prev1, prev2), t_new=t_new)

    conv_p = tails[tiles_per_seq - 1::tiles_per_seq, SUBLANES - (CONV_W - 1):]
    conv_s = a_s.reshape(n_seq, t_new, d_ff)[:, t_new - (CONV_W - 1):]
    pool_p = jnp.stack([u[(b + 1) * seq - POOL_BUF:(b + 1) * seq] for b in range(batch)])
    u_s = u[mp:].reshape(n_seq, t_new, pool_w)
    pool_s = jnp.concatenate([state_pool, u_s], axis=1)[:, -POOL_BUF:]
    return (y_prompt, y_sample, ckv_p.reshape(batch, seq, KV_LORA), kr_p[:, :QK_ROPE].reshape(batch, seq, QK_ROPE),
            ckv_s.reshape(n_seq, t_new, KV_LORA), kr_s[:, :QK_ROPE].reshape(n_seq, t_new, QK_ROPE),
            pool_p, pool_s, conv_p, conv_s)


def kernel(x_prompt, x_sample, cache_kv_latent, cache_k_rope, page_table, state_pool, state_conv, norm_attn_g, w_in, b_gates, kv_norm_g, w_uk, w_uv, w_attn_out, w_pool_group, pool_scale, w_pool_out, w_o, norm_ffn_g, w_up, conv_w, conv_b, w_down, norm_final_g):
    batch, seq, d = x_prompt.shape
    n_seq, t_new, _ = x_sample.shape
    depth = w_in.shape[0]
    assert depth == 1, "the stacked-row pipeline below is written for a single layer"
    n_past = page_table.shape[1] * PAGE_SIZE
    dims = (batch, seq, n_seq, t_new, n_past)
    xp = x_prompt.reshape(batch * seq, d)
    xs = x_sample.reshape(n_seq * t_new, d)
    l = 0
    cache_c = cache_kv_latent.reshape(cache_kv_latent.shape[1:])
    cache_krt = jnp.swapaxes(cache_k_rope.reshape(cache_k_rope.shape[1:]), 1, 2)
    outs = _layer(xp, xs, dims, cache_c, cache_krt, page_table, state_pool[l], state_conv[l],
                  norm_attn_g[l], w_in[l], b_gates[l], kv_norm_g[l], w_uk[l], w_uv[l], w_attn_out[l],
                  w_pool_group[l], pool_scale[l], w_pool_out[l], w_o[l], norm_ffn_g[l], w_up[l], conv_w[l],
                  conv_b[l], w_down[l], norm_final_g)
    y_p, y_s = outs[0].reshape(batch, seq, d), outs[1].reshape(n_seq, t_new, d)
    return (y_p, y_s) + tuple(o[None] for o in outs[2:])
```

```python
import functools
import math

import jax
import jax.numpy as jnp
from jax import lax
from jax.experimental import pallas as pl
from jax.experimental.pallas import tpu as pltpu

F32 = jnp.float32
BF16 = jnp.bfloat16

N_HEADS = 16
QK_NOPE = 128
QK_ROPE = 64
V_DIM = 128
KV_LORA = 512
ROPE_BASE = 10000.0
ATTN_SCALE = (QK_NOPE + QK_ROPE) ** -0.5
EXP2_SCALE = ATTN_SCALE * math.log2(math.e)
PAGE_SIZE = 128
POOL_WINDOWS = (2, 4, 8, 16)
POOL_BUF = max(POOL_WINDOWS) - 1
CONV_W = 3
EPS = 1e-6

LANES = 128
SUBLANES = 8
HEAD_PAD = 2 * LANES
VMEM_LIMIT = 48 * 1024 * 1024

ROW_TILE = 512


def _cparams(*sem):
    return pltpu.CompilerParams(dimension_semantics=sem, vmem_limit_bytes=VMEM_LIMIT)


def _dot(a, b):
    return jnp.dot(a, b, preferred_element_type=F32)


def _dot_nt(a, b):
    return lax.dot_general(a, b, (((1,), (1,)), ((), ())), preferred_element_type=F32)


def _rms(x, g):
    ms = jnp.mean(x * x, axis=-1, keepdims=True)
    return x * lax.rsqrt(ms + EPS) * g


def _rope128(x, ct, st):
    lane = lax.broadcasted_iota(jnp.int32, x.shape, 1)
    other = jnp.where(lane < QK_ROPE // 2, pltpu.roll(x, LANES - QK_ROPE // 2, 1), pltpu.roll(x, QK_ROPE // 2, 1))
    return x * ct + other * st


def _softmax_step(s, vals, state):
    bm = jnp.max(s, axis=-1, keepdims=True)
    if state is None:
        m_new = bm
        p = jnp.exp2((s - m_new) * EXP2_SCALE)
        l_new = jnp.sum(p, axis=-1, keepdims=True)
        acc_new = _dot(p.astype(BF16), vals)
    else:
        m, l, acc = state
        m_new = jnp.maximum(m, bm)
        alpha = jnp.exp2((m - m_new) * EXP2_SCALE)
        p = jnp.exp2((s - m_new) * EXP2_SCALE)
        l_new = alpha * l + jnp.sum(p, axis=-1, keepdims=True)
        acc_new = alpha * acc + _dot(p.astype(BF16), vals)
    return m_new, l_new, acc_new


def _softmax_merge(states):
    m_all = states[0][0]
    for m, _, _ in states[1:]:
        m_all = jnp.maximum(m_all, m)
    l_all = acc_all = None
    for m, l, acc in states:
        w = jnp.exp2((m - m_all) * EXP2_SCALE)
        l_all = w * l if l_all is None else l_all + w * l
        acc_all = w * acc if acc_all is None else acc_all + w * acc
    return m_all, l_all, acc_all


def _prompt_rows(n_prompt):
    return lambda i: jnp.minimum(i, n_prompt - 1)


def _sample_rows(n_prompt):
    return lambda i: jnp.maximum(i - n_prompt, 0)


def _norm_kernel(xp_ref, xs_ref, g_ref, o_ref, *, n_prompt):
    i = pl.program_id(0)

    @pl.when(i < n_prompt)
    def _():
        o_ref[...] = _rms(xp_ref[...], g_ref[...]).astype(o_ref.dtype)

    @pl.when(i >= n_prompt)
    def _():
        o_ref[...] = _rms(xs_ref[...], g_ref[...]).astype(o_ref.dtype)


def _norm(xp, xs, g):
    d = xp.shape[1]
    n_prompt, n_sample = xp.shape[0] // ROW_TILE, xs.shape[0] // ROW_TILE
    pr, sr = _prompt_rows(n_prompt), _sample_rows(n_prompt)
    return pl.pallas_call(
        functools.partial(_norm_kernel, n_prompt=n_prompt),
        grid=(n_prompt + n_sample,),
        in_specs=[
            pl.BlockSpec((ROW_TILE, d), lambda i: (pr(i), 0)),
            pl.BlockSpec((ROW_TILE, d), lambda i: (sr(i), 0)),
            pl.BlockSpec((1, d), lambda i: (0, 0)),
        ],
        out_specs=pl.BlockSpec((ROW_TILE, d), lambda i: (i, 0)),
        out_shape=jax.ShapeDtypeStruct((xp.shape[0] + xs.shape[0], d), BF16),
        compiler_params=_cparams("arbitrary"),
        name="rmsnorm",
    )(xp, xs, g.reshape(1, d))


def _mm_kernel(*refs, n_prompt, has_res):
    h_ref, w_ref = refs[0], refs[1]
    o_ref = refs[-1]
    r = _dot(h_ref[...], w_ref[...])
    if has_res:
        i = pl.program_id(1)

        @pl.when(i < n_prompt)
        def _():
            o_ref[...] = refs[2][...] + r

        @pl.when(i >= n_prompt)
        def _():
            o_ref[...] = refs[3][...] + r
    else:
        o_ref[...] = r.astype(o_ref.dtype)


def _mm(h, w, *, res=None, n_prompt=None, out_dtype=F32, tn=512, name="mm"):
    m, kd = h.shape
    n = w.shape[1]
    ins = [h, w]
    specs = [pl.BlockSpec((ROW_TILE, kd), lambda j, i: (i, 0)), pl.BlockSpec((kd, tn), lambda j, i: (0, j))]
    if res is not None:
        pr, sr = _prompt_rows(n_prompt), _sample_rows(n_prompt)
        ins += list(res)
        specs += [pl.BlockSpec((ROW_TILE, tn), lambda j, i: (pr(i), j)), pl.BlockSpec((ROW_TILE, tn), lambda j, i: (sr(i), j))]
    return pl.pallas_call(
        functools.partial(_mm_kernel, n_prompt=n_prompt, has_res=res is not None),
        grid=(n // tn, m // ROW_TILE),
        in_specs=specs,
        out_specs=pl.BlockSpec((ROW_TILE, tn), lambda j, i: (i, j)),
        out_shape=jax.ShapeDtypeStruct((m, n), out_dtype),
        compiler_params=_cparams("arbitrary", "arbitrary"),
        name=name,
    )(*ins)


def _q_kernel(h_ref, wt_ref, tab_ref, o_ref, wb_ref):
    hd = QK_NOPE + QK_ROPE

    @pl.when(pl.program_id(1) == 0)
    def _():
        for hh in range(wb_ref.shape[0] // HEAD_PAD):
            wb_ref[hh * HEAD_PAD:hh * HEAD_PAD + hd, :] = wt_ref[hh * hd:(hh + 1) * hd, :].astype(BF16)
            wb_ref[hh * HEAD_PAD + hd:(hh + 1) * HEAD_PAD, :] = jnp.zeros((HEAD_PAD - hd, wb_ref.shape[1]), BF16)

    r = _dot_nt(h_ref[...], wb_ref[...])
    ct, st = tab_ref[:, :LANES], tab_ref[:, LANES:]
    for hh in range(r.shape[1] // HEAD_PAD):
        c0 = hh * HEAD_PAD
        o_ref[:, c0:c0 + LANES] = r[:, c0:c0 + LANES].astype(o_ref.dtype)
        o_ref[:, c0 + LANES:c0 + HEAD_PAD] = _rope128(r[:, c0 + LANES:c0 + HEAD_PAD], ct, st).astype(o_ref.dtype)


def _q_proj(h, w_t, tab, tab_row, heads_per_block=4):
    m, kd = h.shape
    hd = QK_NOPE + QK_ROPE
    tn = heads_per_block * HEAD_PAD
    return pl.pallas_call(
        _q_kernel,
        grid=(N_HEADS // heads_per_block, m // ROW_TILE),
        in_specs=[
            pl.BlockSpec((ROW_TILE, kd), lambda j, i: (i, 0)),
            pl.BlockSpec((heads_per_block * hd, kd), lambda j, i: (j, 0)),
            pl.BlockSpec((ROW_TILE, 2 * LANES), lambda j, i: (tab_row(i), 0)),
        ],
        out_specs=pl.BlockSpec((ROW_TILE, tn), lambda j, i: (i, j)),
        out_shape=jax.ShapeDtypeStruct((m, N_HEADS * HEAD_PAD), BF16),
        scratch_shapes=[pltpu.VMEM((tn, kd), BF16)],
        compiler_params=_cparams("arbitrary", "arbitrary"),
        name="q_proj",
    )(h, w_t, tab)


def _ckv_kernel(h_ref, wt_hbm, g_ref, tab_ref, cp_ref, cs_ref, krp_ref, krs_ref, u_ref, wf_ref, wb_ref, sem,
                *, row0, n_prompt):
    i = pl.program_id(0)
    n_ck = KV_LORA + QK_ROPE
    n_ck_pad = KV_LORA + LANES

    @pl.when(i == 0)
    def _():
        cp = pltpu.make_async_copy(wt_hbm.at[pl.ds(row0, wf_ref.shape[0])], wf_ref, sem)
        cp.start()
        cp.wait()
        wb_ref[0:n_ck, :] = wf_ref[0:n_ck, :].astype(BF16)
        wb_ref[n_ck:n_ck_pad, :] = jnp.zeros((n_ck_pad - n_ck, wb_ref.shape[1]), BF16)
        wb_ref[n_ck_pad:, :] = wf_ref[n_ck:, :].astype(BF16)

    r = _dot_nt(h_ref[...], wb_ref[...])
    c = _rms(r[:, :KV_LORA], g_ref[...])
    kr = _rope128(r[:, KV_LORA:n_ck_pad], tab_ref[:, :LANES], tab_ref[:, LANES:])
    u_ref[...] = r[:, n_ck_pad:]

    @pl.when(i < n_prompt)
    def _():
        cp_ref[...] = c
        krp_ref[...] = kr

    @pl.when(i >= n_prompt)
    def _():
        cs_ref[...] = c
        krs_ref[...] = kr


def _ckv_pool_proj(h, w_t, row0, pool_w, g, tab, tab_row, mp, ms):
    m, kd = h.shape
    n_rows = KV_LORA + QK_ROPE + pool_w
    n_pad = KV_LORA + LANES + pool_w
    n_prompt = mp // ROW_TILE
    pr, sr = _prompt_rows(n_prompt), _sample_rows(n_prompt)
    return pl.pallas_call(
        functools.partial(_ckv_kernel, row0=row0, n_prompt=n_prompt),
        grid=(m // ROW_TILE,),
        in_specs=[
            pl.BlockSpec((ROW_TILE, kd), lambda i: (i, 0)),
            pl.BlockSpec(memory_space=pl.ANY),
            pl.BlockSpec((1, KV_LORA), lambda i: (0, 0)),
            pl.BlockSpec((ROW_TILE, 2 * LANES), lambda i: (tab_row(i), 0)),
        ],
        out_specs=[
            pl.BlockSpec((ROW_TILE, KV_LORA), lambda i: (pr(i), 0)),
            pl.BlockSpec((ROW_TILE, KV_LORA), lambda i: (sr(i), 0)),
            pl.BlockSpec((ROW_TILE, LANES), lambda i: (pr(i), 0)),
            pl.BlockSpec((ROW_TILE, LANES), lambda i: (sr(i), 0)),
            pl.BlockSpec((ROW_TILE, pool_w), lambda i: (i, 0)),
        ],
        out_shape=[
            jax.ShapeDtypeStruct((mp, KV_LORA), F32), jax.ShapeDtypeStruct((ms, KV_LORA), F32),
            jax.ShapeDtypeStruct((mp, LANES), F32), jax.ShapeDtypeStruct((ms, LANES), F32),
            jax.ShapeDtypeStruct((m, pool_w), F32),
        ],
        scratch_shapes=[pltpu.VMEM((n_rows, kd), F32), pltpu.VMEM((n_pad, kd), BF16), pltpu.SemaphoreType.DMA(())],
        compiler_params=_cparams("arbitrary"),
        name="ckv_pool_proj",
    )(h, w_t, g.reshape(1, KV_LORA), tab)


def _kvup_kernel(c_ref, kr_ref, w_ref, k_ref, v_ref):
    r = _dot(c_ref[...].astype(BF16), w_ref[...])
    nk = N_HEADS * QK_NOPE
    v_ref[...] = r[:, nk:].astype(v_ref.dtype)
    krb = kr_ref[...].astype(k_ref.dtype)
    for hh in range(N_HEADS):
        k_ref[:, hh * HEAD_PAD:hh * HEAD_PAD + LANES] = r[:, hh * QK_NOPE:(hh + 1) * QK_NOPE].astype(k_ref.dtype)
        k_ref[:, hh * HEAD_PAD + LANES:(hh + 1) * HEAD_PAD] = krb


def _kv_up(ckv, kr, w_ukv, tm=256):
    rows = ckv.shape[0]
    n = w_ukv.shape[1]
    return pl.pallas_call(
        _kvup_kernel,
        grid=(rows // tm,),
        in_specs=[
            pl.BlockSpec((tm, KV_LORA), lambda i: (i, 0)),
            pl.BlockSpec((tm, LANES), lambda i: (i, 0)),
            pl.BlockSpec((KV_LORA, n), lambda i: (0, 0)),
        ],
        out_specs=[pl.BlockSpec((tm, N_HEADS * HEAD_PAD), lambda i: (i, 0)), pl.BlockSpec((tm, N_HEADS * V_DIM), lambda i: (i, 0))],
        out_shape=[jax.ShapeDtypeStruct((rows, N_HEADS * HEAD_PAD), BF16), jax.ShapeDtypeStruct((rows, N_HEADS * V_DIM), BF16)],
        compiler_params=_cparams("arbitrary"),
        name="kv_up",
    )(ckv, kr, w_ukv)


def _pattn_kernel(q_ref, k_ref, v_ref, o_ref, *, tq):
    nq = q_ref.shape[0] // tq
    row = lax.broadcasted_iota(jnp.int32, (tq, tq), 0)
    col = lax.broadcasted_iota(jnp.int32, (tq, tq), 1)
    for qi in range(nq):
        q = q_ref[qi * tq:(qi + 1) * tq, :]
        state = None
        for ki in range(qi + 1):
            s = _dot_nt(q, k_ref[ki * tq:(ki + 1) * tq, :])
            if ki == qi:
                s = jnp.where(col <= row, s, -jnp.inf)
            state = _softmax_step(s, v_ref[ki * tq:(ki + 1) * tq, :], state)
        _, l, acc = state
        o_ref[qi * tq:(qi + 1) * tq, :] = (acc / l).astype(o_ref.dtype)


def _prompt_attn(q, k, v, batch, seq, tq=512):
    return pl.pallas_call(
        functools.partial(_pattn_kernel, tq=tq),
        grid=(batch, N_HEADS),
        in_specs=[
            pl.BlockSpec((seq, HEAD_PAD), lambda b, h: (b, h)),
            pl.BlockSpec((seq, HEAD_PAD), lambda b, h: (b, h)),
            pl.BlockSpec((seq, V_DIM), lambda b, h: (b, h)),
        ],
        out_specs=pl.BlockSpec((seq, V_DIM), lambda b, h: (b, h)),
        out_shape=jax.ShapeDtypeStruct((batch * seq, N_HEADS * V_DIM), BF16),
        compiler_params=_cparams("arbitrary", "arbitrary"),
        name="prompt_attn",
    )(q, k, v)


def _qlat_kernel(q_ref, w_ref, ql_ref, qr_ref):
    q = q_ref[...]
    ql_ref[...] = _dot(q[:, :QK_NOPE], w_ref[...])
    qr_ref[...] = q[:, LANES:].astype(F32)


def _q_latent(q, w_ukt, row0, rows):
    blk = row0 // rows
    return pl.pallas_call(
        _qlat_kernel,
        grid=(N_HEADS,),
        in_specs=[
            pl.BlockSpec((rows, HEAD_PAD), lambda h: (blk, h)),
            pl.BlockSpec((None, QK_NOPE, KV_LORA), lambda h: (h, 0, 0)),
        ],
        out_specs=[
            pl.BlockSpec((None, rows, KV_LORA), lambda h: (h, 0, 0)),
            pl.BlockSpec((None, rows, LANES), lambda h: (h, 0, 0)),
        ],
        out_shape=[jax.ShapeDtypeStruct((N_HEADS, rows, KV_LORA), F32), jax.ShapeDtypeStruct((N_HEADS, rows, LANES), F32)],
        compiler_params=_cparams("arbitrary"),
        name="q_latent",
    )(q, w_ukt)


KCAT = KV_LORA + LANES


def _sattn_kernel(pt_ref, ql_ref, qr_ref, cn_ref, krn_ref, cache_c, cache_krt, o_ref,
                  qcat, ncat, cbuf, krbuf, kcb, krtb, sem, *, cp, n_chunks, t_new):
    seq = pl.program_id(0)
    n_seq = pl.num_programs(0)
    rows = N_HEADS * t_new
    assert n_chunks % 2 == 0 and n_chunks >= 2

    def copies(sq, chunk, slot):
        out = []
        for k in range(cp):
            page = pt_ref[sq, chunk * cp + k]
            out.append(pltpu.make_async_copy(cache_c.at[page], cbuf.at[slot, k], sem.at[slot]))
            out.append(pltpu.make_async_copy(cache_krt.at[page], krbuf.at[slot, k], sem.at[slot]))
        return out

    def start(sq, chunk, slot):
        for n, c in enumerate(copies(sq, chunk, slot)):
            c.start(priority=(n // 2) % 2)

    def wait(sq, chunk, slot):
        for c in copies(sq, chunk, slot):
            c.wait()

    def scores(slot):
        for k in range(cp):
            kcb[slot, k * PAGE_SIZE:(k + 1) * PAGE_SIZE, :] = cbuf[slot, k].astype(BF16)
            krtb[slot, :, k * PAGE_SIZE:(k + 1) * PAGE_SIZE] = krbuf[slot, k].astype(BF16)
        return _dot_nt(qcat[:, :KV_LORA], kcb[slot]) + _dot(qcat[:, KV_LORA:KV_LORA + QK_ROPE], krtb[slot])

    @pl.when(seq == 0)
    def _():
        start(seq, 0, 0)

    start(seq, 1, 1)

    qcat[:, :KV_LORA] = ql_ref[...].reshape(rows, KV_LORA).astype(BF16)
    qcat[:, KV_LORA:] = qr_ref[...].reshape(rows, LANES).astype(BF16)
    ncat[...] = jnp.zeros(ncat.shape, BF16)
    ncat[0:t_new, :KV_LORA] = cn_ref[...].astype(BF16)
    ncat[0:t_new, KV_LORA:] = krn_ref[...].astype(BF16)
    s_new = _dot_nt(qcat[...], ncat[...])
    row = lax.broadcasted_iota(jnp.int32, s_new.shape, 0)
    col = lax.broadcasted_iota(jnp.int32, s_new.shape, 1)
    s_new = jnp.where(col <= row % t_new, s_new, -jnp.inf)
    state = _softmax_step(s_new, ncat[:, :KV_LORA], None)

    wait(seq, 0, 0)
    s_cur = scores(0)
    for c in range(n_chunks):
        slot = c % 2
        if c + 2 < n_chunks:
            start(seq, c + 2, slot)
        elif c + 2 == n_chunks:
            @pl.when(seq + 1 < n_seq)
            def _():
                start(seq + 1, 0, slot)
        if c + 1 < n_chunks:
            wait(seq, c + 1, 1 - slot)
            s_next = scores(1 - slot)
        state = _softmax_merge([state, _softmax_step(s_cur, kcb[slot], None)])
        if c + 1 < n_chunks:
            s_cur = s_next

    _, l, acc = state
    o_ref[...] = (acc / l).reshape(o_ref.shape)


def _sample_attn(qlat, qr, ckv_s, kr_s, cache_c, cache_krt, page_table, t_new, cp=8):
    n_seq, n_pages = page_table.shape
    rows = N_HEADS * t_new
    grid_spec = pltpu.PrefetchScalarGridSpec(
        num_scalar_prefetch=1,
        grid=(n_seq,),
        in_specs=[
            pl.BlockSpec((N_HEADS, t_new, KV_LORA), lambda s, pt: (0, s, 0)),
            pl.BlockSpec((N_HEADS, t_new, LANES), lambda s, pt: (0, s, 0)),
            pl.BlockSpec((t_new, KV_LORA), lambda s, pt: (s, 0)),
            pl.BlockSpec((t_new, LANES), lambda s, pt: (s, 0)),
            pl.BlockSpec(memory_space=pl.ANY),
            pl.BlockSpec(memory_space=pl.ANY),
        ],
        out_specs=pl.BlockSpec((N_HEADS, t_new, KV_LORA), lambda s, pt: (0, s, 0)),
        scratch_shapes=[
            pltpu.VMEM((rows, KCAT), BF16),
            pltpu.VMEM((PAGE_SIZE, KCAT), BF16),
            pltpu.VMEM((2, cp, PAGE_SIZE, KV_LORA), F32),
            pltpu.VMEM((2, cp, QK_ROPE, PAGE_SIZE), F32),
            pltpu.VMEM((2, cp * PAGE_SIZE, KV_LORA), BF16),
            pltpu.VMEM((2, QK_ROPE, cp * PAGE_SIZE), BF16),
            pltpu.SemaphoreType.DMA((2,)),
        ],
    )
    return pl.pallas_call(
        functools.partial(_sattn_kernel, cp=cp, n_chunks=n_pages // cp, t_new=t_new),
        grid_spec=grid_spec,
        out_shape=jax.ShapeDtypeStruct((N_HEADS, n_seq * t_new, KV_LORA), F32),
        compiler_params=_cparams("arbitrary"),
        name="sample_attn",
    )(page_table, qlat, qr, ckv_s, kr_s, cache_c, cache_krt)


def _ouv_kernel(o_ref, w_ref, out_ref):
    out_ref[...] = _dot(o_ref[...].astype(BF16), w_ref[...]).astype(out_ref.dtype)


def _o_uv(olat, w_uv2):
    _, rows, _ = olat.shape
    return pl.pallas_call(
        _ouv_kernel,
        grid=(N_HEADS,),
        in_specs=[
            pl.BlockSpec((None, rows, KV_LORA), lambda h: (h, 0, 0)),
            pl.BlockSpec((KV_LORA, V_DIM), lambda h: (0, h)),
        ],
        out_specs=pl.BlockSpec((rows, V_DIM), lambda h: (0, h)),
        out_shape=jax.ShapeDtypeStruct((rows, N_HEADS * V_DIM), BF16),
        compiler_params=_cparams("arbitrary"),
        name="o_uv",
    )(olat, w_uv2)


HALO = 2 * SUBLANES


def _pool_groups(load, u_of, cnt_of, wg_ref, ps_ref, store):
    group = wg_ref.shape[1]
    for g, w in enumerate(POOL_WINDOWS):
        sl = slice(g * group, (g + 1) * group)
        ws = load(0, sl)
        for k in range(1, w):
            ws = ws + load(k, sl)
        pooled = ws / cnt_of(w) - u_of(sl)
        pooled = pooled.reshape(-1, group).astype(BF16)
        store(sl, (_dot(pooled, wg_ref[g]) * ps_ref[:, sl]))


def _pool_prompt_kernel(u_ref, wg_ref, ps_ref, y_ref, ext_ref, *, tiles_per_seq):
    i = pl.program_id(0)
    tm = u_ref.shape[0]
    it = i % tiles_per_seq

    @pl.when(it == 0)
    def _():
        ext_ref[0:HALO, :] = jnp.zeros((HALO, ext_ref.shape[1]), F32)

    @pl.when(it != 0)
    def _():
        ext_ref[0:HALO, :] = ext_ref[tm:tm + HALO, :]

    ext_ref[HALO:HALO + tm, :] = u_ref[...]
    pos = it * tm + lax.broadcasted_iota(jnp.int32, (tm, 1), 0)

    def store(sl, val):
        y_ref[:, sl] = val.astype(y_ref.dtype)

    _pool_groups(
        lambda k, sl: ext_ref[HALO - k:HALO - k + tm, sl],
        lambda sl: u_ref[:, sl],
        lambda w: jnp.minimum(pos + 1, w).astype(F32),
        wg_ref, ps_ref, store)


def _pool_prompt(u, wg, ps, rows, seq):
    width = wg.shape[0] * wg.shape[1]
    tm = ROW_TILE
    return pl.pallas_call(
        functools.partial(_pool_prompt_kernel, tiles_per_seq=seq // tm),
        grid=(rows // tm,),
        in_specs=[
            pl.BlockSpec((tm, width), lambda i: (i, 0)),
            pl.BlockSpec(wg.shape, lambda i: (0, 0, 0)),
            pl.BlockSpec((1, width), lambda i: (0, 0)),
        ],
        out_specs=pl.BlockSpec((tm, width), lambda i: (i, 0)),
        out_shape=jax.ShapeDtypeStruct((rows, width), BF16),
        scratch_shapes=[pltpu.VMEM((HALO + tm, width), F32)],
        compiler_params=_cparams("arbitrary"),
        name="pool_prompt",
    )(u, wg, ps)


def _pool_sample_kernel(u_ref, st_ref, wg_ref, ps_ref, y_ref, ext_ref, *, pos0):
    t_new = u_ref.shape[1]
    ext_ref[:, 0:HALO, :] = st_ref[...]
    ext_ref[:, HALO:HALO + t_new, :] = u_ref[...]
    pos = pos0 + lax.broadcasted_iota(jnp.int32, (1, t_new, 1), 1)

    def store(sl, val):
        y_ref[:, sl] = val.astype(y_ref.dtype)

    _pool_groups(
        lambda k, sl: ext_ref[:, HALO - k:HALO - k + t_new, sl],
        lambda sl: u_ref[:, :, sl],
        lambda w: jnp.minimum(pos + 1, w).astype(F32),
        wg_ref, ps_ref, store)


def _pool_sample(u3, st, wg, ps, pos0, seq0, ns=64):
    n_seq = st.shape[0]
    _, t_new, width = u3.shape
    blk0 = seq0 // ns
    return pl.pallas_call(
        functools.partial(_pool_sample_kernel, pos0=pos0),
        grid=(n_seq // ns,),
        in_specs=[
            pl.BlockSpec((ns, t_new, width), lambda i: (blk0 + i, 0, 0)),
            pl.BlockSpec((ns, HALO, width), lambda i: (i, 0, 0)),
            pl.BlockSpec(wg.shape, lambda i: (0, 0, 0)),
            pl.BlockSpec((1, width), lambda i: (0, 0)),
        ],
        out_specs=pl.BlockSpec((ns * t_new, width), lambda i: (i, 0)),
        out_shape=jax.ShapeDtypeStruct((n_seq * t_new, width), BF16),
        scratch_shapes=[pltpu.VMEM((ns, HALO + t_new, width), F32)],
        compiler_params=_cparams("arbitrary"),
        name="pool_sample",
    )(u3, st, wg, ps)


def _merge_kernel(h_ref, oap_ref, oas_ref, ypp_ref, yps_ref, wt_hbm, ba_ref, bb_ref, wa_ref, wp_ref, o_ref,
                  wf_ref, wb_ref, sem, *, row0, n_blocks, n_prompt):
    j, i = pl.program_id(0), pl.program_id(1)
    tn = o_ref.shape[1]
    n = tn * n_blocks

    @pl.when(i == 0)
    def _():
        cps = [pltpu.make_async_copy(wt_hbm.at[pl.ds(row0 + br * n + j * tn, tn)], wf_ref.at[br], sem.at[br])
               for br in range(2)]
        for cp in cps:
            cp.start()
        for cp in cps:
            cp.wait()
        wb_ref[...] = wf_ref[...].astype(BF16)

    h = h_ref[...]
    ga = jax.nn.sigmoid(_dot_nt(h, wb_ref[0]) + ba_ref[...])
    gb = jax.nn.sigmoid(_dot_nt(h, wb_ref[1]) + bb_ref[...])

    def finish(oa_ref, yp_ref):
        a = _dot(oa_ref[...], wa_ref[...])
        p = _dot(yp_ref[...], wp_ref[...])
        o_ref[...] = (ga * a + gb * p).astype(o_ref.dtype)

    @pl.when(i < n_prompt)
    def _():
        finish(oap_ref, ypp_ref)

    @pl.when(i >= n_prompt)
    def _():
        finish(oas_ref, yps_ref)


def _merge(h, oa_p, oa_s, yp_p, yp_s, w_t, row0, b_gates, wa, wp, tn=512):
    m, d = h.shape
    ka, kp = wa.shape[0], wp.shape[0]
    n = wa.shape[1]
    nb = n // tn
    n_prompt = oa_p.shape[0] // ROW_TILE
    pr, sr = _prompt_rows(n_prompt), _sample_rows(n_prompt)
    bg = b_gates.reshape(1, -1)
    return pl.pallas_call(
        functools.partial(_merge_kernel, row0=row0, n_blocks=nb, n_prompt=n_prompt),
        grid=(nb, m // ROW_TILE),
        in_specs=[
            pl.BlockSpec((ROW_TILE, d), lambda j, i: (i, 0)),
            pl.BlockSpec((ROW_TILE, ka), lambda j, i: (pr(i), 0)),
            pl.BlockSpec((ROW_TILE, ka), lambda j, i: (sr(i), 0)),
            pl.BlockSpec((ROW_TILE, kp), lambda j, i: (pr(i), 0)),
            pl.BlockSpec((ROW_TILE, kp), lambda j, i: (sr(i), 0)),
            pl.BlockSpec(memory_space=pl.ANY),
            pl.BlockSpec((1, tn), lambda j, i: (0, j)),
            pl.BlockSpec((1, tn), lambda j, i: (0, nb + j)),
            pl.BlockSpec((ka, tn), lambda j, i: (0, j)),
            pl.BlockSpec((kp, tn), lambda j, i: (0, j)),
        ],
        out_specs=pl.BlockSpec((ROW_TILE, tn), lambda j, i: (i, j)),
        out_shape=jax.ShapeDtypeStruct((m, n), BF16),
        scratch_shapes=[pltpu.VMEM((2, tn, d), F32), pltpu.VMEM((2, tn, d), BF16), pltpu.SemaphoreType.DMA((2,))],
        compiler_params=_cparams("arbitrary", "arbitrary"),
        name="merge",
    )(h, oa_p, oa_s, yp_p, yp_s, w_t, bg, bg, wa, wp)


def _ffn_kernel(*refs, sample, tiles_per_seq, t_new):
    if sample:
        (x_ref, g_ref, wa_ref, wv_ref, cw_ref, cb_ref, wd_ref, gf_ref, p1_ref, p2_ref,
         y_ref, a_ref, hn_ref, acc_ref, ext_ref) = refs
    else:
        (x_ref, g_ref, wa_ref, wv_ref, cw_ref, cb_ref, wd_ref, gf_ref,
         y_ref, tail_ref, hn_ref, acc_ref, ext_ref, carry_ref) = refs
    i, f = pl.program_id(0), pl.program_id(1)
    tm = x_ref.shape[0]

    @pl.when(f == 0)
    def _():
        hn_ref[...] = _rms(x_ref[...], g_ref[...]).astype(hn_ref.dtype)
        acc_ref[...] = jnp.zeros(acc_ref.shape, F32)

    hn = hn_ref[...]
    a = _dot(hn, wa_ref[...])
    v = _dot(hn, wv_ref[...])
    ext_ref[SUBLANES:SUBLANES + tm, :] = a
    if sample:
        ext_ref[0:SUBLANES, :] = jnp.zeros((SUBLANES, ext_ref.shape[1]), F32)
        rmod = lax.broadcasted_iota(jnp.int32, (tm, 1), 0) % t_new
        s1 = jnp.where(rmod >= 1, ext_ref[SUBLANES - 1:SUBLANES - 1 + tm, :], p1_ref[...])
        s2 = jnp.where(rmod >= 2, ext_ref[SUBLANES - 2:SUBLANES - 2 + tm, :], p2_ref[...])
        a_ref[...] = a
    else:
        first = i % tiles_per_seq == 0

        @pl.when(first)
        def _():
            ext_ref[0:SUBLANES, :] = jnp.zeros((SUBLANES, ext_ref.shape[1]), F32)

        @pl.when(jnp.logical_not(first))
        def _():
            ext_ref[0:SUBLANES, :] = carry_ref[f]

        tail = a[tm - SUBLANES:, :]
        carry_ref[f] = tail
        tail_ref[...] = tail
        s1 = ext_ref[SUBLANES - 1:SUBLANES - 1 + tm, :]
        s2 = ext_ref[SUBLANES - 2:SUBLANES - 2 + tm, :]
    conv = cb_ref[...] + cw_ref[0:1, :] * s2
    conv = conv + cw_ref[1:2, :] * s1
    conv = conv + cw_ref[2:3, :] * a
    hid = (jax.nn.silu(conv) * v).astype(BF16)
    acc_ref[...] += _dot(hid, wd_ref[...])

    @pl.when(f == pl.num_programs(1) - 1)
    def _():
        x2 = x_ref[...] + acc_ref[...]
        y_ref[...] = _rms(x2, gf_ref[...])


def _ffn(x, row0, rows, g, w_up, conv_w, conv_b, w_down, gf, *, seq=None, prev=None, t_new=None, tf=512):
    d = x.shape[1]
    d_ff = w_down.shape[0]
    nf = d_ff // tf
    tm = ROW_TILE
    blk0 = row0 // tm
    sample = prev is not None
    in_specs = [
        pl.BlockSpec((tm, d), lambda i, f: (blk0 + i, 0)),
        pl.BlockSpec((1, d), lambda i, f: (0, 0)),
        pl.BlockSpec((d, tf), lambda i, f: (0, f)),
        pl.BlockSpec((d, tf), lambda i, f: (0, nf + f)),
        pl.BlockSpec((CONV_W, tf), lambda i, f: (0, f)),
        pl.BlockSpec((1, tf), lambda i, f: (0, f)),
        pl.BlockSpec((tf, d), lambda i, f: (f, 0)),
        pl.BlockSpec((1, d), lambda i, f: (0, 0)),
    ]
    ins = [x, g.reshape(1, d), w_up, w_up, conv_w, conv_b.reshape(1, d_ff), w_down, gf.reshape(1, d)]
    scratch = [pltpu.VMEM((tm, d), BF16), pltpu.VMEM((tm, d), F32), pltpu.VMEM((SUBLANES + tm, tf), F32)]
    y_spec = pl.BlockSpec((tm, d), lambda i, f: (i, 0))
    y_shape = jax.ShapeDtypeStruct((rows, d), F32)
    if sample:
        in_specs += [pl.BlockSpec((tm, tf), lambda i, f: (i, f))] * 2
        ins += list(prev)
        out_specs = [y_spec, pl.BlockSpec((tm, tf), lambda i, f: (i, f))]
        out_shape = [y_shape, jax.ShapeDtypeStruct((rows, d_ff), F32)]
        tiles_per_seq = None
    else:
        out_specs = [y_spec, pl.BlockSpec((None, SUBLANES, tf), lambda i, f: (i, 0, f))]
        out_shape = [y_shape, jax.ShapeDtypeStruct((rows // tm, SUBLANES, d_ff), F32)]
        scratch.append(pltpu.VMEM((nf, SUBLANES, tf), F32))
        tiles_per_seq = seq // tm
    return pl.pallas_call(
        functools.partial(_ffn_kernel, sample=sample, tiles_per_seq=tiles_per_seq, t_new=t_new),
        grid=(rows // tm, nf),
        in_specs=in_specs,
        out_specs=out_specs,
        out_shape=out_shape,
        scratch_shapes=scratch,
        compiler_params=_cparams("arbitrary", "arbitrary"),
        name="ffn_sample" if sample else "ffn_prompt",
    )(*ins)


def _rope_table(pos):
    half = QK_ROPE // 2
    inv = ROPE_BASE ** (-jnp.arange(half, dtype=F32) / half)
    ang = pos[:, None] * inv[None, :]
    c, s = jnp.cos(ang), jnp.sin(ang)
    n = pos.shape[0]
    one, zero = jnp.ones((n, LANES - QK_ROPE), F32), jnp.zeros((n, LANES - QK_ROPE), F32)
    return jnp.concatenate([c, c, one, -s, s, zero], axis=1)


def _layer(xp, xs, dims, cache_c, cache_krt, page_table, state_pool, state_conv,
           norm_attn_g, w_in, b_gates, kv_norm_g, w_uk, w_uv, w_attn_out, w_pool_group, pool_scale,
           w_pool_out, w_o, norm_ffn_g, w_up, conv_w, conv_b, w_down, norm_final_g):
    batch, seq, n_seq, t_new, n_past = dims
    mp, ms = xp.shape[0], xs.shape[0]
    d = xp.shape[1]
    n_prompt = mp // ROW_TILE
    tiles_per_seq = seq // ROW_TILE
    q_cols = N_HEADS * (QK_NOPE + QK_ROPE)
    pool_w = w_pool_out.shape[0]
    c0, c2 = q_cols, q_cols + KV_LORA + QK_ROPE
    c3 = c2 + pool_w

    pos = jnp.concatenate([jnp.arange(seq, dtype=F32), jnp.tile(n_past + jnp.arange(t_new, dtype=F32), ROW_TILE // t_new)])
    tab = _rope_table(pos)

    def tab_row(i):
        return jnp.where(i < n_prompt, i % tiles_per_seq, tiles_per_seq)

    w_t = jnp.swapaxes(w_in, 0, 1)
    w_ukv = jnp.concatenate([w_uk.reshape(KV_LORA, -1), w_uv.reshape(KV_LORA, -1)], axis=1).astype(BF16)
    w_ukt = jnp.transpose(w_uk, (1, 2, 0)).astype(BF16)
    w_uv2 = w_uv.reshape(KV_LORA, -1).astype(BF16)

    h = _norm(xp, xs, norm_attn_g)
    q = _q_proj(h, w_t, tab, tab_row)
    ckv_p, ckv_s, kr_p, kr_s, u = _ckv_pool_proj(h, w_t, c0, pool_w, kv_norm_g, tab, tab_row, mp, ms)

    kp, vp = _kv_up(ckv_p, kr_p, w_ukv)
    o_p = _prompt_attn(q, kp, vp, batch, seq)
    qlat, qr = _q_latent(q, w_ukt, mp, ms)
    olat = _sample_attn(qlat, qr, ckv_s, kr_s, cache_c, cache_krt, page_table, t_new)
    o_s = _o_uv(olat, w_uv2)

    wpg = w_pool_group.astype(BF16)
    ps = pool_scale.reshape(1, pool_w)
    y_p = _pool_prompt(u, wpg, ps, mp, seq)
    u3 = u.reshape((mp + ms) // t_new, t_new, pool_w)
    st_pad = jnp.pad(state_pool, ((0, 0), (HALO - POOL_BUF, 0), (0, 0)))
    y_s = _pool_sample(u3, st_pad, wpg, ps, n_past, mp // t_new)

    merged = _merge(h, o_p, o_s, y_p, y_s, w_t, c3, b_gates, w_attn_out.astype(BF16), w_pool_out.astype(BF16))
    x1 = _mm(merged, w_o.astype(BF16), res=(xp, xs), n_prompt=n_prompt, tn=d, name="o_proj")

    w_up_b = w_up.astype(BF16)
    w_down_b = w_down.astype(BF16)
    y_prompt, tails = _ffn(x1, 0, mp, norm_ffn_g, w_up_b, conv_w, conv_b, w_down_b, norm_final_g, seq=seq)
    d_ff = w_down.shape[0]
    prev1 = jnp.pad(state_conv[:, CONV_W - 2:CONV_W - 1], ((0, 0), (0, t_new - 1), (0, 0))).reshape(ms, d_ff)
    prev2 = jnp.pad(state_conv, ((0, 0), (0, t_new - (CONV_W - 1)), (0, 0))).reshape(ms, d_ff)
    y_sample, a_s = _ffn(x1, mp, ms, norm_ffn_g, w_up_b, conv_w, conv_b, w_down_b, norm_final_g,
                         prev=(prev1, prev2), t_new=t_new)

    conv_p = tails[tiles_per_seq - 1::tiles_per_seq, SUBLANES - (CONV_W - 1):]
    conv_s = a_s.reshape(n_seq, t_new, d_ff)[:, t_new - (CONV_W - 1):]
    pool_p = jnp.stack([u[(b + 1) * seq - POOL_BUF:(b + 1) * seq] for b in range(batch)])
    u_s = u[mp:].reshape(n_seq, t_new, pool_w)
    pool_s = jnp.concatenate([state_pool, u_s], axis=1)[:, -POOL_BUF:]
    return (y_prompt, y_sample, ckv_p.reshape(batch, seq, KV_LORA), kr_p[:, :QK_ROPE].reshape(batch, seq, QK_ROPE),
            ckv_s.reshape(n_seq, t_new, KV_LORA), kr_s[:, :QK_ROPE].reshape(n_seq, t_new, QK_ROPE),
            pool_p, pool_s, conv_p, conv_s)


def kernel(x_prompt, x_sample, cache_kv_latent, cache_k_rope, page_table, state_pool, state_conv, norm_attn_g, w_in, b_gates, kv_norm_g, w_uk, w_uv, w_attn_out, w_pool_group, pool_scale, w_pool_out, w_o, norm_ffn_g, w_up, conv_w, conv_b, w_down, norm_final_g):
    batch, seq, d = x_prompt.shape
    n_seq, t_new, _ = x_sample.shape
    depth = w_in.shape[0]
    assert depth == 1, "the stacked-row pipeline below is written for a single layer"
    n_past = page_table.shape[1] * PAGE_SIZE
    dims = (batch, seq, n_seq, t_new, n_past)
    xp = x_prompt.reshape(batch * seq, d)
    xs = x_sample.reshape(n_seq * t_new, d)
    l = 0
    cache_c = cache_kv_latent.reshape(cache_kv_latent.shape[1:])
    cache_krt = jnp.swapaxes(cache_k_rope.reshape(cache_k_rope.shape[1:]), 1, 2)
    outs = _layer(xp, xs, dims, cache_c, cache_krt, page_table, state_pool[l], state_conv[l],
                  norm_attn_g[l], w_in[l], b_gates[l], kv_norm_g[l], w_uk[l], w_uv[l], w_attn_out[l],
                  w_pool_group[l], pool_scale[l], w_pool_out[l], w_o[l], norm_ffn_g[l], w_up[l], conv_w[l],
                  conv_b[l], w_down[l], norm_final_g)
    y_p, y_s = outs[0].reshape(batch, seq, d), outs[1].reshape(n_seq, t_new, d)
    return (y_p, y_s) + tuple(o[None] for o in outs[2:])
```

```python
import functools
import math

import jax
import jax.numpy as jnp
from jax import lax
from jax.experimental import pallas as pl
from jax.experimental.pallas import tpu as pltpu

F32 = jnp.float32
BF16 = jnp.bfloat16

N_HEADS = 16
QK_NOPE = 128
QK_ROPE = 64
V_DIM = 128
KV_LORA = 512
ROPE_BASE = 10000.0
ATTN_SCALE = (QK_NOPE + QK_ROPE) ** -0.5
EXP2_SCALE = ATTN_SCALE * math.log2(math.e)
PAGE_SIZE = 128
POOL_WINDOWS = (2, 4, 8, 16)
POOL_BUF = max(POOL_WINDOWS) - 1
CONV_W = 3
EPS = 1e-6

LANES = 128
SUBLANES = 8
HEAD_PAD = 2 * LANES
VMEM_LIMIT = 48 * 1024 * 1024

ROW_TILE = 512
FFN_SPLIT = 2


def _cparams(*sem):
    return pltpu.CompilerParams(dimension_semantics=sem, vmem_limit_bytes=VMEM_LIMIT)


def _dot(a, b):
    return jnp.dot(a, b, preferred_element_type=F32)


def _dot_nt(a, b):
    return lax.dot_general(a, b, (((1,), (1,)), ((), ())), preferred_element_type=F32)


def _rms(x, g):
    ms = jnp.mean(x * x, axis=-1, keepdims=True)
    return x * lax.rsqrt(ms + EPS) * g


def _rope128(x, ct, st):
    lane = lax.broadcasted_iota(jnp.int32, x.shape, 1)
    other = jnp.where(lane < QK_ROPE // 2, pltpu.roll(x, LANES - QK_ROPE // 2, 1), pltpu.roll(x, QK_ROPE // 2, 1))
    return x * ct + other * st


def _softmax_step(s, vals, state):
    bm = jnp.max(s, axis=-1, keepdims=True)
    if state is None:
        m_new = bm
        p = jnp.exp2((s - m_new) * EXP2_SCALE)
        l_new = jnp.sum(p, axis=-1, keepdims=True)
        acc_new = _dot(p.astype(BF16), vals)
    else:
        m, l, acc = state
        m_new = jnp.maximum(m, bm)
        alpha = jnp.exp2((m - m_new) * EXP2_SCALE)
        p = jnp.exp2((s - m_new) * EXP2_SCALE)
        l_new = alpha * l + jnp.sum(p, axis=-1, keepdims=True)
        acc_new = alpha * acc + _dot(p.astype(BF16), vals)
    return m_new, l_new, acc_new


def _softmax_merge(states):
    m_all = states[0][0]
    for m, _, _ in states[1:]:
        m_all = jnp.maximum(m_all, m)
    l_all = acc_all = None
    for m, l, acc in states:
        w = jnp.exp2((m - m_all) * EXP2_SCALE)
        l_all = w * l if l_all is None else l_all + w * l
        acc_all = w * acc if acc_all is None else acc_all + w * acc
    return m_all, l_all, acc_all


def _prompt_rows(n_prompt):
    return lambda i: jnp.minimum(i, n_prompt - 1)


def _sample_rows(n_prompt):
    return lambda i: jnp.maximum(i - n_prompt, 0)


def _norm_kernel(xp_ref, xs_ref, g_ref, o_ref, *, n_prompt):
    i = pl.program_id(0)

    @pl.when(i < n_prompt)
    def _():
        o_ref[...] = _rms(xp_ref[...], g_ref[...]).astype(o_ref.dtype)

    @pl.when(i >= n_prompt)
    def _():
        o_ref[...] = _rms(xs_ref[...], g_ref[...]).astype(o_ref.dtype)


def _norm(xp, xs, g):
    d = xp.shape[1]
    n_prompt, n_sample = xp.shape[0] // ROW_TILE, xs.shape[0] // ROW_TILE
    pr, sr = _prompt_rows(n_prompt), _sample_rows(n_prompt)
    return pl.pallas_call(
        functools.partial(_norm_kernel, n_prompt=n_prompt),
        grid=(n_prompt + n_sample,),
        in_specs=[
            pl.BlockSpec((ROW_TILE, d), lambda i: (pr(i), 0)),
            pl.BlockSpec((ROW_TILE, d), lambda i: (sr(i), 0)),
            pl.BlockSpec((1, d), lambda i: (0, 0)),
        ],
        out_specs=pl.BlockSpec((ROW_TILE, d), lambda i: (i, 0)),
        out_shape=jax.ShapeDtypeStruct((xp.shape[0] + xs.shape[0], d), BF16),
        compiler_params=_cparams("arbitrary"),
        name="rmsnorm",
    )(xp, xs, g.reshape(1, d))


def _mm_kernel(*refs, n_prompt, has_res):
    h_ref, w_ref = refs[0], refs[1]
    o_ref = refs[-1]
    r = _dot(h_ref[...], w_ref[...])
    if has_res:
        i = pl.program_id(1)

        @pl.when(i < n_prompt)
        def _():
            o_ref[...] = refs[2][...] + r

        @pl.when(i >= n_prompt)
        def _():
            o_ref[...] = refs[3][...] + r
    else:
        o_ref[...] = r.astype(o_ref.dtype)


def _mm(h, w, *, res=None, n_prompt=None, out_dtype=F32, tn=512, name="mm"):
    m, kd = h.shape
    n = w.shape[1]
    ins = [h, w]
    specs = [pl.BlockSpec((ROW_TILE, kd), lambda j, i: (i, 0)), pl.BlockSpec((kd, tn), lambda j, i: (0, j))]
    if res is not None:
        pr, sr = _prompt_rows(n_prompt), _sample_rows(n_prompt)
        ins += list(res)
        specs += [pl.BlockSpec((ROW_TILE, tn), lambda j, i: (pr(i), j)), pl.BlockSpec((ROW_TILE, tn), lambda j, i: (sr(i), j))]
    return pl.pallas_call(
        functools.partial(_mm_kernel, n_prompt=n_prompt, has_res=res is not None),
        grid=(n // tn, m // ROW_TILE),
        in_specs=specs,
        out_specs=pl.BlockSpec((ROW_TILE, tn), lambda j, i: (i, j)),
        out_shape=jax.ShapeDtypeStruct((m, n), out_dtype),
        compiler_params=_cparams("arbitrary", "arbitrary"),
        name=name,
    )(*ins)


def _q_kernel(h_ref, wt_ref, tab_ref, o_ref, wb_ref):
    hd = QK_NOPE + QK_ROPE

    @pl.when(pl.program_id(1) == 0)
    def _():
        for hh in range(wb_ref.shape[0] // HEAD_PAD):
            wb_ref[hh * HEAD_PAD:hh * HEAD_PAD + hd, :] = wt_ref[hh * hd:(hh + 1) * hd, :].astype(BF16)
            wb_ref[hh * HEAD_PAD + hd:(hh + 1) * HEAD_PAD, :] = jnp.zeros((HEAD_PAD - hd, wb_ref.shape[1]), BF16)

    r = _dot_nt(h_ref[...], wb_ref[...])
    ct, st = tab_ref[:, :LANES], tab_ref[:, LANES:]
    for hh in range(r.shape[1] // HEAD_PAD):
        c0 = hh * HEAD_PAD
        o_ref[:, c0:c0 + LANES] = r[:, c0:c0 + LANES].astype(o_ref.dtype)
        o_ref[:, c0 + LANES:c0 + HEAD_PAD] = _rope128(r[:, c0 + LANES:c0 + HEAD_PAD], ct, st).astype(o_ref.dtype)


def _q_proj(h, w_t, tab, tab_row, heads_per_block=4):
    m, kd = h.shape
    hd = QK_NOPE + QK_ROPE
    tn = heads_per_block * HEAD_PAD
    return pl.pallas_call(
        _q_kernel,
        grid=(N_HEADS // heads_per_block, m // ROW_TILE),
        in_specs=[
            pl.BlockSpec((ROW_TILE, kd), lambda j, i: (i, 0)),
            pl.BlockSpec((heads_per_block * hd, kd), lambda j, i: (j, 0)),
            pl.BlockSpec((ROW_TILE, 2 * LANES), lambda j, i: (tab_row(i), 0)),
        ],
        out_specs=pl.BlockSpec((ROW_TILE, tn), lambda j, i: (i, j)),
        out_shape=jax.ShapeDtypeStruct((m, N_HEADS * HEAD_PAD), BF16),
        scratch_shapes=[pltpu.VMEM((tn, kd), BF16)],
        compiler_params=_cparams("arbitrary", "arbitrary"),
        name="q_proj",
    )(h, w_t, tab)


def _ckv_kernel(h_ref, wt_hbm, g_ref, tab_ref, cp_ref, cs_ref, krp_ref, krs_ref, u_ref, wf_ref, wb_ref, sem,
                *, row0, n_prompt):
    i = pl.program_id(0)
    n_ck = KV_LORA + QK_ROPE
    n_ck_pad = KV_LORA + LANES

    @pl.when(i == 0)
    def _():
        cp = pltpu.make_async_copy(wt_hbm.at[pl.ds(row0, wf_ref.shape[0])], wf_ref, sem)
        cp.start()
        cp.wait()
        wb_ref[0:n_ck, :] = wf_ref[0:n_ck, :].astype(BF16)
        wb_ref[n_ck:n_ck_pad, :] = jnp.zeros((n_ck_pad - n_ck, wb_ref.shape[1]), BF16)
        wb_ref[n_ck_pad:, :] = wf_ref[n_ck:, :].astype(BF16)

    r = _dot_nt(h_ref[...], wb_ref[...])
    c = _rms(r[:, :KV_LORA], g_ref[...])
    kr = _rope128(r[:, KV_LORA:n_ck_pad], tab_ref[:, :LANES], tab_ref[:, LANES:])
    u_ref[...] = r[:, n_ck_pad:]

    @pl.when(i < n_prompt)
    def _():
        cp_ref[...] = c
        krp_ref[...] = kr

    @pl.when(i >= n_prompt)
    def _():
        cs_ref[...] = c
        krs_ref[...] = kr


def _ckv_pool_proj(h, w_t, row0, pool_w, g, tab, tab_row, mp, ms):
    m, kd = h.shape
    n_rows = KV_LORA + QK_ROPE + pool_w
    n_pad = KV_LORA + LANES + pool_w
    n_prompt = mp // ROW_TILE
    pr, sr = _prompt_rows(n_prompt), _sample_rows(n_prompt)
    return pl.pallas_call(
        functools.partial(_ckv_kernel, row0=row0, n_prompt=n_prompt),
        grid=(m // ROW_TILE,),
        in_specs=[
            pl.BlockSpec((ROW_TILE, kd), lambda i: (i, 0)),
            pl.BlockSpec(memory_space=pl.ANY),
            pl.BlockSpec((1, KV_LORA), lambda i: (0, 0)),
            pl.BlockSpec((ROW_TILE, 2 * LANES), lambda i: (tab_row(i), 0)),
        ],
        out_specs=[
            pl.BlockSpec((ROW_TILE, KV_LORA), lambda i: (pr(i), 0)),
            pl.BlockSpec((ROW_TILE, KV_LORA), lambda i: (sr(i), 0)),
            pl.BlockSpec((ROW_TILE, LANES), lambda i: (pr(i), 0)),
            pl.BlockSpec((ROW_TILE, LANES), lambda i: (sr(i), 0)),
            pl.BlockSpec((ROW_TILE, pool_w), lambda i: (i, 0)),
        ],
        out_shape=[
            jax.ShapeDtypeStruct((mp, KV_LORA), F32), jax.ShapeDtypeStruct((ms, KV_LORA), F32),
            jax.ShapeDtypeStruct((mp, LANES), F32), jax.ShapeDtypeStruct((ms, LANES), F32),
            jax.ShapeDtypeStruct((m, pool_w), F32),
        ],
        scratch_shapes=[pltpu.VMEM((n_rows, kd), F32), pltpu.VMEM((n_pad, kd), BF16), pltpu.SemaphoreType.DMA(())],
        compiler_params=_cparams("arbitrary"),
        name="ckv_pool_proj",
    )(h, w_t, g.reshape(1, KV_LORA), tab)


def _kvup_kernel(c_ref, kr_ref, w_ref, k_ref, v_ref):
    r = _dot(c_ref[...].astype(BF16), w_ref[...])
    nk = N_HEADS * QK_NOPE
    v_ref[...] = r[:, nk:].astype(v_ref.dtype)
    krb = kr_ref[...].astype(k_ref.dtype)
    for hh in range(N_HEADS):
        k_ref[:, hh * HEAD_PAD:hh * HEAD_PAD + LANES] = r[:, hh * QK_NOPE:(hh + 1) * QK_NOPE].astype(k_ref.dtype)
        k_ref[:, hh * HEAD_PAD + LANES:(hh + 1) * HEAD_PAD] = krb


def _kv_up(ckv, kr, w_ukv, tm=256):
    rows = ckv.shape[0]
    n = w_ukv.shape[1]
    return pl.pallas_call(
        _kvup_kernel,
        grid=(rows // tm,),
        in_specs=[
            pl.BlockSpec((tm, KV_LORA), lambda i: (i, 0)),
            pl.BlockSpec((tm, LANES), lambda i: (i, 0)),
            pl.BlockSpec((KV_LORA, n), lambda i: (0, 0)),
        ],
        out_specs=[pl.BlockSpec((tm, N_HEADS * HEAD_PAD), lambda i: (i, 0)), pl.BlockSpec((tm, N_HEADS * V_DIM), lambda i: (i, 0))],
        out_shape=[jax.ShapeDtypeStruct((rows, N_HEADS * HEAD_PAD), BF16), jax.ShapeDtypeStruct((rows, N_HEADS * V_DIM), BF16)],
        compiler_params=_cparams("arbitrary"),
        name="kv_up",
    )(ckv, kr, w_ukv)


def _pattn_kernel(q_ref, k_ref, v_ref, o_ref, *, tq):
    nq = q_ref.shape[0] // tq
    row = lax.broadcasted_iota(jnp.int32, (tq, tq), 0)
    col = lax.broadcasted_iota(jnp.int32, (tq, tq), 1)
    for qi in range(nq):
        q = q_ref[qi * tq:(qi + 1) * tq, :]
        state = None
        for ki in range(qi + 1):
            s = _dot_nt(q, k_ref[ki * tq:(ki + 1) * tq, :])
            if ki == qi:
                s = jnp.where(col <= row, s, -jnp.inf)
            state = _softmax_step(s, v_ref[ki * tq:(ki + 1) * tq, :], state)
        _, l, acc = state
        o_ref[qi * tq:(qi + 1) * tq, :] = (acc / l).astype(o_ref.dtype)


def _prompt_attn(q, k, v, batch, seq, tq=512):
    return pl.pallas_call(
        functools.partial(_pattn_kernel, tq=tq),
        grid=(batch, N_HEADS),
        in_specs=[
            pl.BlockSpec((seq, HEAD_PAD), lambda b, h: (b, h)),
            pl.BlockSpec((seq, HEAD_PAD), lambda b, h: (b, h)),
            pl.BlockSpec((seq, V_DIM), lambda b, h: (b, h)),
        ],
        out_specs=pl.BlockSpec((seq, V_DIM), lambda b, h: (b, h)),
        out_shape=jax.ShapeDtypeStruct((batch * seq, N_HEADS * V_DIM), BF16),
        compiler_params=_cparams("arbitrary", "arbitrary"),
        name="prompt_attn",
    )(q, k, v)


def _qlat_kernel(q_ref, w_ref, ql_ref, qr_ref):
    q = q_ref[...]
    ql_ref[...] = _dot(q[:, :QK_NOPE], w_ref[...])
    qr_ref[...] = q[:, LANES:].astype(F32)


def _q_latent(q, w_ukt, row0, rows):
    blk = row0 // rows
    return pl.pallas_call(
        _qlat_kernel,
        grid=(N_HEADS,),
        in_specs=[
            pl.BlockSpec((rows, HEAD_PAD), lambda h: (blk, h)),
            pl.BlockSpec((None, QK_NOPE, KV_LORA), lambda h: (h, 0, 0)),
        ],
        out_specs=[
            pl.BlockSpec((None, rows, KV_LORA), lambda h: (h, 0, 0)),
            pl.BlockSpec((None, rows, LANES), lambda h: (h, 0, 0)),
        ],
        out_shape=[jax.ShapeDtypeStruct((N_HEADS, rows, KV_LORA), F32), jax.ShapeDtypeStruct((N_HEADS, rows, LANES), F32)],
        compiler_params=_cparams("arbitrary"),
        name="q_latent",
    )(q, w_ukt)


KCAT = KV_LORA + LANES


def _sattn_kernel(pt_ref, ql_ref, qr_ref, cn_ref, krn_ref, cache_c, cache_krt, o_ref,
                  qcat, ncat, cbuf, krbuf, kcb, krtb, sem, *, cp, n_chunks, t_new):
    seq = pl.program_id(0)
    n_seq = pl.num_programs(0)
    rows = N_HEADS * t_new
    n_slots = cbuf.shape[0]
    assert n_chunks % n_slots == 0

    def copies(sq, chunk):
        slot = chunk % n_slots
        out = []
        for k in range(cp):
            page = pt_ref[sq, chunk * cp + k]
            out.append(pltpu.make_async_copy(cache_c.at[page], cbuf.at[slot, k], sem.at[slot]))
            out.append(pltpu.make_async_copy(cache_krt.at[page], krbuf.at[slot, k], sem.at[slot]))
        return out

    def start(sq, chunk):
        for n, c in enumerate(copies(sq, chunk)):
            c.start(priority=(n // 2) % 2)

    def fetch(chunk):
        slot, half = chunk % n_slots, chunk % 2
        for c in copies(seq, chunk):
            c.wait()
        for k in range(cp):
            kcb[half, k * PAGE_SIZE:(k + 1) * PAGE_SIZE, :] = cbuf[slot, k].astype(BF16)
            krtb[half, :, k * PAGE_SIZE:(k + 1) * PAGE_SIZE] = krbuf[slot, k].astype(BF16)
        ahead = chunk + n_slots
        if ahead < n_chunks:
            start(seq, ahead)
        else:
            @pl.when(seq + 1 < n_seq)
            def _():
                start(seq + 1, ahead - n_chunks)
        return _dot_nt(qcat[:, :KV_LORA], kcb[half]) + _dot(qcat[:, KV_LORA:KV_LORA + QK_ROPE], krtb[half])

    @pl.when(seq == 0)
    def _():
        for chunk in range(n_slots):
            start(seq, chunk)

    qcat[:, :KV_LORA] = ql_ref[...].reshape(rows, KV_LORA).astype(BF16)
    qcat[:, KV_LORA:] = qr_ref[...].reshape(rows, LANES).astype(BF16)
    ncat[...] = jnp.zeros(ncat.shape, BF16)
    ncat[0:t_new, :KV_LORA] = cn_ref[...].astype(BF16)
    ncat[0:t_new, KV_LORA:] = krn_ref[...].astype(BF16)
    s_new = _dot_nt(qcat[...], ncat[...])
    row = lax.broadcasted_iota(jnp.int32, s_new.shape, 0)
    col = lax.broadcasted_iota(jnp.int32, s_new.shape, 1)
    s_new = jnp.where(col <= row % t_new, s_new, -jnp.inf)
    state = _softmax_step(s_new, ncat[:, :KV_LORA], None)

    s_cur = fetch(0)
    for c in range(n_chunks):
        if c + 1 < n_chunks:
            s_next = fetch(c + 1)
        state = _softmax_merge([state, _softmax_step(s_cur, kcb[c % 2], None)])
        if c + 1 < n_chunks:
            s_cur = s_next

    _, l, acc = state
    o_ref[...] = (acc / l).reshape(o_ref.shape)


def _sample_attn(qlat, qr, ckv_s, kr_s, cache_c, cache_krt, page_table, t_new, cp=8, n_slots=4):
    n_seq, n_pages = page_table.shape
    rows = N_HEADS * t_new
    grid_spec = pltpu.PrefetchScalarGridSpec(
        num_scalar_prefetch=1,
        grid=(n_seq,),
        in_specs=[
            pl.BlockSpec((N_HEADS, t_new, KV_LORA), lambda s, pt: (0, s, 0)),
            pl.BlockSpec((N_HEADS, t_new, LANES), lambda s, pt: (0, s, 0)),
            pl.BlockSpec((t_new, KV_LORA), lambda s, pt: (s, 0)),
            pl.BlockSpec((t_new, LANES), lambda s, pt: (s, 0)),
            pl.BlockSpec(memory_space=pl.ANY),
            pl.BlockSpec(memory_space=pl.ANY),
        ],
        out_specs=pl.BlockSpec((N_HEADS, t_new, KV_LORA), lambda s, pt: (0, s, 0)),
        scratch_shapes=[
            pltpu.VMEM((rows, KCAT), BF16),
            pltpu.VMEM((PAGE_SIZE, KCAT), BF16),
            pltpu.VMEM((n_slots, cp, PAGE_SIZE, KV_LORA), F32),
            pltpu.VMEM((n_slots, cp, QK_ROPE, PAGE_SIZE), F32),
            pltpu.VMEM((2, cp * PAGE_SIZE, KV_LORA), BF16),
            pltpu.VMEM((2, QK_ROPE, cp * PAGE_SIZE), BF16),
            pltpu.SemaphoreType.DMA((n_slots,)),
        ],
    )
    return pl.pallas_call(
        functools.partial(_sattn_kernel, cp=cp, n_chunks=n_pages // cp, t_new=t_new),
        grid_spec=grid_spec,
        out_shape=jax.ShapeDtypeStruct((N_HEADS, n_seq * t_new, KV_LORA), F32),
        compiler_params=_cparams("arbitrary"),
        name="sample_attn",
    )(page_table, qlat, qr, ckv_s, kr_s, cache_c, cache_krt)


def _ouv_kernel(o_ref, w_ref, out_ref):
    out_ref[...] = _dot(o_ref[...].astype(BF16), w_ref[...]).astype(out_ref.dtype)


def _o_uv(olat, w_uv2):
    _, rows, _ = olat.shape
    return pl.pallas_call(
        _ouv_kernel,
        grid=(N_HEADS,),
        in_specs=[
            pl.BlockSpec((None, rows, KV_LORA), lambda h: (h, 0, 0)),
            pl.BlockSpec((KV_LORA, V_DIM), lambda h: (0, h)),
        ],
        out_specs=pl.BlockSpec((rows, V_DIM), lambda h: (0, h)),
        out_shape=jax.ShapeDtypeStruct((rows, N_HEADS * V_DIM), BF16),
        compiler_params=_cparams("arbitrary"),
        name="o_uv",
    )(olat, w_uv2)


HALO = 2 * SUBLANES


def _pool_groups(load, u_of, cnt_of, wg_ref, ps_ref, store):
    group = wg_ref.shape[1]
    for g, w in enumerate(POOL_WINDOWS):
        sl = slice(g * group, (g + 1) * group)
        ws = load(0, sl)
        for k in range(1, w):
            ws = ws + load(k, sl)
        pooled = ws / cnt_of(w) - u_of(sl)
        pooled = pooled.reshape(-1, group).astype(BF16)
        store(sl, (_dot(pooled, wg_ref[g]) * ps_ref[:, sl]))


def _pool_prompt_kernel(u_ref, wg_ref, ps_ref, y_ref, ext_ref, *, tiles_per_seq):
    i = pl.program_id(0)
    tm = u_ref.shape[0]
    it = i % tiles_per_seq

    @pl.when(it == 0)
    def _():
        ext_ref[0:HALO, :] = jnp.zeros((HALO, ext_ref.shape[1]), F32)

    @pl.when(it != 0)
    def _():
        ext_ref[0:HALO, :] = ext_ref[tm:tm + HALO, :]

    ext_ref[HALO:HALO + tm, :] = u_ref[...]
    pos = it * tm + lax.broadcasted_iota(jnp.int32, (tm, 1), 0)

    def store(sl, val):
        y_ref[:, sl] = val.astype(y_ref.dtype)

    _pool_groups(
        lambda k, sl: ext_ref[HALO - k:HALO - k + tm, sl],
        lambda sl: u_ref[:, sl],
        lambda w: jnp.minimum(pos + 1, w).astype(F32),
        wg_ref, ps_ref, store)


def _pool_prompt(u, wg, ps, rows, seq):
    width = wg.shape[0] * wg.shape[1]
    tm = ROW_TILE
    return pl.pallas_call(
        functools.partial(_pool_prompt_kernel, tiles_per_seq=seq // tm),
        grid=(rows // tm,),
        in_specs=[
            pl.BlockSpec((tm, width), lambda i: (i, 0)),
            pl.BlockSpec(wg.shape, lambda i: (0, 0, 0)),
            pl.BlockSpec((1, width), lambda i: (0, 0)),
        ],
        out_specs=pl.BlockSpec((tm, width), lambda i: (i, 0)),
        out_shape=jax.ShapeDtypeStruct((rows, width), BF16),
        scratch_shapes=[pltpu.VMEM((HALO + tm, width), F32)],
        compiler_params=_cparams("arbitrary"),
        name="pool_prompt",
    )(u, wg, ps)


def _pool_sample_kernel(u_ref, st_ref, wg_ref, ps_ref, y_ref, ext_ref, *, pos0):
    t_new = u_ref.shape[1]
    ext_ref[:, 0:HALO, :] = st_ref[...]
    ext_ref[:, HALO:HALO + t_new, :] = u_ref[...]
    pos = pos0 + lax.broadcasted_iota(jnp.int32, (1, t_new, 1), 1)

    def store(sl, val):
        y_ref[:, sl] = val.astype(y_ref.dtype)

    _pool_groups(
        lambda k, sl: ext_ref[:, HALO - k:HALO - k + t_new, sl],
        lambda sl: u_ref[:, :, sl],
        lambda w: jnp.minimum(pos + 1, w).astype(F32),
        wg_ref, ps_ref, store)


def _pool_sample(u3, st, wg, ps, pos0, seq0, ns=64):
    n_seq = st.shape[0]
    _, t_new, width = u3.shape
    blk0 = seq0 // ns
    return pl.pallas_call(
        functools.partial(_pool_sample_kernel, pos0=pos0),
        grid=(n_seq // ns,),
        in_specs=[
            pl.BlockSpec((ns, t_new, width), lambda i: (blk0 + i, 0, 0)),
            pl.BlockSpec((ns, HALO, width), lambda i: (i, 0, 0)),
            pl.BlockSpec(wg.shape, lambda i: (0, 0, 0)),
            pl.BlockSpec((1, width), lambda i: (0, 0)),
        ],
        out_specs=pl.BlockSpec((ns * t_new, width), lambda i: (i, 0)),
        out_shape=jax.ShapeDtypeStruct((n_seq * t_new, width), BF16),
        scratch_shapes=[pltpu.VMEM((ns, HALO + t_new, width), F32)],
        compiler_params=_cparams("arbitrary"),
        name="pool_sample",
    )(u3, st, wg, ps)


def _merge_kernel(h_ref, oap_ref, oas_ref, ypp_ref, yps_ref, wt_hbm, ba_ref, bb_ref, wa_ref, wp_ref, o_ref,
                  wf_ref, wb_ref, sem, *, row0, n_blocks, n_prompt):
    j, i = pl.program_id(0), pl.program_id(1)
    tn = o_ref.shape[1]
    n = tn * n_blocks

    @pl.when(i == 0)
    def _():
        cps = [pltpu.make_async_copy(wt_hbm.at[pl.ds(row0 + br * n + j * tn, tn)], wf_ref.at[br], sem.at[br])
               for br in range(2)]
        for cp in cps:
            cp.start()
        for cp in cps:
            cp.wait()
        wb_ref[...] = wf_ref[...].astype(BF16)

    h = h_ref[...]
    ga = jax.nn.sigmoid(_dot_nt(h, wb_ref[0]) + ba_ref[...])
    gb = jax.nn.sigmoid(_dot_nt(h, wb_ref[1]) + bb_ref[...])

    def finish(oa_ref, yp_ref):
        a = _dot(oa_ref[...], wa_ref[...])
        p = _dot(yp_ref[...], wp_ref[...])
        o_ref[...] = (ga * a + gb * p).astype(o_ref.dtype)

    @pl.when(i < n_prompt)
    def _():
        finish(oap_ref, ypp_ref)

    @pl.when(i >= n_prompt)
    def _():
        finish(oas_ref, yps_ref)


def _merge(h, oa_p, oa_s, yp_p, yp_s, w_t, row0, b_gates, wa, wp, tn=512):
    m, d = h.shape
    ka, kp = wa.shape[0], wp.shape[0]
    n = wa.shape[1]
    nb = n // tn
    n_prompt = oa_p.shape[0] // ROW_TILE
    pr, sr = _prompt_rows(n_prompt), _sample_rows(n_prompt)
    bg = b_gates.reshape(1, -1)
    return pl.pallas_call(
        functools.partial(_merge_kernel, row0=row0, n_blocks=nb, n_prompt=n_prompt),
        grid=(nb, m // ROW_TILE),
        in_specs=[
            pl.BlockSpec((ROW_TILE, d), lambda j, i: (i, 0)),
            pl.BlockSpec((ROW_TILE, ka), lambda j, i: (pr(i), 0)),
            pl.BlockSpec((ROW_TILE, ka), lambda j, i: (sr(i), 0)),
            pl.BlockSpec((ROW_TILE, kp), lambda j, i: (pr(i), 0)),
            pl.BlockSpec((ROW_TILE, kp), lambda j, i: (sr(i), 0)),
            pl.BlockSpec(memory_space=pl.ANY),
            pl.BlockSpec((1, tn), lambda j, i: (0, j)),
            pl.BlockSpec((1, tn), lambda j, i: (0, nb + j)),
            pl.BlockSpec((ka, tn), lambda j, i: (0, j)),
            pl.BlockSpec((kp, tn), lambda j, i: (0, j)),
        ],
        out_specs=pl.BlockSpec((ROW_TILE, tn), lambda j, i: (i, j)),
        out_shape=jax.ShapeDtypeStruct((m, n), BF16),
        scratch_shapes=[pltpu.VMEM((2, tn, d), F32), pltpu.VMEM((2, tn, d), BF16), pltpu.SemaphoreType.DMA((2,))],
        compiler_params=_cparams("arbitrary", "arbitrary"),
        name="merge",
    )(h, oa_p, oa_s, yp_p, yp_s, w_t, bg, bg, wa, wp)


def _ffn_kernel(*refs, sample, tiles_per_seq, t_new):
    if sample:
        (x_ref, g_ref, wa_ref, wv_ref, cw_ref, cb_ref, wd_ref, gf_ref, p1_ref, p2_ref,
         y_ref, a_ref, hn_ref, acc_ref) = refs
    else:
        (x_ref, g_ref, wa_ref, wv_ref, cw_ref, cb_ref, wd_ref, gf_ref,
         y_ref, tail_ref, hn_ref, acc_ref, carry_ref) = refs
    i, f = pl.program_id(0), pl.program_id(1)
    tm = x_ref.shape[0]

    @pl.when(f == 0)
    def _():
        hn_ref[...] = _rms(x_ref[...], g_ref[...]).astype(hn_ref.dtype)
        acc_ref[...] = jnp.zeros(acc_ref.shape, F32)

    if not sample:
        @pl.when((i == 0) & (f == 0))
        def _():
            carry_ref[...] = jnp.zeros(carry_ref.shape, F32)

    hn = hn_ref[...]
    tf = wa_ref.shape[1]
    sw = tf // FFN_SPLIT
    cols = [slice(k * sw, (k + 1) * sw) for k in range(FFN_SPLIT)]
    ups = [(_dot(hn, wa_ref[:, cs]), _dot(hn, wv_ref[:, cs])) for cs in cols]
    hids = []
    for cs, (a, v) in zip(cols, ups):
        r1, r2 = pltpu.roll(a, 1, 0), pltpu.roll(a, 2, 0)
        if sample:
            rmod = lax.broadcasted_iota(jnp.int32, (tm, 1), 0) % t_new
            s1 = jnp.where(rmod >= 1, r1, p1_ref[:, cs])
            s2 = jnp.where(rmod >= 2, r2, p2_ref[:, cs])
            a_ref[:, cs] = a
        else:
            first = i % tiles_per_seq == 0
            prev = jnp.where(first, 0.0, carry_ref[f, :, cs])
            row8 = lax.broadcasted_iota(jnp.int32, (SUBLANES, 1), 0)
            h1 = jnp.where(row8 < 1, pltpu.roll(prev, 1, 0), r1[:SUBLANES])
            h2 = jnp.where(row8 < 2, pltpu.roll(prev, 2, 0), r2[:SUBLANES])
            s1 = jnp.concatenate([h1, r1[SUBLANES:]], axis=0)
            s2 = jnp.concatenate([h2, r2[SUBLANES:]], axis=0)
            tail = a[tm - SUBLANES:, :]
            carry_ref[f, :, cs] = tail
            tail_ref[:, cs] = tail
        conv = cb_ref[:, cs] + cw_ref[0:1, cs] * s2
        conv = conv + cw_ref[1:2, cs] * s1
        conv = conv + cw_ref[2:3, cs] * a
        half = 0.5 * conv
        hids.append(((half * v) * (jnp.tanh(half) + 1.0)).astype(BF16))
    acc = acc_ref[...]
    for cs, hid in zip(cols, hids):
        acc = acc + _dot(hid, wd_ref[cs, :])
    acc_ref[...] = acc

    @pl.when(f == pl.num_programs(1) - 1)
    def _():
        x2 = x_ref[...] + acc_ref[...]
        y_ref[...] = _rms(x2, gf_ref[...])


def _ffn(x, row0, rows, g, w_up, conv_w, conv_b, w_down, gf, *, seq=None, prev=None, t_new=None, tf=512):
    d = x.shape[1]
    d_ff = w_down.shape[0]
    nf = d_ff // tf
    tm = ROW_TILE
    blk0 = row0 // tm
    sample = prev is not None
    in_specs = [
        pl.BlockSpec((tm, d), lambda i, f: (blk0 + i, 0)),
        pl.BlockSpec((1, d), lambda i, f: (0, 0)),
        pl.BlockSpec((d, tf), lambda i, f: (0, f)),
        pl.BlockSpec((d, tf), lambda i, f: (0, nf + f)),
        pl.BlockSpec((CONV_W, tf), lambda i, f: (0, f)),
        pl.BlockSpec((1, tf), lambda i, f: (0, f)),
        pl.BlockSpec((tf, d), lambda i, f: (f, 0)),
        pl.BlockSpec((1, d), lambda i, f: (0, 0)),
    ]
    ins = [x, g.reshape(1, d), w_up, w_up, conv_w, conv_b.reshape(1, d_ff), w_down, gf.reshape(1, d)]
    scratch = [pltpu.VMEM((tm, d), BF16), pltpu.VMEM((tm, d), F32)]
    y_spec = pl.BlockSpec((tm, d), lambda i, f: (i, 0))
    y_shape = jax.ShapeDtypeStruct((rows, d), F32)
    if sample:
        in_specs += [pl.BlockSpec((tm, tf), lambda i, f: (i, f))] * 2
        ins += list(prev)
        out_specs = [y_spec, pl.BlockSpec((tm, tf), lambda i, f: (i, f))]
        out_shape = [y_shape, jax.ShapeDtypeStruct((rows, d_ff), F32)]
        tiles_per_seq = None
    else:
        out_specs = [y_spec, pl.BlockSpec((None, SUBLANES, tf), lambda i, f: (i, 0, f))]
        out_shape = [y_shape, jax.ShapeDtypeStruct((rows // tm, SUBLANES, d_ff), F32)]
        scratch.append(pltpu.VMEM((nf, SUBLANES, tf), F32))
        tiles_per_seq = seq // tm
    return pl.pallas_call(
        functools.partial(_ffn_kernel, sample=sample, tiles_per_seq=tiles_per_seq, t_new=t_new),
        grid=(rows // tm, nf),
        in_specs=in_specs,
        out_specs=out_specs,
        out_shape=out_shape,
        scratch_shapes=scratch,
        compiler_params=_cparams("arbitrary", "arbitrary"),
        name="ffn_sample" if sample else "ffn_prompt",
    )(*ins)


def _rope_table(pos):
    half = QK_ROPE // 2
    inv = ROPE_BASE ** (-jnp.arange(half, dtype=F32) / half)
    ang = pos[:, None] * inv[None, :]
    c, s = jnp.cos(ang), jnp.sin(ang)
    n = pos.shape[0]
    one, zero = jnp.ones((n, LANES - QK_ROPE), F32), jnp.zeros((n, LANES - QK_ROPE), F32)
    return jnp.concatenate([c, c, one, -s, s, zero], axis=1)


def _layer(xp, xs, dims, cache_c, cache_krt, page_table, state_pool, state_conv,
           norm_attn_g, w_in, b_gates, kv_norm_g, w_uk, w_uv, w_attn_out, w_pool_group, pool_scale,
           w_pool_out, w_o, norm_ffn_g, w_up, conv_w, conv_b, w_down, norm_final_g):
    batch, seq, n_seq, t_new, n_past = dims
    mp, ms = xp.shape[0], xs.shape[0]
    d = xp.shape[1]
    n_prompt = mp // ROW_TILE
    tiles_per_seq = seq // ROW_TILE
    q_cols = N_HEADS * (QK_NOPE + QK_ROPE)
    pool_w = w_pool_out.shape[0]
    c0, c2 = q_cols, q_cols + KV_LORA + QK_ROPE
    c3 = c2 + pool_w

    pos = jnp.concatenate([jnp.arange(seq, dtype=F32), jnp.tile(n_past + jnp.arange(t_new, dtype=F32), ROW_TILE // t_new)])
    tab = _rope_table(pos)

    def tab_row(i):
        return jnp.where(i < n_prompt, i % tiles_per_seq, tiles_per_seq)

    w_t = jnp.swapaxes(w_in, 0, 1)
    w_ukv = jnp.concatenate([w_uk.reshape(KV_LORA, -1), w_uv.reshape(KV_LORA, -1)], axis=1).astype(BF16)
    w_ukt = jnp.transpose(w_uk, (1, 2, 0)).astype(BF16)
    w_uv2 = w_uv.reshape(KV_LORA, -1).astype(BF16)

    h = _norm(xp, xs, norm_attn_g)
    q = _q_proj(h, w_t, tab, tab_row)
    ckv_p, ckv_s, kr_p, kr_s, u = _ckv_pool_proj(h, w_t, c0, pool_w, kv_norm_g, tab, tab_row, mp, ms)

    kp, vp = _kv_up(ckv_p, kr_p, w_ukv)
    o_p = _prompt_attn(q, kp, vp, batch, seq)
    qlat, qr = _q_latent(q, w_ukt, mp, ms)
    olat = _sample_attn(qlat, qr, ckv_s, kr_s, cache_c, cache_krt, page_table, t_new)
    o_s = _o_uv(olat, w_uv2)

    wpg = w_pool_group.astype(BF16)
    ps = pool_scale.reshape(1, pool_w)
    y_p = _pool_prompt(u, wpg, ps, mp, seq)
    u3 = u.reshape((mp + ms) // t_new, t_new, pool_w)
    st_pad = jnp.pad(state_pool, ((0, 0), (HALO - POOL_BUF, 0), (0, 0)))
    y_s = _pool_sample(u3, st_pad, wpg, ps, n_past, mp // t_new)

    merged = _merge(h, o_p, o_s, y_p, y_s, w_t, c3, b_gates, w_attn_out.astype(BF16), w_pool_out.astype(BF16))
    x1 = _mm(merged, w_o.astype(BF16), res=(xp, xs), n_prompt=n_prompt, tn=d, name="o_proj")

    w_up_b = w_up.astype(BF16)
    w_down_b = w_down.astype(BF16)
    y_prompt, tails = _ffn(x1, 0, mp, norm_ffn_g, w_up_b, conv_w, conv_b, w_down_b, norm_final_g, seq=seq)
    d_ff = w_down.shape[0]
    prev1 = jnp.pad(state_conv[:, CONV_W - 2:CONV_W - 1], ((0, 0), (0, t_new - 1), (0, 0))).reshape(ms, d_ff)
    prev2 = jnp.pad(state_conv, ((0, 0), (0, t_new - (CONV_W - 1)), (0, 0))).reshape(ms, d_ff)
    y_sample, a_s = _ffn(x1, mp, ms, norm_ffn_g, w_up_b, conv_w, conv_b, w_down_b, norm_final_g,
                         prev=(prev1, prev2), t_new=t_new)

    conv_p = tails[tiles_per_seq - 1::tiles_per_seq, SUBLANES - (CONV_W - 1):]
    conv_s = a_s.reshape(n_seq, t_new, d_ff)[:, t_new - (CONV_W - 1):]
    pool_p = jnp.stack([u[(b + 1) * seq - POOL_BUF:(b + 1) * seq] for b in range(batch)])
    u_s = u[mp:].reshape(n_seq, t_new, pool_w)
    pool_s = jnp.concatenate([state_pool, u_s], axis=1)[:, -POOL_BUF:]
    return (y_prompt, y_sample, ckv_p.reshape(batch, seq, KV_LORA), kr_p[:, :QK_ROPE].reshape(batch, seq, QK_ROPE),
            ckv_s.reshape(n_seq, t_new, KV_LORA), kr_s[:, :QK_ROPE].reshape(n_seq, t_new, QK_ROPE),
            pool_p, pool_s, conv_p, conv_s)


def kernel(x_prompt, x_sample, cache_kv_latent, cache_k_rope, page_table, state_pool, state_conv, norm_attn_g, w_in, b_gates, kv_norm_g, w_uk, w_uv, w_attn_out, w_pool_group, pool_scale, w_pool_out, w_o, norm_ffn_g, w_up, conv_w, conv_b, w_down, norm_final_g):
    batch, seq, d = x_prompt.shape
    n_seq, t_new, _ = x_sample.shape
    depth = w_in.shape[0]
    assert depth == 1, "the stacked-row pipeline below is written for a single layer"
    n_past = page_table.shape[1] * PAGE_SIZE
    dims = (batch, seq, n_seq, t_new, n_past)
    xp = x_prompt.reshape(batch * seq, d)
    xs = x_sample.reshape(n_seq * t_new, d)
    l = 0
    cache_c = cache_kv_latent.reshape(cache_kv_latent.shape[1:])
    cache_krt = jnp.swapaxes(cache_k_rope.reshape(cache_k_rope.shape[1:]), 1, 2)
    outs = _layer(xp, xs, dims, cache_c, cache_krt, page_table, state_pool[l], state_conv[l],
                  norm_attn_g[l], w_in[l], b_gates[l], kv_norm_g[l], w_uk[l], w_uv[l], w_attn_out[l],
                  w_pool_group[l], pool_scale[l], w_pool_out[l], w_o[l], norm_ffn_g[l], w_up[l], conv_w[l],
                  conv_b[l], w_down[l], norm_final_g)
    y_p, y_s = outs[0].reshape(batch, seq, d), outs[1].reshape(n_seq, t_new, d)
    return (y_p, y_s) + tuple(o[None] for o in outs[2:])
```

```python
import functools
import math

import jax
import jax.numpy as jnp
from jax import lax
from jax.experimental import pallas as pl
from jax.experimental.pallas import tpu as pltpu

F32 = jnp.float32
BF16 = jnp.bfloat16

N_HEADS = 16
QK_NOPE = 128
QK_ROPE = 64
V_DIM = 128
KV_LORA = 512
ROPE_BASE = 10000.0
ATTN_SCALE = (QK_NOPE + QK_ROPE) ** -0.5
EXP2_SCALE = ATTN_SCALE * math.log2(math.e)
PAGE_SIZE = 128
POOL_WINDOWS = (2, 4, 8, 16)
POOL_BUF = max(POOL_WINDOWS) - 1
CONV_W = 3
EPS = 1e-6

LANES = 128
SUBLANES = 8
HEAD_PAD = 2 * LANES
VMEM_LIMIT = 48 * 1024 * 1024

ROW_TILE = 512
FFN_SPLIT = 2


def _cparams(*sem):
    return pltpu.CompilerParams(dimension_semantics=sem, vmem_limit_bytes=VMEM_LIMIT)


def _dot(a, b):
    return jnp.dot(a, b, preferred_element_type=F32)


def _dot_nt(a, b):
    return lax.dot_general(a, b, (((1,), (1,)), ((), ())), preferred_element_type=F32)


def _rms(x, g):
    ms = jnp.mean(x * x, axis=-1, keepdims=True)
    return x * lax.rsqrt(ms + EPS) * g


def _rope128(x, ct, st):
    lane = lax.broadcasted_iota(jnp.int32, x.shape, 1)
    half = QK_ROPE // 2
    other = jnp.where(lane % QK_ROPE < half, pltpu.roll(x, LANES - half, 1), pltpu.roll(x, half, 1))
    return x * ct + other * st


def _softmax_step(s, vals, state):
    bm = jnp.max(s, axis=-1, keepdims=True)
    if state is None:
        m_new = bm
        p = jnp.exp2((s - m_new) * EXP2_SCALE)
        l_new = jnp.sum(p, axis=-1, keepdims=True)
        acc_new = _dot(p.astype(BF16), vals)
    else:
        m, l, acc = state
        m_new = jnp.maximum(m, bm)
        alpha = jnp.exp2((m - m_new) * EXP2_SCALE)
        p = jnp.exp2((s - m_new) * EXP2_SCALE)
        l_new = alpha * l + jnp.sum(p, axis=-1, keepdims=True)
        acc_new = alpha * acc + _dot(p.astype(BF16), vals)
    return m_new, l_new, acc_new


def _softmax_merge(states):
    m_all = states[0][0]
    for m, _, _ in states[1:]:
        m_all = jnp.maximum(m_all, m)
    l_all = acc_all = None
    for m, l, acc in states:
        w = jnp.exp2((m - m_all) * EXP2_SCALE)
        l_all = w * l if l_all is None else l_all + w * l
        acc_all = w * acc if acc_all is None else acc_all + w * acc
    return m_all, l_all, acc_all


def _prompt_rows(n_prompt):
    return lambda i: jnp.minimum(i, n_prompt - 1)


def _sample_rows(n_prompt):
    return lambda i: jnp.maximum(i - n_prompt, 0)


def _norm_kernel(xp_ref, xs_ref, g_ref, o_ref, *, n_prompt):
    i = pl.program_id(0)

    @pl.when(i < n_prompt)
    def _():
        o_ref[...] = _rms(xp_ref[...], g_ref[...]).astype(o_ref.dtype)

    @pl.when(i >= n_prompt)
    def _():
        o_ref[...] = _rms(xs_ref[...], g_ref[...]).astype(o_ref.dtype)


def _norm(xp, xs, g):
    d = xp.shape[1]
    n_prompt, n_sample = xp.shape[0] // ROW_TILE, xs.shape[0] // ROW_TILE
    pr, sr = _prompt_rows(n_prompt), _sample_rows(n_prompt)
    return pl.pallas_call(
        functools.partial(_norm_kernel, n_prompt=n_prompt),
        grid=(n_prompt + n_sample,),
        in_specs=[
            pl.BlockSpec((ROW_TILE, d), lambda i: (pr(i), 0)),
            pl.BlockSpec((ROW_TILE, d), lambda i: (sr(i), 0)),
            pl.BlockSpec((1, d), lambda i: (0, 0)),
        ],
        out_specs=pl.BlockSpec((ROW_TILE, d), lambda i: (i, 0)),
        out_shape=jax.ShapeDtypeStruct((xp.shape[0] + xs.shape[0], d), BF16),
        compiler_params=_cparams("arbitrary"),
        name="rmsnorm",
    )(xp, xs, g.reshape(1, d))


def _mm_kernel(*refs, n_prompt, has_res):
    h_ref, w_ref = refs[0], refs[1]
    o_ref = refs[-1]
    r = _dot(h_ref[...], w_ref[...])
    if has_res:
        i = pl.program_id(1)

        @pl.when(i < n_prompt)
        def _():
            o_ref[...] = refs[2][...] + r

        @pl.when(i >= n_prompt)
        def _():
            o_ref[...] = refs[3][...] + r
    else:
        o_ref[...] = r.astype(o_ref.dtype)


def _mm(h, w, *, res=None, n_prompt=None, out_dtype=F32, tn=512, name="mm"):
    m, kd = h.shape
    n = w.shape[1]
    ins = [h, w]
    specs = [pl.BlockSpec((ROW_TILE, kd), lambda j, i: (i, 0)), pl.BlockSpec((kd, tn), lambda j, i: (0, j))]
    if res is not None:
        pr, sr = _prompt_rows(n_prompt), _sample_rows(n_prompt)
        ins += list(res)
        specs += [pl.BlockSpec((ROW_TILE, tn), lambda j, i: (pr(i), j)), pl.BlockSpec((ROW_TILE, tn), lambda j, i: (sr(i), j))]
    return pl.pallas_call(
        functools.partial(_mm_kernel, n_prompt=n_prompt, has_res=res is not None),
        grid=(n // tn, m // ROW_TILE),
        in_specs=specs,
        out_specs=pl.BlockSpec((ROW_TILE, tn), lambda j, i: (i, j)),
        out_shape=jax.ShapeDtypeStruct((m, n), out_dtype),
        compiler_params=_cparams("arbitrary", "arbitrary"),
        name=name,
    )(*ins)


def _q_kernel(h_ref, wt_ref, tab_ref, o_ref, wb_ref):
    hd = QK_NOPE + QK_ROPE
    pair = 2 * hd
    n_pairs = wb_ref.shape[0] // pair

    @pl.when(pl.program_id(1) == 0)
    def _():
        for p in range(n_pairs):
            a0, b0 = 2 * p * hd, (2 * p + 1) * hd
            w0 = p * pair
            wb_ref[w0:w0 + QK_NOPE, :] = wt_ref[a0:a0 + QK_NOPE, :].astype(BF16)
            wb_ref[w0 + QK_NOPE:w0 + 2 * QK_NOPE, :] = wt_ref[b0:b0 + QK_NOPE, :].astype(BF16)
            wb_ref[w0 + 2 * QK_NOPE:w0 + 2 * QK_NOPE + QK_ROPE, :] = wt_ref[a0 + QK_NOPE:a0 + hd, :].astype(BF16)
            wb_ref[w0 + 2 * QK_NOPE + QK_ROPE:w0 + pair, :] = wt_ref[b0 + QK_NOPE:b0 + hd, :].astype(BF16)

    r = _dot_nt(h_ref[...], wb_ref[...])
    ct, st = tab_ref[:, :LANES], tab_ref[:, LANES:]
    lane = lax.broadcasted_iota(jnp.int32, (r.shape[0], LANES), 1)
    for p in range(n_pairs):
        r0 = p * pair
        oa, ob = 2 * p * HEAD_PAD, (2 * p + 1) * HEAD_PAD
        o_ref[:, oa:oa + LANES] = r[:, r0:r0 + LANES].astype(o_ref.dtype)
        o_ref[:, ob:ob + LANES] = r[:, r0 + LANES:r0 + 2 * LANES].astype(o_ref.dtype)
        rp = _rope128(r[:, r0 + 2 * LANES:r0 + 3 * LANES], ct, st)
        o_ref[:, oa + LANES:oa + HEAD_PAD] = jnp.where(lane < QK_ROPE, rp, 0.0).astype(o_ref.dtype)
        o_ref[:, ob + LANES:ob + HEAD_PAD] = jnp.where(lane < QK_ROPE, pltpu.roll(rp, QK_ROPE, 1), 0.0).astype(o_ref.dtype)


def _q_proj(h, w_t, tab, tab_row, heads_per_block=4):
    m, kd = h.shape
    hd = QK_NOPE + QK_ROPE
    tn = heads_per_block * HEAD_PAD
    return pl.pallas_call(
        _q_kernel,
        grid=(N_HEADS // heads_per_block, m // ROW_TILE),
        in_specs=[
            pl.BlockSpec((ROW_TILE, kd), lambda j, i: (i, 0)),
            pl.BlockSpec((heads_per_block * hd, kd), lambda j, i: (j, 0)),
            pl.BlockSpec((ROW_TILE, 2 * LANES), lambda j, i: (tab_row(i), 0)),
        ],
        out_specs=pl.BlockSpec((ROW_TILE, tn), lambda j, i: (i, j)),
        out_shape=jax.ShapeDtypeStruct((m, N_HEADS * HEAD_PAD), BF16),
        scratch_shapes=[pltpu.VMEM((heads_per_block * hd, kd), BF16)],
        compiler_params=_cparams("arbitrary", "arbitrary"),
        name="q_proj",
    )(h, w_t, tab)


def _ckv_kernel(h_ref, wt_hbm, g_ref, tab_ref, cp_ref, cs_ref, krp_ref, krs_ref, u_ref, wf_ref, wb_ref, sem,
                *, row0, n_prompt):
    i = pl.program_id(0)
    n_ck = KV_LORA + QK_ROPE
    n_ck_pad = KV_LORA + LANES

    @pl.when(i == 0)
    def _():
        cp = pltpu.make_async_copy(wt_hbm.at[pl.ds(row0, wf_ref.shape[0])], wf_ref, sem)
        cp.start()
        cp.wait()
        wb_ref[0:n_ck, :] = wf_ref[0:n_ck, :].astype(BF16)
        wb_ref[n_ck:n_ck_pad, :] = jnp.zeros((n_ck_pad - n_ck, wb_ref.shape[1]), BF16)
        wb_ref[n_ck_pad:, :] = wf_ref[n_ck:, :].astype(BF16)

    r = _dot_nt(h_ref[...], wb_ref[...])
    c = _rms(r[:, :KV_LORA], g_ref[...])
    kr = _rope128(r[:, KV_LORA:n_ck_pad], tab_ref[:, :LANES], tab_ref[:, LANES:])
    u_ref[...] = r[:, n_ck_pad:]

    @pl.when(i < n_prompt)
    def _():
        cp_ref[...] = c
        krp_ref[...] = kr

    @pl.when(i >= n_prompt)
    def _():
        cs_ref[...] = c
        krs_ref[...] = kr


def _ckv_pool_proj(h, w_t, row0, pool_w, g, tab, tab_row, mp, ms):
    m, kd = h.shape
    n_rows = KV_LORA + QK_ROPE + pool_w
    n_pad = KV_LORA + LANES + pool_w
    n_prompt = mp // ROW_TILE
    pr, sr = _prompt_rows(n_prompt), _sample_rows(n_prompt)
    return pl.pallas_call(
        functools.partial(_ckv_kernel, row0=row0, n_prompt=n_prompt),
        grid=(m // ROW_TILE,),
        in_specs=[
            pl.BlockSpec((ROW_TILE, kd), lambda i: (i, 0)),
            pl.BlockSpec(memory_space=pl.ANY),
            pl.BlockSpec((1, KV_LORA), lambda i: (0, 0)),
            pl.BlockSpec((ROW_TILE, 2 * LANES), lambda i: (tab_row(i), 0)),
        ],
        out_specs=[
            pl.BlockSpec((ROW_TILE, KV_LORA), lambda i: (pr(i), 0)),
            pl.BlockSpec((ROW_TILE, KV_LORA), lambda i: (sr(i), 0)),
            pl.BlockSpec((ROW_TILE, LANES), lambda i: (pr(i), 0)),
            pl.BlockSpec((ROW_TILE, LANES), lambda i: (sr(i), 0)),
            pl.BlockSpec((ROW_TILE, pool_w), lambda i: (i, 0)),
        ],
        out_shape=[
            jax.ShapeDtypeStruct((mp, KV_LORA), F32), jax.ShapeDtypeStruct((ms, KV_LORA), F32),
            jax.ShapeDtypeStruct((mp, LANES), F32), jax.ShapeDtypeStruct((ms, LANES), F32),
            jax.ShapeDtypeStruct((m, pool_w), F32),
        ],
        scratch_shapes=[pltpu.VMEM((n_rows, kd), F32), pltpu.VMEM((n_pad, kd), BF16), pltpu.SemaphoreType.DMA(())],
        compiler_params=_cparams("arbitrary"),
        name="ckv_pool_proj",
    )(h, w_t, g.reshape(1, KV_LORA), tab)


def _kvup_kernel(c_ref, kr_ref, w_ref, k_ref, v_ref):
    r = _dot(c_ref[...].astype(BF16), w_ref[...])
    nk = N_HEADS * QK_NOPE
    v_ref[...] = r[:, nk:].astype(v_ref.dtype)
    krb = kr_ref[...].astype(k_ref.dtype)
    for hh in range(N_HEADS):
        k_ref[:, hh * HEAD_PAD:hh * HEAD_PAD + LANES] = r[:, hh * QK_NOPE:(hh + 1) * QK_NOPE].astype(k_ref.dtype)
        k_ref[:, hh * HEAD_PAD + LANES:(hh + 1) * HEAD_PAD] = krb


def _kv_up(ckv, kr, w_ukv, tm=256):
    rows = ckv.shape[0]
    n = w_ukv.shape[1]
    return pl.pallas_call(
        _kvup_kernel,
        grid=(rows // tm,),
        in_specs=[
            pl.BlockSpec((tm, KV_LORA), lambda i: (i, 0)),
            pl.BlockSpec((tm, LANES), lambda i: (i, 0)),
            pl.BlockSpec((KV_LORA, n), lambda i: (0, 0)),
        ],
        out_specs=[pl.BlockSpec((tm, N_HEADS * HEAD_PAD), lambda i: (i, 0)), pl.BlockSpec((tm, N_HEADS * V_DIM), lambda i: (i, 0))],
        out_shape=[jax.ShapeDtypeStruct((rows, N_HEADS * HEAD_PAD), BF16), jax.ShapeDtypeStruct((rows, N_HEADS * V_DIM), BF16)],
        compiler_params=_cparams("arbitrary"),
        name="kv_up",
    )(ckv, kr, w_ukv)


def _pattn_kernel(q_ref, k_ref, v_ref, o_ref, *, tq):
    nq = q_ref.shape[0] // tq
    row = lax.broadcasted_iota(jnp.int32, (tq, tq), 0)
    col = lax.broadcasted_iota(jnp.int32, (tq, tq), 1)
    for qi in range(nq):
        q = q_ref[qi * tq:(qi + 1) * tq, :]
        state = None
        for ki in range(qi + 1):
            s = _dot_nt(q, k_ref[ki * tq:(ki + 1) * tq, :])
            if ki == qi:
                s = jnp.where(col <= row, s, -jnp.inf)
            state = _softmax_step(s, v_ref[ki * tq:(ki + 1) * tq, :], state)
        _, l, acc = state
        o_ref[qi * tq:(qi + 1) * tq, :] = (acc / l).astype(o_ref.dtype)


def _prompt_attn(q, k, v, batch, seq, tq=512):
    return pl.pallas_call(
        functools.partial(_pattn_kernel, tq=tq),
        grid=(batch, N_HEADS),
        in_specs=[
            pl.BlockSpec((seq, HEAD_PAD), lambda b, h: (b, h)),
            pl.BlockSpec((seq, HEAD_PAD), lambda b, h: (b, h)),
            pl.BlockSpec((seq, V_DIM), lambda b, h: (b, h)),
        ],
        out_specs=pl.BlockSpec((seq, V_DIM), lambda b, h: (b, h)),
        out_shape=jax.ShapeDtypeStruct((batch * seq, N_HEADS * V_DIM), BF16),
        compiler_params=_cparams("arbitrary", "arbitrary"),
        name="prompt_attn",
    )(q, k, v)


def _qlat_kernel(q_ref, w_ref, ql_ref, qr_ref):
    q = q_ref[...]
    ql_ref[...] = _dot(q[:, :QK_NOPE], w_ref[...])
    qr_ref[...] = q[:, LANES:].astype(F32)


def _q_latent(q, w_ukt, row0, rows):
    blk = row0 // rows
    return pl.pallas_call(
        _qlat_kernel,
        grid=(N_HEADS,),
        in_specs=[
            pl.BlockSpec((rows, HEAD_PAD), lambda h: (blk, h)),
            pl.BlockSpec((None, QK_NOPE, KV_LORA), lambda h: (h, 0, 0)),
        ],
        out_specs=[
            pl.BlockSpec((None, rows, KV_LORA), lambda h: (h, 0, 0)),
            pl.BlockSpec((None, rows, LANES), lambda h: (h, 0, 0)),
        ],
        out_shape=[jax.ShapeDtypeStruct((N_HEADS, rows, KV_LORA), F32), jax.ShapeDtypeStruct((N_HEADS, rows, LANES), F32)],
        compiler_params=_cparams("arbitrary"),
        name="q_latent",
    )(q, w_ukt)


KCAT = KV_LORA + LANES


def _sattn_kernel(pt_ref, ql_ref, qr_ref, cn_ref, krn_ref, cache_c, cache_krt, o_ref,
                  qcat, ncat, cbuf, krbuf, kcb, krtb, sem, *, cp, n_chunks, t_new):
    seq = pl.program_id(0)
    n_seq = pl.num_programs(0)
    rows = N_HEADS * t_new
    n_slots = cbuf.shape[0]
    assert n_chunks % n_slots == 0

    def copies(sq, chunk):
        slot = chunk % n_slots
        out = []
        for k in range(cp):
            page = pt_ref[sq, chunk * cp + k]
            out.append(pltpu.make_async_copy(cache_c.at[page], cbuf.at[slot, k], sem.at[slot]))
            out.append(pltpu.make_async_copy(cache_krt.at[page], krbuf.at[slot, k], sem.at[slot]))
        return out

    def start(sq, chunk):
        for n, c in enumerate(copies(sq, chunk)):
            c.start(priority=(n // 2) % 2)

    def fetch(chunk):
        slot, half = chunk % n_slots, chunk % 2
        for c in copies(seq, chunk):
            c.wait()
        for k in range(cp):
            kcb[half, k * PAGE_SIZE:(k + 1) * PAGE_SIZE, :] = cbuf[slot, k].astype(BF16)
            krtb[half, :, k * PAGE_SIZE:(k + 1) * PAGE_SIZE] = krbuf[slot, k].astype(BF16)
        ahead = chunk + n_slots
        if ahead < n_chunks:
            start(seq, ahead)
        else:
            @pl.when(seq + 1 < n_seq)
            def _():
                start(seq + 1, ahead - n_chunks)
        return _dot_nt(qcat[:, :KV_LORA], kcb[half]) + _dot(qcat[:, KV_LORA:KV_LORA + QK_ROPE], krtb[half])

    @pl.when(seq == 0)
    def _():
        for chunk in range(n_slots):
            start(seq, chunk)

    qcat[:, :KV_LORA] = ql_ref[...].reshape(rows, KV_LORA).astype(BF16)
    qcat[:, KV_LORA:] = qr_ref[...].reshape(rows, LANES).astype(BF16)
    ncat[...] = jnp.zeros(ncat.shape, BF16)
    ncat[0:t_new, :KV_LORA] = cn_ref[...].astype(BF16)
    ncat[0:t_new, KV_LORA:] = krn_ref[...].astype(BF16)
    s_new = _dot_nt(qcat[...], ncat[...])
    row = lax.broadcasted_iota(jnp.int32, s_new.shape, 0)
    col = lax.broadcasted_iota(jnp.int32, s_new.shape, 1)
    s_new = jnp.where(col <= row % t_new, s_new, -jnp.inf)
    state = _softmax_step(s_new, ncat[:, :KV_LORA], None)

    s_cur = fetch(0)
    for c in range(n_chunks):
        if c + 1 < n_chunks:
            s_next = fetch(c + 1)
        state = _softmax_merge([state, _softmax_step(s_cur, kcb[c % 2], None)])
        if c + 1 < n_chunks:
            s_cur = s_next

    _, l, acc = state
    o_ref[...] = (acc / l).reshape(o_ref.shape)


def _sample_attn(qlat, qr, ckv_s, kr_s, cache_c, cache_krt, page_table, t_new, cp=8, n_slots=4):
    n_seq, n_pages = page_table.shape
    rows = N_HEADS * t_new
    grid_spec = pltpu.PrefetchScalarGridSpec(
        num_scalar_prefetch=1,
        grid=(n_seq,),
        in_specs=[
            pl.BlockSpec((N_HEADS, t_new, KV_LORA), lambda s, pt: (0, s, 0)),
            pl.BlockSpec((N_HEADS, t_new, LANES), lambda s, pt: (0, s, 0)),
            pl.BlockSpec((t_new, KV_LORA), lambda s, pt: (s, 0)),
            pl.BlockSpec((t_new, LANES), lambda s, pt: (s, 0)),
            pl.BlockSpec(memory_space=pl.ANY),
            pl.BlockSpec(memory_space=pl.ANY),
        ],
        out_specs=pl.BlockSpec((N_HEADS, t_new, KV_LORA), lambda s, pt: (0, s, 0)),
        scratch_shapes=[
            pltpu.VMEM((rows, KCAT), BF16),
            pltpu.VMEM((PAGE_SIZE, KCAT), BF16),
            pltpu.VMEM((n_slots, cp, PAGE_SIZE, KV_LORA), F32),
            pltpu.VMEM((n_slots, cp, QK_ROPE, PAGE_SIZE), F32),
            pltpu.VMEM((2, cp * PAGE_SIZE, KV_LORA), BF16),
            pltpu.VMEM((2, QK_ROPE, cp * PAGE_SIZE), BF16),
            pltpu.SemaphoreType.DMA((n_slots,)),
        ],
    )
    return pl.pallas_call(
        functools.partial(_sattn_kernel, cp=cp, n_chunks=n_pages // cp, t_new=t_new),
        grid_spec=grid_spec,
        out_shape=jax.ShapeDtypeStruct((N_HEADS, n_seq * t_new, KV_LORA), F32),
        compiler_params=_cparams("arbitrary"),
        name="sample_attn",
    )(page_table, qlat, qr, ckv_s, kr_s, cache_c, cache_krt)


def _ouv_kernel(o_ref, w_ref, out_ref):
    out_ref[...] = _dot(o_ref[...].astype(BF16), w_ref[...]).astype(out_ref.dtype)


def _o_uv(olat, w_uv2):
    _, rows, _ = olat.shape
    return pl.pallas_call(
        _ouv_kernel,
        grid=(N_HEADS,),
        in_specs=[
            pl.BlockSpec((None, rows, KV_LORA), lambda h: (h, 0, 0)),
            pl.BlockSpec((KV_LORA, V_DIM), lambda h: (0, h)),
        ],
        out_specs=pl.BlockSpec((rows, V_DIM), lambda h: (0, h)),
        out_shape=jax.ShapeDtypeStruct((rows, N_HEADS * V_DIM), BF16),
        compiler_params=_cparams("arbitrary"),
        name="o_uv",
    )(olat, w_uv2)


HALO = 2 * SUBLANES


def _pool_groups(load, u_of, cnt_of, wg_ref, ps_ref, store):
    group = wg_ref.shape[1]
    for g, w in enumerate(POOL_WINDOWS):
        sl = slice(g * group, (g + 1) * group)
        ws = load(0, sl)
        for k in range(1, w):
            ws = ws + load(k, sl)
        pooled = ws / cnt_of(w) - u_of(sl)
        pooled = pooled.reshape(-1, group).astype(BF16)
        store(sl, (_dot(pooled, wg_ref[g]) * ps_ref[:, sl]))


def _pool_prompt_kernel(u_ref, wg_ref, ps_ref, y_ref, ext_ref, *, tiles_per_seq):
    i = pl.program_id(0)
    tm = u_ref.shape[0]
    it = i % tiles_per_seq

    @pl.when(it == 0)
    def _():
        ext_ref[0:HALO, :] = jnp.zeros((HALO, ext_ref.shape[1]), F32)

    @pl.when(it != 0)
    def _():
        ext_ref[0:HALO, :] = ext_ref[tm:tm + HALO, :]

    ext_ref[HALO:HALO + tm, :] = u_ref[...]
    pos = it * tm + lax.broadcasted_iota(jnp.int32, (tm, 1), 0)

    def store(sl, val):
        y_ref[:, sl] = val.astype(y_ref.dtype)

    _pool_groups(
        lambda k, sl: ext_ref[HALO - k:HALO - k + tm, sl],
        lambda sl: u_ref[:, sl],
        lambda w: jnp.minimum(pos + 1, w).astype(F32),
        wg_ref, ps_ref, store)


def _pool_prompt(u, wg, ps, rows, seq):
    width = wg.shape[0] * wg.shape[1]
    tm = ROW_TILE
    return pl.pallas_call(
        functools.partial(_pool_prompt_kernel, tiles_per_seq=seq // tm),
        grid=(rows // tm,),
        in_specs=[
            pl.BlockSpec((tm, width), lambda i: (i, 0)),
            pl.BlockSpec(wg.shape, lambda i: (0, 0, 0)),
            pl.BlockSpec((1, width), lambda i: (0, 0)),
        ],
        out_specs=pl.BlockSpec((tm, width), lambda i: (i, 0)),
        out_shape=jax.ShapeDtypeStruct((rows, width), BF16),
        scratch_shapes=[pltpu.VMEM((HALO + tm, width), F32)],
        compiler_params=_cparams("arbitrary"),
        name="pool_prompt",
    )(u, wg, ps)


def _pool_sample_kernel(u_ref, st_ref, wg_ref, ps_ref, y_ref, ext_ref, *, pos0):
    t_new = u_ref.shape[1]
    ext_ref[:, 0:HALO, :] = st_ref[...]
    ext_ref[:, HALO:HALO + t_new, :] = u_ref[...]
    pos = pos0 + lax.broadcasted_iota(jnp.int32, (1, t_new, 1), 1)

    def store(sl, val):
        y_ref[:, sl] = val.astype(y_ref.dtype)

    _pool_groups(
        lambda k, sl: ext_ref[:, HALO - k:HALO - k + t_new, sl],
        lambda sl: u_ref[:, :, sl],
        lambda w: jnp.minimum(pos + 1, w).astype(F32),
        wg_ref, ps_ref, store)


def _pool_sample(u3, st, wg, ps, pos0, seq0, ns=64):
    n_seq = st.shape[0]
    _, t_new, width = u3.shape
    blk0 = seq0 // ns
    return pl.pallas_call(
        functools.partial(_pool_sample_kernel, pos0=pos0),
        grid=(n_seq // ns,),
        in_specs=[
            pl.BlockSpec((ns, t_new, width), lambda i: (blk0 + i, 0, 0)),
            pl.BlockSpec((ns, HALO, width), lambda i: (i, 0, 0)),
            pl.BlockSpec(wg.shape, lambda i: (0, 0, 0)),
            pl.BlockSpec((1, width), lambda i: (0, 0)),
        ],
        out_specs=pl.BlockSpec((ns * t_new, width), lambda i: (i, 0)),
        out_shape=jax.ShapeDtypeStruct((n_seq * t_new, width), BF16),
        scratch_shapes=[pltpu.VMEM((ns, HALO + t_new, width), F32)],
        compiler_params=_cparams("arbitrary"),
        name="pool_sample",
    )(u3, st, wg, ps)


def _merge_kernel(h_ref, oap_ref, oas_ref, ypp_ref, yps_ref, wt_hbm, ba_ref, bb_ref, wa_ref, wp_ref, o_ref,
                  wf_ref, wb_ref, sem, *, row0, n_blocks, n_prompt):
    j, i = pl.program_id(0), pl.program_id(1)
    tn = o_ref.shape[1]
    n = tn * n_blocks

    @pl.when(i == 0)
    def _():
        cps = [pltpu.make_async_copy(wt_hbm.at[pl.ds(row0 + br * n + j * tn, tn)], wf_ref.at[br], sem.at[br])
               for br in range(2)]
        for cp in cps:
            cp.start()
        for cp in cps:
            cp.wait()
        wb_ref[...] = wf_ref[...].astype(BF16)

    h = h_ref[...]
    ga = jax.nn.sigmoid(_dot_nt(h, wb_ref[0]) + ba_ref[...])
    gb = jax.nn.sigmoid(_dot_nt(h, wb_ref[1]) + bb_ref[...])

    def finish(oa_ref, yp_ref):
        a = _dot(oa_ref[...], wa_ref[...])
        p = _dot(yp_ref[...], wp_ref[...])
        o_ref[...] = (ga * a + gb * p).astype(o_ref.dtype)

    @pl.when(i < n_prompt)
    def _():
        finish(oap_ref, ypp_ref)

    @pl.when(i >= n_prompt)
    def _():
        finish(oas_ref, yps_ref)


def _merge(h, oa_p, oa_s, yp_p, yp_s, w_t, row0, b_gates, wa, wp, tn=512):
    m, d = h.shape
    ka, kp = wa.shape[0], wp.shape[0]
    n = wa.shape[1]
    nb = n // tn
    n_prompt = oa_p.shape[0] // ROW_TILE
    pr, sr = _prompt_rows(n_prompt), _sample_rows(n_prompt)
    bg = b_gates.reshape(1, -1)
    return pl.pallas_call(
        functools.partial(_merge_kernel, row0=row0, n_blocks=nb, n_prompt=n_prompt),
        grid=(nb, m // ROW_TILE),
        in_specs=[
            pl.BlockSpec((ROW_TILE, d), lambda j, i: (i, 0)),
            pl.BlockSpec((ROW_TILE, ka), lambda j, i: (pr(i), 0)),
            pl.BlockSpec((ROW_TILE, ka), lambda j, i: (sr(i), 0)),
            pl.BlockSpec((ROW_TILE, kp), lambda j, i: (pr(i), 0)),
            pl.BlockSpec((ROW_TILE, kp), lambda j, i: (sr(i), 0)),
            pl.BlockSpec(memory_space=pl.ANY),
            pl.BlockSpec((1, tn), lambda j, i: (0, j)),
            pl.BlockSpec((1, tn), lambda j, i: (0, nb + j)),
            pl.BlockSpec((ka, tn), lambda j, i: (0, j)),
            pl.BlockSpec((kp, tn), lambda j, i: (0, j)),
        ],
        out_specs=pl.BlockSpec((ROW_TILE, tn), lambda j, i: (i, j)),
        out_shape=jax.ShapeDtypeStruct((m, n), BF16),
        scratch_shapes=[pltpu.VMEM((2, tn, d), F32), pltpu.VMEM((2, tn, d), BF16), pltpu.SemaphoreType.DMA((2,))],
        compiler_params=_cparams("arbitrary", "arbitrary"),
        name="merge",
    )(h, oa_p, oa_s, yp_p, yp_s, w_t, bg, bg, wa, wp)


def _ffn_kernel(*refs, sample, tiles_per_seq, t_new):
    if sample:
        (x_ref, g_ref, wa_ref, wv_ref, cw_ref, cb_ref, wd_ref, gf_ref, p1_ref, p2_ref,
         y_ref, a_ref, hn_ref, acc_ref) = refs
    else:
        (x_ref, g_ref, wa_ref, wv_ref, cw_ref, cb_ref, wd_ref, gf_ref,
         y_ref, tail_ref, hn_ref, acc_ref, carry_ref) = refs
    i, f = pl.program_id(0), pl.program_id(1)
    tm = x_ref.shape[0]

    @pl.when(f == 0)
    def _():
        hn_ref[...] = _rms(x_ref[...], g_ref[...]).astype(hn_ref.dtype)
        acc_ref[...] = jnp.zeros(acc_ref.shape, F32)

    if not sample:
        @pl.when((i == 0) & (f == 0))
        def _():
            carry_ref[...] = jnp.zeros(carry_ref.shape, F32)

    hn = hn_ref[...]
    tf = wa_ref.shape[1]
    sw = tf // FFN_SPLIT
    cols = [slice(k * sw, (k + 1) * sw) for k in range(FFN_SPLIT)]
    ups = [(_dot(hn, wa_ref[:, cs]), _dot(hn, wv_ref[:, cs])) for cs in cols]
    hids = []
    for cs, (a, v) in zip(cols, ups):
        r1, r2 = pltpu.roll(a, 1, 0), pltpu.roll(a, 2, 0)
        if sample:
            rmod = lax.broadcasted_iota(jnp.int32, (tm, 1), 0) % t_new
            s1 = jnp.where(rmod >= 1, r1, p1_ref[:, cs])
            s2 = jnp.where(rmod >= 2, r2, p2_ref[:, cs])
            a_ref[:, cs] = a
        else:
            first = i % tiles_per_seq == 0
            prev = jnp.where(first, 0.0, carry_ref[f, :, cs])
            row8 = lax.broadcasted_iota(jnp.int32, (SUBLANES, 1), 0)
            h1 = jnp.where(row8 < 1, pltpu.roll(prev, 1, 0), r1[:SUBLANES])
            h2 = jnp.where(row8 < 2, pltpu.roll(prev, 2, 0), r2[:SUBLANES])
            s1 = jnp.concatenate([h1, r1[SUBLANES:]], axis=0)
            s2 = jnp.concatenate([h2, r2[SUBLANES:]], axis=0)
            tail = a[tm - SUBLANES:, :]
            carry_ref[f, :, cs] = tail
            tail_ref[:, cs] = tail
        conv = cb_ref[:, cs] + cw_ref[0:1, cs] * s2
        conv = conv + cw_ref[1:2, cs] * s1
        conv = conv + cw_ref[2:3, cs] * a
        half = 0.5 * conv
        hids.append(((half * v) * (jnp.tanh(half) + 1.0)).astype(BF16))
    acc = acc_ref[...]
    for cs, hid in zip(cols, hids):
        acc = acc + _dot(hid, wd_ref[cs, :])
    acc_ref[...] = acc

    @pl.when(f == pl.num_programs(1) - 1)
    def _():
        x2 = x_ref[...] + acc_ref[...]
        y_ref[...] = _rms(x2, gf_ref[...])


def _ffn(x, row0, rows, g, w_up, conv_w, conv_b, w_down, gf, *, seq=None, prev=None, t_new=None, tf=512):
    d = x.shape[1]
    d_ff = w_down.shape[0]
    nf = d_ff // tf
    tm = ROW_TILE
    blk0 = row0 // tm
    sample = prev is not None
    in_specs = [
        pl.BlockSpec((tm, d), lambda i, f: (blk0 + i, 0)),
        pl.BlockSpec((1, d), lambda i, f: (0, 0)),
        pl.BlockSpec((d, tf), lambda i, f: (0, f)),
        pl.BlockSpec((d, tf), lambda i, f: (0, nf + f)),
        pl.BlockSpec((CONV_W, tf), lambda i, f: (0, f)),
        pl.BlockSpec((1, tf), lambda i, f: (0, f)),
        pl.BlockSpec((tf, d), lambda i, f: (f, 0)),
        pl.BlockSpec((1, d), lambda i, f: (0, 0)),
    ]
    ins = [x, g.reshape(1, d), w_up, w_up, conv_w, conv_b.reshape(1, d_ff), w_down, gf.reshape(1, d)]
    scratch = [pltpu.VMEM((tm, d), BF16), pltpu.VMEM((tm, d), F32)]
    y_spec = pl.BlockSpec((tm, d), lambda i, f: (i, 0))
    y_shape = jax.ShapeDtypeStruct((rows, d), F32)
    if sample:
        in_specs += [pl.BlockSpec((tm, tf), lambda i, f: (i, f))] * 2
        ins += list(prev)
        out_specs = [y_spec, pl.BlockSpec((tm, tf), lambda i, f: (i, f))]
        out_shape = [y_shape, jax.ShapeDtypeStruct((rows, d_ff), F32)]
        tiles_per_seq = None
    else:
        out_specs = [y_spec, pl.BlockSpec((None, SUBLANES, tf), lambda i, f: (i, 0, f))]
        out_shape = [y_shape, jax.ShapeDtypeStruct((rows // tm, SUBLANES, d_ff), F32)]
        scratch.append(pltpu.VMEM((nf, SUBLANES, tf), F32))
        tiles_per_seq = seq // tm
    return pl.pallas_call(
        functools.partial(_ffn_kernel, sample=sample, tiles_per_seq=tiles_per_seq, t_new=t_new),
        grid=(rows // tm, nf),
        in_specs=in_specs,
        out_specs=out_specs,
        out_shape=out_shape,
        scratch_shapes=scratch,
        compiler_params=_cparams("arbitrary", "arbitrary"),
        name="ffn_sample" if sample else "ffn_prompt",
    )(*ins)


def _rope_table(pos):
    half = QK_ROPE // 2
    inv = ROPE_BASE ** (-jnp.arange(half, dtype=F32) / half)
    ang = pos[:, None] * inv[None, :]
    c, s = jnp.cos(ang), jnp.sin(ang)
    return jnp.concatenate([c, c, c, c, -s, s, -s, s], axis=1)


def _layer(xp, xs, dims, cache_c, cache_krt, page_table, state_pool, state_conv,
           norm_attn_g, w_in, b_gates, kv_norm_g, w_uk, w_uv, w_attn_out, w_pool_group, pool_scale,
           w_pool_out, w_o, norm_ffn_g, w_up, conv_w, conv_b, w_down, norm_final_g):
    batch, seq, n_seq, t_new, n_past = dims
    mp, ms = xp.shape[0], xs.shape[0]
    d = xp.shape[1]
    n_prompt = mp // ROW_TILE
    tiles_per_seq = seq // ROW_TILE
    q_cols = N_HEADS * (QK_NOPE + QK_ROPE)
    pool_w = w_pool_out.shape[0]
    c0, c2 = q_cols, q_cols + KV_LORA + QK_ROPE
    c3 = c2 + pool_w

    pos = jnp.concatenate([jnp.arange(seq, dtype=F32), jnp.tile(n_past + jnp.arange(t_new, dtype=F32), ROW_TILE // t_new)])
    tab = _rope_table(pos)

    def tab_row(i):
        return jnp.where(i < n_prompt, i % tiles_per_seq, tiles_per_seq)

    w_t = jnp.swapaxes(w_in, 0, 1)
    w_ukv = jnp.concatenate([w_uk.reshape(KV_LORA, -1), w_uv.reshape(KV_LORA, -1)], axis=1).astype(BF16)
    w_ukt = jnp.transpose(w_uk, (1, 2, 0)).astype(BF16)
    w_uv2 = w_uv.reshape(KV_LORA, -1).astype(BF16)

    h = _norm(xp, xs, norm_attn_g)
    q = _q_proj(h, w_t, tab, tab_row)
    ckv_p, ckv_s, kr_p, kr_s, u = _ckv_pool_proj(h, w_t, c0, pool_w, kv_norm_g, tab, tab_row, mp, ms)

    kp, vp = _kv_up(ckv_p, kr_p, w_ukv)
    o_p = _prompt_attn(q, kp, vp, batch, seq)
    qlat, qr = _q_latent(q, w_ukt, mp, ms)
    olat = _sample_attn(qlat, qr, ckv_s, kr_s, cache_c, cache_krt, page_table, t_new)
    o_s = _o_uv(olat, w_uv2)

    wpg = w_pool_group.astype(BF16)
    ps = pool_scale.reshape(1, pool_w)
    y_p = _pool_prompt(u, wpg, ps, mp, seq)
    u3 = u.reshape((mp + ms) // t_new, t_new, pool_w)
    st_pad = jnp.pad(state_pool, ((0, 0), (HALO - POOL_BUF, 0), (0, 0)))
    y_s = _pool_sample(u3, st_pad, wpg, ps, n_past, mp // t_new)

    merged = _merge(h, o_p, o_s, y_p, y_s, w_t, c3, b_gates, w_attn_out.astype(BF16), w_pool_out.astype(BF16))
    x1 = _mm(merged, w_o.astype(BF16), res=(xp, xs), n_prompt=n_prompt, tn=d, name="o_proj")

    w_up_b = w_up.astype(BF16)
    w_down_b = w_down.astype(BF16)
    y_prompt, tails = _ffn(x1, 0, mp, norm_ffn_g, w_up_b, conv_w, conv_b, w_down_b, norm_final_g, seq=seq)
    d_ff = w_down.shape[0]
    prev1 = jnp.pad(state_conv[:, CONV_W - 2:CONV_W - 1], ((0, 0), (0, t_new - 1), (0, 0))).reshape(ms, d_ff)
    prev2 = jnp.pad(state_conv, ((0, 0), (0, t_new - (CONV_W - 1)), (0, 0))).reshape(ms, d_ff)
    y_sample, a_s = _ffn(x1, mp, ms, norm_ffn_g, w_up_b, conv_w, conv_b, w_down_b, norm_final_g,
                         prev=(prev1, prev2), t_new=t_new)

    conv_p = tails[tiles_per_seq - 1::tiles_per_seq, SUBLANES - (CONV_W - 1):]
    conv_s = a_s.reshape(n_seq, t_new, d_ff)[:, t_new - (CONV_W - 1):]
    pool_p = jnp.stack([u[(b + 1) * seq - POOL_BUF:(b + 1) * seq] for b in range(batch)])
    u_s = u[mp:].reshape(n_seq, t_new, pool_w)
    pool_s = jnp.concatenate([state_pool, u_s], axis=1)[:, -POOL_BUF:]
    return (y_prompt, y_sample, ckv_p.reshape(batch, seq, KV_LORA), kr_p[:, :QK_ROPE].reshape(batch, seq, QK_ROPE),
            ckv_s.reshape(n_seq, t_new, KV_LORA), kr_s[:, :QK_ROPE].reshape(n_seq, t_new, QK_ROPE),
            pool_p, pool_s, conv_p, conv_s)


def kernel(x_prompt, x_sample, cache_kv_latent, cache_k_rope, page_table, state_pool, state_conv, norm_attn_g, w_in, b_gates, kv_norm_g, w_uk, w_uv, w_attn_out, w_pool_group, pool_scale, w_pool_out, w_o, norm_ffn_g, w_up, conv_w, conv_b, w_down, norm_final_g):
    batch, seq, d = x_prompt.shape
    n_seq, t_new, _ = x_sample.shape
    depth = w_in.shape[0]
    assert depth == 1, "the stacked-row pipeline below is written for a single layer"
    n_past = page_table.shape[1] * PAGE_SIZE
    dims = (batch, seq, n_seq, t_new, n_past)
    xp = x_prompt.reshape(batch * seq, d)
    xs = x_sample.reshape(n_seq * t_new, d)
    l = 0
    cache_c = cache_kv_latent.reshape(cache_kv_latent.shape[1:])
    cache_krt = jnp.swapaxes(cache_k_rope.reshape(cache_k_rope.shape[1:]), 1, 2)
    outs = _layer(xp, xs, dims, cache_c, cache_krt, page_table, state_pool[l], state_conv[l],
                  norm_attn_g[l], w_in[l], b_gates[l], kv_norm_g[l], w_uk[l], w_uv[l], w_attn_out[l],
                  w_pool_group[l], pool_scale[l], w_pool_out[l], w_o[l], norm_ffn_g[l], w_up[l], conv_w[l],
                  conv_b[l], w_down[l], norm_final_g)
    y_p, y_s = outs[0].reshape(batch, seq, d), outs[1].reshape(n_seq, t_new, d)
    return (y_p, y_s) + tuple(o[None] for o in outs[2:])
```

```python
import functools
import math

import jax
import jax.numpy as jnp
from jax import lax
from jax.experimental import pallas as pl
from jax.experimental.pallas import tpu as pltpu

F32 = jnp.float32
BF16 = jnp.bfloat16

N_HEADS = 16
QK_NOPE = 128
QK_ROPE = 64
V_DIM = 128
KV_LORA = 512
ROPE_BASE = 10000.0
ATTN_SCALE = (QK_NOPE + QK_ROPE) ** -0.5
EXP2_SCALE = ATTN_SCALE * math.log2(math.e)
PAGE_SIZE = 128
POOL_WINDOWS = (2, 4, 8, 16)
POOL_BUF = max(POOL_WINDOWS) - 1
CONV_W = 3
EPS = 1e-6

LANES = 128
SUBLANES = 8
HEAD_PAD = 2 * LANES
VMEM_LIMIT = 48 * 1024 * 1024

ROW_TILE = 512
FFN_SPLIT = 2


def _cparams(*sem):
    return pltpu.CompilerParams(dimension_semantics=sem, vmem_limit_bytes=VMEM_LIMIT)


def _dot(a, b):
    return jnp.dot(a, b, preferred_element_type=F32)


def _dot_nt(a, b):
    return lax.dot_general(a, b, (((1,), (1,)), ((), ())), preferred_element_type=F32)


def _rms(x, g):
    ms = jnp.mean(x * x, axis=-1, keepdims=True)
    return x * lax.rsqrt(ms + EPS) * g


def _rope128(x, ct, st):
    lane = lax.broadcasted_iota(jnp.int32, x.shape, 1)
    half = QK_ROPE // 2
    other = jnp.where(lane % QK_ROPE < half, pltpu.roll(x, LANES - half, 1), pltpu.roll(x, half, 1))
    return x * ct + other * st


def _softmax_step(s, vals, state):
    bm = jnp.max(s, axis=-1, keepdims=True)
    if state is None:
        m_new = bm
        p = jnp.exp2((s - m_new) * EXP2_SCALE)
        l_new = jnp.sum(p, axis=-1, keepdims=True)
        acc_new = _dot(p.astype(BF16), vals)
    else:
        m, l, acc = state
        m_new = jnp.maximum(m, bm)
        alpha = jnp.exp2((m - m_new) * EXP2_SCALE)
        p = jnp.exp2((s - m_new) * EXP2_SCALE)
        l_new = alpha * l + jnp.sum(p, axis=-1, keepdims=True)
        acc_new = alpha * acc + _dot(p.astype(BF16), vals)
    return m_new, l_new, acc_new


def _softmax_merge(states):
    m_all = states[0][0]
    for m, _, _ in states[1:]:
        m_all = jnp.maximum(m_all, m)
    l_all = acc_all = None
    for m, l, acc in states:
        w = jnp.exp2((m - m_all) * EXP2_SCALE)
        l_all = w * l if l_all is None else l_all + w * l
        acc_all = w * acc if acc_all is None else acc_all + w * acc
    return m_all, l_all, acc_all


def _prompt_rows(n_prompt):
    return lambda i: jnp.minimum(i, n_prompt - 1)


def _sample_rows(n_prompt):
    return lambda i: jnp.maximum(i - n_prompt, 0)


def _norm_kernel(xp_ref, xs_ref, g_ref, o_ref, *, n_prompt):
    i = pl.program_id(0)

    @pl.when(i < n_prompt)
    def _():
        o_ref[...] = _rms(xp_ref[...], g_ref[...]).astype(o_ref.dtype)

    @pl.when(i >= n_prompt)
    def _():
        o_ref[...] = _rms(xs_ref[...], g_ref[...]).astype(o_ref.dtype)


def _norm(xp, xs, g):
    d = xp.shape[1]
    n_prompt, n_sample = xp.shape[0] // ROW_TILE, xs.shape[0] // ROW_TILE
    pr, sr = _prompt_rows(n_prompt), _sample_rows(n_prompt)
    return pl.pallas_call(
        functools.partial(_norm_kernel, n_prompt=n_prompt),
        grid=(n_prompt + n_sample,),
        in_specs=[
            pl.BlockSpec((ROW_TILE, d), lambda i: (pr(i), 0)),
            pl.BlockSpec((ROW_TILE, d), lambda i: (sr(i), 0)),
            pl.BlockSpec((1, d), lambda i: (0, 0)),
        ],
        out_specs=pl.BlockSpec((ROW_TILE, d), lambda i: (i, 0)),
        out_shape=jax.ShapeDtypeStruct((xp.shape[0] + xs.shape[0], d), BF16),
        compiler_params=_cparams("arbitrary"),
        name="rmsnorm",
    )(xp, xs, g.reshape(1, d))


def _mm_kernel(*refs, n_prompt, has_res):
    h_ref, w_ref = refs[0], refs[1]
    o_ref = refs[-1]
    r = _dot(h_ref[...], w_ref[...])
    if has_res:
        i = pl.program_id(1)

        @pl.when(i < n_prompt)
        def _():
            o_ref[...] = refs[2][...] + r

        @pl.when(i >= n_prompt)
        def _():
            o_ref[...] = refs[3][...] + r
    else:
        o_ref[...] = r.astype(o_ref.dtype)


def _mm(h, w, *, res=None, n_prompt=None, out_dtype=F32, tn=512, name="mm"):
    m, kd = h.shape
    n = w.shape[1]
    ins = [h, w]
    specs = [pl.BlockSpec((ROW_TILE, kd), lambda j, i: (i, 0)), pl.BlockSpec((kd, tn), lambda j, i: (0, j))]
    if res is not None:
        pr, sr = _prompt_rows(n_prompt), _sample_rows(n_prompt)
        ins += list(res)
        specs += [pl.BlockSpec((ROW_TILE, tn), lambda j, i: (pr(i), j)), pl.BlockSpec((ROW_TILE, tn), lambda j, i: (sr(i), j))]
    return pl.pallas_call(
        functools.partial(_mm_kernel, n_prompt=n_prompt, has_res=res is not None),
        grid=(n // tn, m // ROW_TILE),
        in_specs=specs,
        out_specs=pl.BlockSpec((ROW_TILE, tn), lambda j, i: (i, j)),
        out_shape=jax.ShapeDtypeStruct((m, n), out_dtype),
        compiler_params=_cparams("arbitrary", "arbitrary"),
        name=name,
    )(*ins)


def _q_kernel(h_ref, wt_ref, tab_ref, o_ref, wb_ref):
    hd = QK_NOPE + QK_ROPE
    pair = 2 * hd
    n_pairs = wb_ref.shape[0] // pair

    @pl.when(pl.program_id(1) == 0)
    def _():
        for p in range(n_pairs):
            a0, b0 = 2 * p * hd, (2 * p + 1) * hd
            w0 = p * pair
            wb_ref[w0:w0 + QK_NOPE, :] = wt_ref[a0:a0 + QK_NOPE, :].astype(BF16)
            wb_ref[w0 + QK_NOPE:w0 + 2 * QK_NOPE, :] = wt_ref[b0:b0 + QK_NOPE, :].astype(BF16)
            wb_ref[w0 + 2 * QK_NOPE:w0 + 2 * QK_NOPE + QK_ROPE, :] = wt_ref[a0 + QK_NOPE:a0 + hd, :].astype(BF16)
            wb_ref[w0 + 2 * QK_NOPE + QK_ROPE:w0 + pair, :] = wt_ref[b0 + QK_NOPE:b0 + hd, :].astype(BF16)

    r = _dot_nt(h_ref[...], wb_ref[...])
    ct, st = tab_ref[:, :LANES], tab_ref[:, LANES:]
    lane = lax.broadcasted_iota(jnp.int32, (r.shape[0], LANES), 1)
    for p in range(n_pairs):
        r0 = p * pair
        oa, ob = 2 * p * HEAD_PAD, (2 * p + 1) * HEAD_PAD
        o_ref[:, oa:oa + LANES] = r[:, r0:r0 + LANES].astype(o_ref.dtype)
        o_ref[:, ob:ob + LANES] = r[:, r0 + LANES:r0 + 2 * LANES].astype(o_ref.dtype)
        rp = _rope128(r[:, r0 + 2 * LANES:r0 + 3 * LANES], ct, st)
        o_ref[:, oa + LANES:oa + HEAD_PAD] = jnp.where(lane < QK_ROPE, rp, 0.0).astype(o_ref.dtype)
        o_ref[:, ob + LANES:ob + HEAD_PAD] = jnp.where(lane < QK_ROPE, pltpu.roll(rp, QK_ROPE, 1), 0.0).astype(o_ref.dtype)


def _q_proj(h, w_t, tab, tab_row, heads_per_block=4):
    m, kd = h.shape
    hd = QK_NOPE + QK_ROPE
    tn = heads_per_block * HEAD_PAD
    return pl.pallas_call(
        _q_kernel,
        grid=(N_HEADS // heads_per_block, m // ROW_TILE),
        in_specs=[
            pl.BlockSpec((ROW_TILE, kd), lambda j, i: (i, 0)),
            pl.BlockSpec((heads_per_block * hd, kd), lambda j, i: (j, 0)),
            pl.BlockSpec((ROW_TILE, 2 * LANES), lambda j, i: (tab_row(i), 0)),
        ],
        out_specs=pl.BlockSpec((ROW_TILE, tn), lambda j, i: (i, j)),
        out_shape=jax.ShapeDtypeStruct((m, N_HEADS * HEAD_PAD), BF16),
        scratch_shapes=[pltpu.VMEM((heads_per_block * hd, kd), BF16)],
        compiler_params=_cparams("arbitrary", "arbitrary"),
        name="q_proj",
    )(h, w_t, tab)


def _ckv_kernel(h_ref, wt_hbm, g_ref, tab_ref, cp_ref, cs_ref, krp_ref, krs_ref, u_ref, wf_ref, wb_ref, sem,
                *, row0, n_prompt):
    i = pl.program_id(0)
    n_ck = KV_LORA + QK_ROPE
    n_ck_pad = KV_LORA + LANES

    @pl.when(i == 0)
    def _():
        cp = pltpu.make_async_copy(wt_hbm.at[pl.ds(row0, wf_ref.shape[0])], wf_ref, sem)
        cp.start()
        cp.wait()
        wb_ref[0:n_ck, :] = wf_ref[0:n_ck, :].astype(BF16)
        wb_ref[n_ck:n_ck_pad, :] = jnp.zeros((n_ck_pad - n_ck, wb_ref.shape[1]), BF16)
        wb_ref[n_ck_pad:, :] = wf_ref[n_ck:, :].astype(BF16)

    r = _dot_nt(h_ref[...], wb_ref[...])
    c = _rms(r[:, :KV_LORA], g_ref[...])
    kr = _rope128(r[:, KV_LORA:n_ck_pad], tab_ref[:, :LANES], tab_ref[:, LANES:])
    u_ref[...] = r[:, n_ck_pad:]

    @pl.when(i < n_prompt)
    def _():
        cp_ref[...] = c
        krp_ref[...] = kr

    @pl.when(i >= n_prompt)
    def _():
        cs_ref[...] = c
        krs_ref[...] = kr


def _ckv_pool_proj(h, w_t, row0, pool_w, g, tab, tab_row, mp, ms):
    m, kd = h.shape
    n_rows = KV_LORA + QK_ROPE + pool_w
    n_pad = KV_LORA + LANES + pool_w
    n_prompt = mp // ROW_TILE
    pr, sr = _prompt_rows(n_prompt), _sample_rows(n_prompt)
    return pl.pallas_call(
        functools.partial(_ckv_kernel, row0=row0, n_prompt=n_prompt),
        grid=(m // ROW_TILE,),
        in_specs=[
            pl.BlockSpec((ROW_TILE, kd), lambda i: (i, 0)),
            pl.BlockSpec(memory_space=pl.ANY),
            pl.BlockSpec((1, KV_LORA), lambda i: (0, 0)),
            pl.BlockSpec((ROW_TILE, 2 * LANES), lambda i: (tab_row(i), 0)),
        ],
        out_specs=[
            pl.BlockSpec((ROW_TILE, KV_LORA), lambda i: (pr(i), 0)),
            pl.BlockSpec((ROW_TILE, KV_LORA), lambda i: (sr(i), 0)),
            pl.BlockSpec((ROW_TILE, LANES), lambda i: (pr(i), 0)),
            pl.BlockSpec((ROW_TILE, LANES), lambda i: (sr(i), 0)),
            pl.BlockSpec((ROW_TILE, pool_w), lambda i: (i, 0)),
        ],
        out_shape=[
            jax.ShapeDtypeStruct((mp, KV_LORA), F32), jax.ShapeDtypeStruct((ms, KV_LORA), F32),
            jax.ShapeDtypeStruct((mp, LANES), F32), jax.ShapeDtypeStruct((ms, LANES), F32),
            jax.ShapeDtypeStruct((m, pool_w), F32),
        ],
        scratch_shapes=[pltpu.VMEM((n_rows, kd), F32), pltpu.VMEM((n_pad, kd), BF16), pltpu.SemaphoreType.DMA(())],
        compiler_params=_cparams("arbitrary"),
        name="ckv_pool_proj",
    )(h, w_t, g.reshape(1, KV_LORA), tab)


def _kvup_kernel(c_ref, kr_ref, w_ref, k_ref, v_ref):
    r = _dot(c_ref[...].astype(BF16), w_ref[...])
    nk = N_HEADS * QK_NOPE
    v_ref[...] = r[:, nk:].astype(v_ref.dtype)
    krb = kr_ref[...].astype(k_ref.dtype)
    for hh in range(N_HEADS):
        k_ref[:, hh * HEAD_PAD:hh * HEAD_PAD + LANES] = r[:, hh * QK_NOPE:(hh + 1) * QK_NOPE].astype(k_ref.dtype)
        k_ref[:, hh * HEAD_PAD + LANES:(hh + 1) * HEAD_PAD] = krb


def _kv_up(ckv, kr, w_ukv, tm=256):
    rows = ckv.shape[0]
    n = w_ukv.shape[1]
    return pl.pallas_call(
        _kvup_kernel,
        grid=(rows // tm,),
        in_specs=[
            pl.BlockSpec((tm, KV_LORA), lambda i: (i, 0)),
            pl.BlockSpec((tm, LANES), lambda i: (i, 0)),
            pl.BlockSpec((KV_LORA, n), lambda i: (0, 0)),
        ],
        out_specs=[pl.BlockSpec((tm, N_HEADS * HEAD_PAD), lambda i: (i, 0)), pl.BlockSpec((tm, N_HEADS * V_DIM), lambda i: (i, 0))],
        out_shape=[jax.ShapeDtypeStruct((rows, N_HEADS * HEAD_PAD), BF16), jax.ShapeDtypeStruct((rows, N_HEADS * V_DIM), BF16)],
        compiler_params=_cparams("arbitrary"),
        name="kv_up",
    )(ckv, kr, w_ukv)


def _pattn_kernel(q_ref, k_ref, v_ref, o_ref, *, tq):
    nq = q_ref.shape[0] // tq
    row = lax.broadcasted_iota(jnp.int32, (tq, tq), 0)
    col = lax.broadcasted_iota(jnp.int32, (tq, tq), 1)
    for qi in range(nq):
        q = q_ref[qi * tq:(qi + 1) * tq, :]
        state = None
        for ki in range(qi + 1):
            s = _dot_nt(q, k_ref[ki * tq:(ki + 1) * tq, :])
            if ki == qi:
                s = jnp.where(col <= row, s, -jnp.inf)
            state = _softmax_step(s, v_ref[ki * tq:(ki + 1) * tq, :], state)
        _, l, acc = state
        o_ref[qi * tq:(qi + 1) * tq, :] = (acc / l).astype(o_ref.dtype)


def _prompt_attn(q, k, v, batch, seq, tq=512):
    return pl.pallas_call(
        functools.partial(_pattn_kernel, tq=tq),
        grid=(batch, N_HEADS),
        in_specs=[
            pl.BlockSpec((seq, HEAD_PAD), lambda b, h: (b, h)),
            pl.BlockSpec((seq, HEAD_PAD), lambda b, h: (b, h)),
            pl.BlockSpec((seq, V_DIM), lambda b, h: (b, h)),
        ],
        out_specs=pl.BlockSpec((seq, V_DIM), lambda b, h: (b, h)),
        out_shape=jax.ShapeDtypeStruct((batch * seq, N_HEADS * V_DIM), BF16),
        compiler_params=_cparams("arbitrary", "arbitrary"),
        name="prompt_attn",
    )(q, k, v)


def _qlat_kernel(q_ref, w_ref, ql_ref, qr_ref):
    q = q_ref[...]
    ql_ref[...] = _dot(q[:, :QK_NOPE], w_ref[...])
    qr_ref[...] = q[:, LANES:].astype(F32)


def _q_latent(q, w_ukt, row0, rows):
    blk = row0 // rows
    return pl.pallas_call(
        _qlat_kernel,
        grid=(N_HEADS,),
        in_specs=[
            pl.BlockSpec((rows, HEAD_PAD), lambda h: (blk, h)),
            pl.BlockSpec((None, QK_NOPE, KV_LORA), lambda h: (h, 0, 0)),
        ],
        out_specs=[
            pl.BlockSpec((None, rows, KV_LORA), lambda h: (h, 0, 0)),
            pl.BlockSpec((None, rows, LANES), lambda h: (h, 0, 0)),
        ],
        out_shape=[jax.ShapeDtypeStruct((N_HEADS, rows, KV_LORA), F32), jax.ShapeDtypeStruct((N_HEADS, rows, LANES), F32)],
        compiler_params=_cparams("arbitrary"),
        name="q_latent",
    )(q, w_ukt)


KCAT = KV_LORA + LANES


def _sattn_kernel(pt_ref, ql_ref, qr_ref, cn_ref, krn_ref, cache_c, cache_krt, o_ref,
                  qcat, ncat, cbuf, krbuf, kcb, krtb, sem, *, cp, n_chunks, t_new):
    seq = pl.program_id(0)
    n_seq = pl.num_programs(0)
    rows = N_HEADS * t_new
    n_slots = cbuf.shape[0]
    assert n_chunks % n_slots == 0

    def copies(sq, chunk):
        slot = chunk % n_slots
        out = []
        for k in range(cp):
            page = pt_ref[sq, chunk * cp + k]
            out.append(pltpu.make_async_copy(cache_c.at[page], cbuf.at[slot, k], sem.at[slot]))
            out.append(pltpu.make_async_copy(cache_krt.at[page], krbuf.at[slot, k], sem.at[slot]))
        return out

    def start(sq, chunk):
        for n, c in enumerate(copies(sq, chunk)):
            c.start(priority=(n // 2) % 2)

    def fetch(chunk):
        slot, half = chunk % n_slots, chunk % 2
        for c in copies(seq, chunk):
            c.wait()
        for k in range(cp):
            kcb[half, k * PAGE_SIZE:(k + 1) * PAGE_SIZE, :] = cbuf[slot, k].astype(BF16)
            krtb[half, :, k * PAGE_SIZE:(k + 1) * PAGE_SIZE] = krbuf[slot, k].astype(BF16)
        ahead = chunk + n_slots
        if ahead < n_chunks:
            start(seq, ahead)
        else:
            @pl.when(seq + 1 < n_seq)
            def _():
                start(seq + 1, ahead - n_chunks)
        return _dot_nt(qcat[:, :KV_LORA], kcb[half]) + _dot(qcat[:, KV_LORA:KV_LORA + QK_ROPE], krtb[half])

    @pl.when(seq == 0)
    def _():
        for chunk in range(n_slots):
            start(seq, chunk)

    qcat[:, :KV_LORA] = ql_ref[...].reshape(rows, KV_LORA).astype(BF16)
    qcat[:, KV_LORA:] = qr_ref[...].reshape(rows, LANES).astype(BF16)
    ncat[...] = jnp.zeros(ncat.shape, BF16)
    ncat[0:t_new, :KV_LORA] = cn_ref[...].astype(BF16)
    ncat[0:t_new, KV_LORA:] = krn_ref[...].astype(BF16)
    s_new = _dot_nt(qcat[...], ncat[...])
    row = lax.broadcasted_iota(jnp.int32, s_new.shape, 0)
    col = lax.broadcasted_iota(jnp.int32, s_new.shape, 1)
    s_new = jnp.where(col <= row % t_new, s_new, -jnp.inf)
    state = _softmax_step(s_new, ncat[:, :KV_LORA], None)

    s_cur = fetch(0)
    for c in range(n_chunks):
        if c + 1 < n_chunks:
            s_next = fetch(c + 1)
        state = _softmax_merge([state, _softmax_step(s_cur, kcb[c % 2], None)])
        if c + 1 < n_chunks:
            s_cur = s_next

    _, l, acc = state
    o_ref[...] = (acc / l).reshape(o_ref.shape)


def _sample_attn(qlat, qr, ckv_s, kr_s, cache_c, cache_krt, page_table, t_new, cp=8, n_slots=4):
    n_seq, n_pages = page_table.shape
    rows = N_HEADS * t_new
    grid_spec = pltpu.PrefetchScalarGridSpec(
        num_scalar_prefetch=1,
        grid=(n_seq,),
        in_specs=[
            pl.BlockSpec((N_HEADS, t_new, KV_LORA), lambda s, pt: (0, s, 0)),
            pl.BlockSpec((N_HEADS, t_new, LANES), lambda s, pt: (0, s, 0)),
            pl.BlockSpec((t_new, KV_LORA), lambda s, pt: (s, 0)),
            pl.BlockSpec((t_new, LANES), lambda s, pt: (s, 0)),
            pl.BlockSpec(memory_space=pl.ANY),
            pl.BlockSpec(memory_space=pl.ANY),
        ],
        out_specs=pl.BlockSpec((N_HEADS, t_new, KV_LORA), lambda s, pt: (0, s, 0)),
        scratch_shapes=[
            pltpu.VMEM((rows, KCAT), BF16),
            pltpu.VMEM((PAGE_SIZE, KCAT), BF16),
            pltpu.VMEM((n_slots, cp, PAGE_SIZE, KV_LORA), F32),
            pltpu.VMEM((n_slots, cp, QK_ROPE, PAGE_SIZE), F32),
            pltpu.VMEM((2, cp * PAGE_SIZE, KV_LORA), BF16),
            pltpu.VMEM((2, QK_ROPE, cp * PAGE_SIZE), BF16),
            pltpu.SemaphoreType.DMA((n_slots,)),
        ],
    )
    return pl.pallas_call(
        functools.partial(_sattn_kernel, cp=cp, n_chunks=n_pages // cp, t_new=t_new),
        grid_spec=grid_spec,
        out_shape=jax.ShapeDtypeStruct((N_HEADS, n_seq * t_new, KV_LORA), F32),
        compiler_params=_cparams("arbitrary"),
        name="sample_attn",
    )(page_table, qlat, qr, ckv_s, kr_s, cache_c, cache_krt)


def _ouv_kernel(o_ref, w_ref, out_ref):
    out_ref[...] = _dot(o_ref[...].astype(BF16), w_ref[...]).astype(out_ref.dtype)


def _o_uv(olat, w_uv2):
    _, rows, _ = olat.shape
    return pl.pallas_call(
        _ouv_kernel,
        grid=(N_HEADS,),
        in_specs=[
            pl.BlockSpec((None, rows, KV_LORA), lambda h: (h, 0, 0)),
            pl.BlockSpec((KV_LORA, V_DIM), lambda h: (0, h)),
        ],
        out_specs=pl.BlockSpec((rows, V_DIM), lambda h: (0, h)),
        out_shape=jax.ShapeDtypeStruct((rows, N_HEADS * V_DIM), BF16),
        compiler_params=_cparams("arbitrary"),
        name="o_uv",
    )(olat, w_uv2)


HALO = 2 * SUBLANES


def _pool_groups(load, u_of, cnt_of, wg_ref, ps_ref, store):
    group = wg_ref.shape[1]
    for g, w in enumerate(POOL_WINDOWS):
        sl = slice(g * group, (g + 1) * group)
        ws = load(0, sl)
        for k in range(1, w):
            ws = ws + load(k, sl)
        pooled = ws / cnt_of(w) - u_of(sl)
        pooled = pooled.reshape(-1, group).astype(BF16)
        store(sl, (_dot(pooled, wg_ref[g]) * ps_ref[:, sl]))


def _pool_prompt_kernel(u_ref, wg_ref, ps_ref, y_ref, ext_ref, *, tiles_per_seq):
    i = pl.program_id(0)
    tm = u_ref.shape[0]
    it = i % tiles_per_seq

    @pl.when(it == 0)
    def _():
        ext_ref[0:HALO, :] = jnp.zeros((HALO, ext_ref.shape[1]), F32)

    @pl.when(it != 0)
    def _():
        ext_ref[0:HALO, :] = ext_ref[tm:tm + HALO, :]

    ext_ref[HALO:HALO + tm, :] = u_ref[...]
    pos = it * tm + lax.broadcasted_iota(jnp.int32, (tm, 1), 0)

    def store(sl, val):
        y_ref[:, sl] = val.astype(y_ref.dtype)

    _pool_groups(
        lambda k, sl: ext_ref[HALO - k:HALO - k + tm, sl],
        lambda sl: u_ref[:, sl],
        lambda w: jnp.minimum(pos + 1, w).astype(F32),
        wg_ref, ps_ref, store)


def _pool_prompt(u, wg, ps, rows, seq):
    width = wg.shape[0] * wg.shape[1]
    tm = ROW_TILE
    return pl.pallas_call(
        functools.partial(_pool_prompt_kernel, tiles_per_seq=seq // tm),
        grid=(rows // tm,),
        in_specs=[
            pl.BlockSpec((tm, width), lambda i: (i, 0)),
            pl.BlockSpec(wg.shape, lambda i: (0, 0, 0)),
            pl.BlockSpec((1, width), lambda i: (0, 0)),
        ],
        out_specs=pl.BlockSpec((tm, width), lambda i: (i, 0)),
        out_shape=jax.ShapeDtypeStruct((rows, width), BF16),
        scratch_shapes=[pltpu.VMEM((HALO + tm, width), F32)],
        compiler_params=_cparams("arbitrary"),
        name="pool_prompt",
    )(u, wg, ps)


def _pool_sample_kernel(u_ref, st_ref, wg_ref, ps_ref, y_ref, ext_ref, *, pos0):
    t_new = u_ref.shape[1]
    ext_ref[:, 0:HALO, :] = st_ref[...]
    ext_ref[:, HALO:HALO + t_new, :] = u_ref[...]
    pos = pos0 + lax.broadcasted_iota(jnp.int32, (1, t_new, 1), 1)

    def store(sl, val):
        y_ref[:, sl] = val.astype(y_ref.dtype)

    _pool_groups(
        lambda k, sl: ext_ref[:, HALO - k:HALO - k + t_new, sl],
        lambda sl: u_ref[:, :, sl],
        lambda w: jnp.minimum(pos + 1, w).astype(F32),
        wg_ref, ps_ref, store)


def _pool_sample(u3, st, wg, ps, pos0, seq0, ns=64):
    n_seq = st.shape[0]
    _, t_new, width = u3.shape
    blk0 = seq0 // ns
    return pl.pallas_call(
        functools.partial(_pool_sample_kernel, pos0=pos0),
        grid=(n_seq // ns,),
        in_specs=[
            pl.BlockSpec((ns, t_new, width), lambda i: (blk0 + i, 0, 0)),
            pl.BlockSpec((ns, HALO, width), lambda i: (i, 0, 0)),
            pl.BlockSpec(wg.shape, lambda i: (0, 0, 0)),
            pl.BlockSpec((1, width), lambda i: (0, 0)),
        ],
        out_specs=pl.BlockSpec((ns * t_new, width), lambda i: (i, 0)),
        out_shape=jax.ShapeDtypeStruct((n_seq * t_new, width), BF16),
        scratch_shapes=[pltpu.VMEM((ns, HALO + t_new, width), F32)],
        compiler_params=_cparams("arbitrary"),
        name="pool_sample",
    )(u3, st, wg, ps)


def _merge_kernel(h_ref, oap_ref, oas_ref, ypp_ref, yps_ref, wt_hbm, ba_ref, bb_ref, wa_ref, wp_ref, o_ref,
                  wf_ref, wb_ref, sem, *, row0, n_blocks, n_prompt):
    j, i = pl.program_id(0), pl.program_id(1)
    tn = o_ref.shape[1]
    n = tn * n_blocks

    @pl.when(i == 0)
    def _():
        cps = [pltpu.make_async_copy(wt_hbm.at[pl.ds(row0 + br * n + j * tn, tn)], wf_ref.at[br], sem.at[br])
               for br in range(2)]
        for cp in cps:
            cp.start()
        for cp in cps:
            cp.wait()
        wb_ref[...] = wf_ref[...].astype(BF16)

    h = h_ref[...]
    ga = jax.nn.sigmoid(_dot_nt(h, wb_ref[0]) + ba_ref[...])
    gb = jax.nn.sigmoid(_dot_nt(h, wb_ref[1]) + bb_ref[...])

    def finish(oa_ref, yp_ref):
        a = _dot(oa_ref[...], wa_ref[...])
        p = _dot(yp_ref[...], wp_ref[...])
        o_ref[...] = (ga * a + gb * p).astype(o_ref.dtype)

    @pl.when(i < n_prompt)
    def _():
        finish(oap_ref, ypp_ref)

    @pl.when(i >= n_prompt)
    def _():
        finish(oas_ref, yps_ref)


def _merge(h, oa_p, oa_s, yp_p, yp_s, w_t, row0, b_gates, wa, wp, tn=512):
    m, d = h.shape
    ka, kp = wa.shape[0], wp.shape[0]
    n = wa.shape[1]
    nb = n // tn
    n_prompt = oa_p.shape[0] // ROW_TILE
    pr, sr = _prompt_rows(n_prompt), _sample_rows(n_prompt)
    bg = b_gates.reshape(1, -1)
    return pl.pallas_call(
        functools.partial(_merge_kernel, row0=row0, n_blocks=nb, n_prompt=n_prompt),
        grid=(nb, m // ROW_TILE),
        in_specs=[
            pl.BlockSpec((ROW_TILE, d), lambda j, i: (i, 0)),
            pl.BlockSpec((ROW_TILE, ka), lambda j, i: (pr(i), 0)),
            pl.BlockSpec((ROW_TILE, ka), lambda j, i: (sr(i), 0)),
            pl.BlockSpec((ROW_TILE, kp), lambda j, i: (pr(i), 0)),
            pl.BlockSpec((ROW_TILE, kp), lambda j, i: (sr(i), 0)),
            pl.BlockSpec(memory_space=pl.ANY),
            pl.BlockSpec((1, tn), lambda j, i: (0, j)),
            pl.BlockSpec((1, tn), lambda j, i: (0, nb + j)),
            pl.BlockSpec((ka, tn), lambda j, i: (0, j)),
            pl.BlockSpec((kp, tn), lambda j, i: (0, j)),
        ],
        out_specs=pl.BlockSpec((ROW_TILE, tn), lambda j, i: (i, j)),
        out_shape=jax.ShapeDtypeStruct((m, n), BF16),
        scratch_shapes=[pltpu.VMEM((2, tn, d), F32), pltpu.VMEM((2, tn, d), BF16), pltpu.SemaphoreType.DMA((2,))],
        compiler_params=_cparams("arbitrary", "arbitrary"),
        name="merge",
    )(h, oa_p, oa_s, yp_p, yp_s, w_t, bg, bg, wa, wp)


def _ffn_kernel(*refs, sample, tiles_per_seq, t_new):
    if sample:
        (x_ref, g_ref, wa_ref, wv_ref, cw_ref, cb_ref, wd_ref, gf_ref, st_ref,
         y_ref, cs_ref, hn_ref, acc_ref) = refs
        assert CONV_W == 3
    else:
        (x_ref, g_ref, wa_ref, wv_ref, cw_ref, cb_ref, wd_ref, gf_ref,
         y_ref, tail_ref, hn_ref, acc_ref, carry_ref) = refs
    i, f = pl.program_id(0), pl.program_id(1)
    tm = x_ref.shape[0]

    @pl.when(f == 0)
    def _():
        hn_ref[...] = _rms(x_ref[...], g_ref[...]).astype(hn_ref.dtype)
        acc_ref[...] = jnp.zeros(acc_ref.shape, F32)

    if not sample:
        @pl.when((i == 0) & (f == 0))
        def _():
            carry_ref[...] = jnp.zeros(carry_ref.shape, F32)

    hn = hn_ref[...]
    tf = wa_ref.shape[1]
    sw = tf // FFN_SPLIT
    cols = [slice(k * sw, (k + 1) * sw) for k in range(FFN_SPLIT)]
    ups = [(_dot(hn, wa_ref[:, cs]), _dot(hn, wv_ref[:, cs])) for cs in cols]
    hids = []
    for cs, (a, v) in zip(cols, ups):
        r1, r2 = pltpu.roll(a, 1, 0), pltpu.roll(a, 2, 0)
        if sample:
            ns = tm // t_new
            a, v, r1, r2 = (z.reshape(ns, t_new, sw) for z in (a, v, r1, r2))
            t = lax.broadcasted_iota(jnp.int32, (1, t_new, 1), 1)
            st0, st1 = st_ref[:, 0:1, cs], st_ref[:, 1:2, cs]
            s1 = jnp.where(t >= 1, r1, st1)
            s2 = jnp.where(t >= 2, r2, jnp.where(t == 0, st0, st1))
            cs_ref[:, :, cs] = a[:, t_new - (CONV_W - 1):, :]
        else:
            first = i % tiles_per_seq == 0
            prev = jnp.where(first, 0.0, carry_ref[f, :, cs])
            row8 = lax.broadcasted_iota(jnp.int32, (SUBLANES, 1), 0)
            h1 = jnp.where(row8 < 1, pltpu.roll(prev, 1, 0), r1[:SUBLANES])
            h2 = jnp.where(row8 < 2, pltpu.roll(prev, 2, 0), r2[:SUBLANES])
            s1 = jnp.concatenate([h1, r1[SUBLANES:]], axis=0)
            s2 = jnp.concatenate([h2, r2[SUBLANES:]], axis=0)
            tail = a[tm - SUBLANES:, :]
            carry_ref[f, :, cs] = tail
            tail_ref[:, cs] = tail
        conv = cb_ref[:, cs] + cw_ref[0:1, cs] * s2
        conv = conv + cw_ref[1:2, cs] * s1
        conv = conv + cw_ref[2:3, cs] * a
        half = 0.5 * conv
        hids.append(((half * v) * (jnp.tanh(half) + 1.0)).reshape(tm, sw).astype(BF16))
    acc = acc_ref[...]
    for cs, hid in zip(cols, hids):
        acc = acc + _dot(hid, wd_ref[cs, :])
    acc_ref[...] = acc

    @pl.when(f == pl.num_programs(1) - 1)
    def _():
        x2 = x_ref[...] + acc_ref[...]
        y_ref[...] = _rms(x2, gf_ref[...])


def _ffn(x, row0, rows, g, w_up, conv_w, conv_b, w_down, gf, *, seq=None, prev=None, t_new=None, tf=512):
    d = x.shape[1]
    d_ff = w_down.shape[0]
    nf = d_ff // tf
    tm = ROW_TILE
    blk0 = row0 // tm
    sample = prev is not None
    in_specs = [
        pl.BlockSpec((tm, d), lambda i, f: (blk0 + i, 0)),
        pl.BlockSpec((1, d), lambda i, f: (0, 0)),
        pl.BlockSpec((d, tf), lambda i, f: (0, f)),
        pl.BlockSpec((d, tf), lambda i, f: (0, nf + f)),
        pl.BlockSpec((CONV_W, tf), lambda i, f: (0, f)),
        pl.BlockSpec((1, tf), lambda i, f: (0, f)),
        pl.BlockSpec((tf, d), lambda i, f: (f, 0)),
        pl.BlockSpec((1, d), lambda i, f: (0, 0)),
    ]
    ins = [x, g.reshape(1, d), w_up, w_up, conv_w, conv_b.reshape(1, d_ff), w_down, gf.reshape(1, d)]
    scratch = [pltpu.VMEM((tm, d), BF16), pltpu.VMEM((tm, d), F32)]
    y_spec = pl.BlockSpec((tm, d), lambda i, f: (i, 0))
    y_shape = jax.ShapeDtypeStruct((rows, d), F32)
    if sample:
        st_spec = pl.BlockSpec((tm // t_new, CONV_W - 1, tf), lambda i, f: (i, 0, f))
        in_specs.append(st_spec)
        ins.append(prev)
        out_specs = [y_spec, st_spec]
        out_shape = [y_shape, jax.ShapeDtypeStruct(prev.shape, F32)]
        tiles_per_seq = None
    else:
        out_specs = [y_spec, pl.BlockSpec((None, SUBLANES, tf), lambda i, f: (i, 0, f))]
        out_shape = [y_shape, jax.ShapeDtypeStruct((rows // tm, SUBLANES, d_ff), F32)]
        scratch.append(pltpu.VMEM((nf, SUBLANES, tf), F32))
        tiles_per_seq = seq // tm
    return pl.pallas_call(
        functools.partial(_ffn_kernel, sample=sample, tiles_per_seq=tiles_per_seq, t_new=t_new),
        grid=(rows // tm, nf),
        in_specs=in_specs,
        out_specs=out_specs,
        out_shape=out_shape,
        scratch_shapes=scratch,
        compiler_params=_cparams("arbitrary", "arbitrary"),
        name="ffn_sample" if sample else "ffn_prompt",
    )(*ins)


def _rope_table(pos):
    half = QK_ROPE // 2
    inv = ROPE_BASE ** (-jnp.arange(half, dtype=F32) / half)
    ang = pos[:, None] * inv[None, :]
    c, s = jnp.cos(ang), jnp.sin(ang)
    return jnp.concatenate([c, c, c, c, -s, s, -s, s], axis=1)


def _layer(xp, xs, dims, cache_c, cache_krt, page_table, state_pool, state_conv,
           norm_attn_g, w_in, b_gates, kv_norm_g, w_uk, w_uv, w_attn_out, w_pool_group, pool_scale,
           w_pool_out, w_o, norm_ffn_g, w_up, conv_w, conv_b, w_down, norm_final_g):
    batch, seq, n_seq, t_new, n_past = dims
    mp, ms = xp.shape[0], xs.shape[0]
    d = xp.shape[1]
    n_prompt = mp // ROW_TILE
    tiles_per_seq = seq // ROW_TILE
    q_cols = N_HEADS * (QK_NOPE + QK_ROPE)
    pool_w = w_pool_out.shape[0]
    c0, c2 = q_cols, q_cols + KV_LORA + QK_ROPE
    c3 = c2 + pool_w

    pos = jnp.concatenate([jnp.arange(seq, dtype=F32), jnp.tile(n_past + jnp.arange(t_new, dtype=F32), ROW_TILE // t_new)])
    tab = _rope_table(pos)

    def tab_row(i):
        return jnp.where(i < n_prompt, i % tiles_per_seq, tiles_per_seq)

    w_t = jnp.swapaxes(w_in, 0, 1)
    w_ukv = jnp.concatenate([w_uk.reshape(KV_LORA, -1), w_uv.reshape(KV_LORA, -1)], axis=1).astype(BF16)
    w_ukt = jnp.transpose(w_uk, (1, 2, 0)).astype(BF16)
    w_uv2 = w_uv.reshape(KV_LORA, -1).astype(BF16)

    h = _norm(xp, xs, norm_attn_g)
    q = _q_proj(h, w_t, tab, tab_row)
    ckv_p, ckv_s, kr_p, kr_s, u = _ckv_pool_proj(h, w_t, c0, pool_w, kv_norm_g, tab, tab_row, mp, ms)

    kp, vp = _kv_up(ckv_p, kr_p, w_ukv)
    o_p = _prompt_attn(q, kp, vp, batch, seq)
    qlat, qr = _q_latent(q, w_ukt, mp, ms)
    olat = _sample_attn(qlat, qr, ckv_s, kr_s, cache_c, cache_krt, page_table, t_new)
    o_s = _o_uv(olat, w_uv2)

    wpg = w_pool_group.astype(BF16)
    ps = pool_scale.reshape(1, pool_w)
    y_p = _pool_prompt(u, wpg, ps, mp, seq)
    u3 = u.reshape((mp + ms) // t_new, t_new, pool_w)
    st_pad = jnp.pad(state_pool, ((0, 0), (HALO - POOL_BUF, 0), (0, 0)))
    y_s = _pool_sample(u3, st_pad, wpg, ps, n_past, mp // t_new)

    merged = _merge(h, o_p, o_s, y_p, y_s, w_t, c3, b_gates, w_attn_out.astype(BF16), w_pool_out.astype(BF16))
    x1 = _mm(merged, w_o.astype(BF16), res=(xp, xs), n_prompt=n_prompt, tn=d, name="o_proj")

    w_up_b = w_up.astype(BF16)
    w_down_b = w_down.astype(BF16)
    y_prompt, tails = _ffn(x1, 0, mp, norm_ffn_g, w_up_b, conv_w, conv_b, w_down_b, norm_final_g, seq=seq)
    d_ff = w_down.shape[0]
    y_sample, conv_s = _ffn(x1, mp, ms, norm_ffn_g, w_up_b, conv_w, conv_b, w_down_b, norm_final_g,
                            prev=state_conv, t_new=t_new)

    conv_p = tails[tiles_per_seq - 1::tiles_per_seq, SUBLANES - (CONV_W - 1):]
    pool_p = jnp.stack([u[(b + 1) * seq - POOL_BUF:(b + 1) * seq] for b in range(batch)])
    u_s = u[mp:].reshape(n_seq, t_new, pool_w)
    pool_s = jnp.concatenate([state_pool, u_s], axis=1)[:, -POOL_BUF:]
    return (y_prompt, y_sample, ckv_p.reshape(batch, seq, KV_LORA), kr_p[:, :QK_ROPE].reshape(batch, seq, QK_ROPE),
            ckv_s.reshape(n_seq, t_new, KV_LORA), kr_s[:, :QK_ROPE].reshape(n_seq, t_new, QK_ROPE),
            pool_p, pool_s, conv_p, conv_s)


def kernel(x_prompt, x_sample, cache_kv_latent, cache_k_rope, page_table, state_pool, state_conv, norm_attn_g, w_in, b_gates, kv_norm_g, w_uk, w_uv, w_attn_out, w_pool_group, pool_scale, w_pool_out, w_o, norm_ffn_g, w_up, conv_w, conv_b, w_down, norm_final_g):
    batch, seq, d = x_prompt.shape
    n_seq, t_new, _ = x_sample.shape
    depth = w_in.shape[0]
    assert depth == 1, "the stacked-row pipeline below is written for a single layer"
    n_past = page_table.shape[1] * PAGE_SIZE
    dims = (batch, seq, n_seq, t_new, n_past)
    xp = x_prompt.reshape(batch * seq, d)
    xs = x_sample.reshape(n_seq * t_new, d)
    l = 0
    cache_c = cache_kv_latent.reshape(cache_kv_latent.shape[1:])
    cache_krt = jnp.swapaxes(cache_k_rope.reshape(cache_k_rope.shape[1:]), 1, 2)
    outs = _layer(xp, xs, dims, cache_c, cache_krt, page_table, state_pool[l], state_conv[l],
                  norm_attn_g[l], w_in[l], b_gates[l], kv_norm_g[l], w_uk[l], w_uv[l], w_attn_out[l],
                  w_pool_group[l], pool_scale[l], w_pool_out[l], w_o[l], norm_ffn_g[l], w_up[l], conv_w[l],
                  conv_b[l], w_down[l], norm_final_g)
    y_p, y_s = outs[0].reshape(batch, seq, d), outs[1].reshape(n_seq, t_new, d)
    return (y_p, y_s) + tuple(o[None] for o in outs[2:])
```

```python
import functools
import math

import jax
import jax.numpy as jnp
from jax import lax
from jax.experimental import pallas as pl
from jax.experimental.pallas import tpu as pltpu

F32 = jnp.float32
BF16 = jnp.bfloat16

N_HEADS = 16
QK_NOPE = 128
QK_ROPE = 64
V_DIM = 128
KV_LORA = 512
ROPE_BASE = 10000.0
ATTN_SCALE = (QK_NOPE + QK_ROPE) ** -0.5
EXP2_SCALE = ATTN_SCALE * math.log2(math.e)
PAGE_SIZE = 128
POOL_WINDOWS = (2, 4, 8, 16)
POOL_BUF = max(POOL_WINDOWS) - 1
CONV_W = 3
EPS = 1e-6

LANES = 128
SUBLANES = 8
HEAD_PAD = 2 * LANES
VMEM_LIMIT = 48 * 1024 * 1024

ROW_TILE = 512
FFN_SPLIT = 2


def _cparams(*sem):
    return pltpu.CompilerParams(dimension_semantics=sem, vmem_limit_bytes=VMEM_LIMIT)


def _dot(a, b):
    return jnp.dot(a, b, preferred_element_type=F32)


def _dot_nt(a, b):
    return lax.dot_general(a, b, (((1,), (1,)), ((), ())), preferred_element_type=F32)


def _rms(x, g):
    ms = jnp.mean(x * x, axis=-1, keepdims=True)
    return x * lax.rsqrt(ms + EPS) * g


def _rope128(x, ct, st):
    lane = lax.broadcasted_iota(jnp.int32, x.shape, 1)
    half = QK_ROPE // 2
    other = jnp.where(lane % QK_ROPE < half, pltpu.roll(x, LANES - half, 1), pltpu.roll(x, half, 1))
    return x * ct + other * st


def _softmax_step(s, vals, state):
    bm = jnp.max(s, axis=-1, keepdims=True)
    if state is None:
        m_new = bm
        p = jnp.exp2((s - m_new) * EXP2_SCALE)
        l_new = jnp.sum(p, axis=-1, keepdims=True)
        acc_new = _dot(p.astype(BF16), vals)
    else:
        m, l, acc = state
        m_new = jnp.maximum(m, bm)
        alpha = jnp.exp2((m - m_new) * EXP2_SCALE)
        p = jnp.exp2((s - m_new) * EXP2_SCALE)
        l_new = alpha * l + jnp.sum(p, axis=-1, keepdims=True)
        acc_new = alpha * acc + _dot(p.astype(BF16), vals)
    return m_new, l_new, acc_new


def _softmax_merge(states):
    m_all = states[0][0]
    for m, _, _ in states[1:]:
        m_all = jnp.maximum(m_all, m)
    l_all = acc_all = None
    for m, l, acc in states:
        w = jnp.exp2((m - m_all) * EXP2_SCALE)
        l_all = w * l if l_all is None else l_all + w * l
        acc_all = w * acc if acc_all is None else acc_all + w * acc
    return m_all, l_all, acc_all


def _prompt_rows(n_prompt):
    return lambda i: jnp.minimum(i, n_prompt - 1)


def _sample_rows(n_prompt):
    return lambda i: jnp.maximum(i - n_prompt, 0)


def _norm_kernel(xp_ref, xs_ref, g_ref, o_ref, *, n_prompt):
    i = pl.program_id(0)

    @pl.when(i < n_prompt)
    def _():
        o_ref[...] = _rms(xp_ref[...], g_ref[...]).astype(o_ref.dtype)

    @pl.when(i >= n_prompt)
    def _():
        o_ref[...] = _rms(xs_ref[...], g_ref[...]).astype(o_ref.dtype)


def _norm(xp, xs, g):
    d = xp.shape[1]
    n_prompt, n_sample = xp.shape[0] // ROW_TILE, xs.shape[0] // ROW_TILE
    pr, sr = _prompt_rows(n_prompt), _sample_rows(n_prompt)
    return pl.pallas_call(
        functools.partial(_norm_kernel, n_prompt=n_prompt),
        grid=(n_prompt + n_sample,),
        in_specs=[
            pl.BlockSpec((ROW_TILE, d), lambda i: (pr(i), 0)),
            pl.BlockSpec((ROW_TILE, d), lambda i: (sr(i), 0)),
            pl.BlockSpec((1, d), lambda i: (0, 0)),
        ],
        out_specs=pl.BlockSpec((ROW_TILE, d), lambda i: (i, 0)),
        out_shape=jax.ShapeDtypeStruct((xp.shape[0] + xs.shape[0], d), BF16),
        compiler_params=_cparams("arbitrary"),
        name="rmsnorm",
    )(xp, xs, g.reshape(1, d))


def _mm_kernel(*refs, n_prompt, has_res):
    h_ref, w_ref = refs[0], refs[1]
    o_ref = refs[-1]
    r = _dot(h_ref[...], w_ref[...])
    if has_res:
        i = pl.program_id(1)

        @pl.when(i < n_prompt)
        def _():
            o_ref[...] = refs[2][...] + r

        @pl.when(i >= n_prompt)
        def _():
            o_ref[...] = refs[3][...] + r
    else:
        o_ref[...] = r.astype(o_ref.dtype)


def _mm(h, w, *, res=None, n_prompt=None, out_dtype=F32, tn=512, name="mm"):
    m, kd = h.shape
    n = w.shape[1]
    ins = [h, w]
    specs = [pl.BlockSpec((ROW_TILE, kd), lambda j, i: (i, 0)), pl.BlockSpec((kd, tn), lambda j, i: (0, j))]
    if res is not None:
        pr, sr = _prompt_rows(n_prompt), _sample_rows(n_prompt)
        ins += list(res)
        specs += [pl.BlockSpec((ROW_TILE, tn), lambda j, i: (pr(i), j)), pl.BlockSpec((ROW_TILE, tn), lambda j, i: (sr(i), j))]
    return pl.pallas_call(
        functools.partial(_mm_kernel, n_prompt=n_prompt, has_res=res is not None),
        grid=(n // tn, m // ROW_TILE),
        in_specs=specs,
        out_specs=pl.BlockSpec((ROW_TILE, tn), lambda j, i: (i, j)),
        out_shape=jax.ShapeDtypeStruct((m, n), out_dtype),
        compiler_params=_cparams("arbitrary", "arbitrary"),
        name=name,
    )(*ins)


def _q_kernel(h_ref, wt_ref, tab_ref, o_ref, wb_ref):
    hd = QK_NOPE + QK_ROPE
    pair = 2 * hd
    n_pairs = wb_ref.shape[0] // pair

    @pl.when(pl.program_id(1) == 0)
    def _():
        for p in range(n_pairs):
            a0, b0 = 2 * p * hd, (2 * p + 1) * hd
            w0 = p * pair
            wb_ref[w0:w0 + QK_NOPE, :] = wt_ref[a0:a0 + QK_NOPE, :].astype(BF16)
            wb_ref[w0 + QK_NOPE:w0 + 2 * QK_NOPE, :] = wt_ref[b0:b0 + QK_NOPE, :].astype(BF16)
            wb_ref[w0 + 2 * QK_NOPE:w0 + 2 * QK_NOPE + QK_ROPE, :] = wt_ref[a0 + QK_NOPE:a0 + hd, :].astype(BF16)
            wb_ref[w0 + 2 * QK_NOPE + QK_ROPE:w0 + pair, :] = wt_ref[b0 + QK_NOPE:b0 + hd, :].astype(BF16)

    r = _dot_nt(h_ref[...], wb_ref[...])
    ct, st = tab_ref[:, :LANES], tab_ref[:, LANES:]
    lane = lax.broadcasted_iota(jnp.int32, (r.shape[0], LANES), 1)
    for p in range(n_pairs):
        r0 = p * pair
        oa, ob = 2 * p * HEAD_PAD, (2 * p + 1) * HEAD_PAD
        o_ref[:, oa:oa + LANES] = r[:, r0:r0 + LANES].astype(o_ref.dtype)
        o_ref[:, ob:ob + LANES] = r[:, r0 + LANES:r0 + 2 * LANES].astype(o_ref.dtype)
        rp = _rope128(r[:, r0 + 2 * LANES:r0 + 3 * LANES], ct, st)
        o_ref[:, oa + LANES:oa + HEAD_PAD] = jnp.where(lane < QK_ROPE, rp, 0.0).astype(o_ref.dtype)
        o_ref[:, ob + LANES:ob + HEAD_PAD] = jnp.where(lane < QK_ROPE, pltpu.roll(rp, QK_ROPE, 1), 0.0).astype(o_ref.dtype)


def _q_proj(h, w_t, tab, tab_row, heads_per_block=8):
    m, kd = h.shape
    hd = QK_NOPE + QK_ROPE
    tn = heads_per_block * HEAD_PAD
    return pl.pallas_call(
        _q_kernel,
        grid=(N_HEADS // heads_per_block, m // ROW_TILE),
        in_specs=[
            pl.BlockSpec((ROW_TILE, kd), lambda j, i: (i, 0)),
            pl.BlockSpec((heads_per_block * hd, kd), lambda j, i: (j, 0)),
            pl.BlockSpec((ROW_TILE, 2 * LANES), lambda j, i: (tab_row(i), 0)),
        ],
        out_specs=pl.BlockSpec((ROW_TILE, tn), lambda j, i: (i, j)),
        out_shape=jax.ShapeDtypeStruct((m, N_HEADS * HEAD_PAD), BF16),
        scratch_shapes=[pltpu.VMEM((heads_per_block * hd, kd), BF16)],
        compiler_params=_cparams("arbitrary", "arbitrary"),
        name="q_proj",
    )(h, w_t, tab)


def _ckv_kernel(h_ref, wt_hbm, g_ref, tab_ref, cp_ref, cs_ref, krp_ref, krs_ref, u_ref, wf_ref, wb_ref, sem,
                *, row0, n_prompt):
    i = pl.program_id(0)
    n_ck = KV_LORA + QK_ROPE
    n_ck_pad = KV_LORA + LANES

    @pl.when(i == 0)
    def _():
        cp = pltpu.make_async_copy(wt_hbm.at[pl.ds(row0, wf_ref.shape[0])], wf_ref, sem)
        cp.start()
        cp.wait()
        wb_ref[0:n_ck, :] = wf_ref[0:n_ck, :].astype(BF16)
        wb_ref[n_ck:n_ck_pad, :] = jnp.zeros((n_ck_pad - n_ck, wb_ref.shape[1]), BF16)
        wb_ref[n_ck_pad:, :] = wf_ref[n_ck:, :].astype(BF16)

    r = _dot_nt(h_ref[...], wb_ref[...])
    c = _rms(r[:, :KV_LORA], g_ref[...])
    kr = _rope128(r[:, KV_LORA:n_ck_pad], tab_ref[:, :LANES], tab_ref[:, LANES:])
    u_ref[...] = r[:, n_ck_pad:]

    @pl.when(i < n_prompt)
    def _():
        cp_ref[...] = c
        krp_ref[...] = kr

    @pl.when(i >= n_prompt)
    def _():
        cs_ref[...] = c
        krs_ref[...] = kr


def _ckv_pool_proj(h, w_t, row0, pool_w, g, tab, tab_row, mp, ms):
    m, kd = h.shape
    n_rows = KV_LORA + QK_ROPE + pool_w
    n_pad = KV_LORA + LANES + pool_w
    n_prompt = mp // ROW_TILE
    pr, sr = _prompt_rows(n_prompt), _sample_rows(n_prompt)
    return pl.pallas_call(
        functools.partial(_ckv_kernel, row0=row0, n_prompt=n_prompt),
        grid=(m // ROW_TILE,),
        in_specs=[
            pl.BlockSpec((ROW_TILE, kd), lambda i: (i, 0)),
            pl.BlockSpec(memory_space=pl.ANY),
            pl.BlockSpec((1, KV_LORA), lambda i: (0, 0)),
            pl.BlockSpec((ROW_TILE, 2 * LANES), lambda i: (tab_row(i), 0)),
        ],
        out_specs=[
            pl.BlockSpec((ROW_TILE, KV_LORA), lambda i: (pr(i), 0)),
            pl.BlockSpec((ROW_TILE, KV_LORA), lambda i: (sr(i), 0)),
            pl.BlockSpec((ROW_TILE, LANES), lambda i: (pr(i), 0)),
            pl.BlockSpec((ROW_TILE, LANES), lambda i: (sr(i), 0)),
            pl.BlockSpec((ROW_TILE, pool_w), lambda i: (i, 0)),
        ],
        out_shape=[
            jax.ShapeDtypeStruct((mp, KV_LORA), F32), jax.ShapeDtypeStruct((ms, KV_LORA), F32),
            jax.ShapeDtypeStruct((mp, LANES), F32), jax.ShapeDtypeStruct((ms, LANES), F32),
            jax.ShapeDtypeStruct((m, pool_w), F32),
        ],
        scratch_shapes=[pltpu.VMEM((n_rows, kd), F32), pltpu.VMEM((n_pad, kd), BF16), pltpu.SemaphoreType.DMA(())],
        compiler_params=_cparams("arbitrary"),
        name="ckv_pool_proj",
    )(h, w_t, g.reshape(1, KV_LORA), tab)


def _kvup_kernel(c_ref, kr_ref, w_ref, k_ref, v_ref):
    r = _dot(c_ref[...].astype(BF16), w_ref[...])
    nk = N_HEADS * QK_NOPE
    v_ref[...] = r[:, nk:].astype(v_ref.dtype)
    krb = kr_ref[...].astype(k_ref.dtype)
    for hh in range(N_HEADS):
        k_ref[:, hh * HEAD_PAD:hh * HEAD_PAD + LANES] = r[:, hh * QK_NOPE:(hh + 1) * QK_NOPE].astype(k_ref.dtype)
        k_ref[:, hh * HEAD_PAD + LANES:(hh + 1) * HEAD_PAD] = krb


def _kv_up(ckv, kr, w_ukv, tm=512):
    rows = ckv.shape[0]
    n = w_ukv.shape[1]
    return pl.pallas_call(
        _kvup_kernel,
        grid=(rows // tm,),
        in_specs=[
            pl.BlockSpec((tm, KV_LORA), lambda i: (i, 0)),
            pl.BlockSpec((tm, LANES), lambda i: (i, 0)),
            pl.BlockSpec((KV_LORA, n), lambda i: (0, 0)),
        ],
        out_specs=[pl.BlockSpec((tm, N_HEADS * HEAD_PAD), lambda i: (i, 0)), pl.BlockSpec((tm, N_HEADS * V_DIM), lambda i: (i, 0))],
        out_shape=[jax.ShapeDtypeStruct((rows, N_HEADS * HEAD_PAD), BF16), jax.ShapeDtypeStruct((rows, N_HEADS * V_DIM), BF16)],
        compiler_params=_cparams("arbitrary"),
        name="kv_up",
    )(ckv, kr, w_ukv)


def _pattn_kernel(q_ref, k_ref, v_ref, o_ref, *, tq):
    nq = q_ref.shape[0] // tq
    row = lax.broadcasted_iota(jnp.int32, (tq, tq), 0)
    col = lax.broadcasted_iota(jnp.int32, (tq, tq), 1)
    for qi in range(nq):
        q = q_ref[qi * tq:(qi + 1) * tq, :]
        state = None
        for ki in range(qi + 1):
            s = _dot_nt(q, k_ref[ki * tq:(ki + 1) * tq, :])
            if ki == qi:
                s = jnp.where(col <= row, s, -jnp.inf)
            state = _softmax_step(s, v_ref[ki * tq:(ki + 1) * tq, :], state)
        _, l, acc = state
        o_ref[qi * tq:(qi + 1) * tq, :] = (acc / l).astype(o_ref.dtype)


def _prompt_attn(q, k, v, batch, seq, tq=512):
    return pl.pallas_call(
        functools.partial(_pattn_kernel, tq=tq),
        grid=(batch, N_HEADS),
        in_specs=[
            pl.BlockSpec((seq, HEAD_PAD), lambda b, h: (b, h)),
            pl.BlockSpec((seq, HEAD_PAD), lambda b, h: (b, h)),
            pl.BlockSpec((seq, V_DIM), lambda b, h: (b, h)),
        ],
        out_specs=pl.BlockSpec((seq, V_DIM), lambda b, h: (b, h)),
        out_shape=jax.ShapeDtypeStruct((batch * seq, N_HEADS * V_DIM), BF16),
        compiler_params=_cparams("arbitrary", "arbitrary"),
        name="prompt_attn",
    )(q, k, v)


def _qlat_kernel(q_ref, w_ref, ql_ref, qr_ref):
    q = q_ref[...]
    ql_ref[...] = _dot(q[:, :QK_NOPE], w_ref[...])
    qr_ref[...] = q[:, LANES:].astype(F32)


def _q_latent(q, w_ukt, row0, rows):
    blk = row0 // rows
    return pl.pallas_call(
        _qlat_kernel,
        grid=(N_HEADS,),
        in_specs=[
            pl.BlockSpec((rows, HEAD_PAD), lambda h: (blk, h)),
            pl.BlockSpec((None, QK_NOPE, KV_LORA), lambda h: (h, 0, 0)),
        ],
        out_specs=[
            pl.BlockSpec((None, rows, KV_LORA), lambda h: (h, 0, 0)),
            pl.BlockSpec((None, rows, LANES), lambda h: (h, 0, 0)),
        ],
        out_shape=[jax.ShapeDtypeStruct((N_HEADS, rows, KV_LORA), F32), jax.ShapeDtypeStruct((N_HEADS, rows, LANES), F32)],
        compiler_params=_cparams("arbitrary"),
        name="q_latent",
    )(q, w_ukt)


KCAT = KV_LORA + LANES


def _sattn_kernel(pt_ref, ql_ref, qr_ref, cn_ref, krn_ref, cache_c, cache_krt, o_ref,
                  qcat, ncat, cbuf, krbuf, kcb, krtb, sem, *, cp, n_chunks, t_new):
    seq = pl.program_id(0)
    n_seq = pl.num_programs(0)
    rows = N_HEADS * t_new
    n_slots = cbuf.shape[0]
    assert n_chunks % n_slots == 0

    def copies(sq, chunk):
        slot = chunk % n_slots
        out = []
        for k in range(cp):
            page = pt_ref[sq, chunk * cp + k]
            out.append(pltpu.make_async_copy(cache_c.at[page], cbuf.at[slot, k], sem.at[slot]))
            out.append(pltpu.make_async_copy(cache_krt.at[page], krbuf.at[slot, k], sem.at[slot]))
        return out

    def start(sq, chunk):
        for n, c in enumerate(copies(sq, chunk)):
            c.start(priority=(n // 2) % 2)

    def fetch(chunk):
        slot, half = chunk % n_slots, chunk % 2
        for c in copies(seq, chunk):
            c.wait()
        for k in range(cp):
            kcb[half, k * PAGE_SIZE:(k + 1) * PAGE_SIZE, :] = cbuf[slot, k].astype(BF16)
            krtb[half, :, k * PAGE_SIZE:(k + 1) * PAGE_SIZE] = krbuf[slot, k].astype(BF16)
        ahead = chunk + n_slots
        if ahead < n_chunks:
            start(seq, ahead)
        else:
            @pl.when(seq + 1 < n_seq)
            def _():
                start(seq + 1, ahead - n_chunks)
        return _dot_nt(qcat[:, :KV_LORA], kcb[half]) + _dot(qcat[:, KV_LORA:KV_LORA + QK_ROPE], krtb[half])

    @pl.when(seq == 0)
    def _():
        for chunk in range(n_slots):
            start(seq, chunk)

    qcat[:, :KV_LORA] = ql_ref[...].reshape(rows, KV_LORA).astype(BF16)
    qcat[:, KV_LORA:] = qr_ref[...].reshape(rows, LANES).astype(BF16)
    ncat[...] = jnp.zeros(ncat.shape, BF16)
    ncat[0:t_new, :KV_LORA] = cn_ref[...].astype(BF16)
    ncat[0:t_new, KV_LORA:] = krn_ref[...].astype(BF16)
    s_new = _dot_nt(qcat[...], ncat[...])
    row = lax.broadcasted_iota(jnp.int32, s_new.shape, 0)
    col = lax.broadcasted_iota(jnp.int32, s_new.shape, 1)
    s_new = jnp.where(col <= row % t_new, s_new, -jnp.inf)
    state = _softmax_step(s_new, ncat[:, :KV_LORA], None)

    s_cur = fetch(0)
    for c in range(n_chunks):
        if c + 1 < n_chunks:
            s_next = fetch(c + 1)
        state = _softmax_merge([state, _softmax_step(s_cur, kcb[c % 2], None)])
        if c + 1 < n_chunks:
            s_cur = s_next

    _, l, acc = state
    o_ref[...] = (acc / l).reshape(o_ref.shape)


def _sample_attn(qlat, qr, ckv_s, kr_s, cache_c, cache_krt, page_table, t_new, cp=8, n_slots=4):
    n_seq, n_pages = page_table.shape
    rows = N_HEADS * t_new
    grid_spec = pltpu.PrefetchScalarGridSpec(
        num_scalar_prefetch=1,
        grid=(n_seq,),
        in_specs=[
            pl.BlockSpec((N_HEADS, t_new, KV_LORA), lambda s, pt: (0, s, 0)),
            pl.BlockSpec((N_HEADS, t_new, LANES), lambda s, pt: (0, s, 0)),
            pl.BlockSpec((t_new, KV_LORA), lambda s, pt: (s, 0)),
            pl.BlockSpec((t_new, LANES), lambda s, pt: (s, 0)),
            pl.BlockSpec(memory_space=pl.ANY),
            pl.BlockSpec(memory_space=pl.ANY),
        ],
        out_specs=pl.BlockSpec((N_HEADS, t_new, KV_LORA), lambda s, pt: (0, s, 0)),
        scratch_shapes=[
            pltpu.VMEM((rows, KCAT), BF16),
            pltpu.VMEM((PAGE_SIZE, KCAT), BF16),
            pltpu.VMEM((n_slots, cp, PAGE_SIZE, KV_LORA), F32),
            pltpu.VMEM((n_slots, cp, QK_ROPE, PAGE_SIZE), F32),
            pltpu.VMEM((2, cp * PAGE_SIZE, KV_LORA), BF16),
            pltpu.VMEM((2, QK_ROPE, cp * PAGE_SIZE), BF16),
            pltpu.SemaphoreType.DMA((n_slots,)),
        ],
    )
    return pl.pallas_call(
        functools.partial(_sattn_kernel, cp=cp, n_chunks=n_pages // cp, t_new=t_new),
        grid_spec=grid_spec,
        out_shape=jax.ShapeDtypeStruct((N_HEADS, n_seq * t_new, KV_LORA), F32),
        compiler_params=_cparams("arbitrary"),
        name="sample_attn",
    )(page_table, qlat, qr, ckv_s, kr_s, cache_c, cache_krt)


def _ouv_kernel(o_ref, w_ref, out_ref):
    out_ref[...] = _dot(o_ref[...].astype(BF16), w_ref[...]).astype(out_ref.dtype)


def _o_uv(olat, w_uv2):
    _, rows, _ = olat.shape
    return pl.pallas_call(
        _ouv_kernel,
        grid=(N_HEADS,),
        in_specs=[
            pl.BlockSpec((None, rows, KV_LORA), lambda h: (h, 0, 0)),
            pl.BlockSpec((KV_LORA, V_DIM), lambda h: (0, h)),
        ],
        out_specs=pl.BlockSpec((rows, V_DIM), lambda h: (0, h)),
        out_shape=jax.ShapeDtypeStruct((rows, N_HEADS * V_DIM), BF16),
        compiler_params=_cparams("arbitrary"),
        name="o_uv",
    )(olat, w_uv2)


HALO = 2 * SUBLANES


def _pool_groups(load, u_of, cnt_of, wg_ref, ps_ref, store):
    group = wg_ref.shape[1]
    for g, w in enumerate(POOL_WINDOWS):
        sl = slice(g * group, (g + 1) * group)
        ws = load(0, sl)
        for k in range(1, w):
            ws = ws + load(k, sl)
        pooled = ws / cnt_of(w) - u_of(sl)
        pooled = pooled.reshape(-1, group).astype(BF16)
        store(sl, (_dot(pooled, wg_ref[g]) * ps_ref[:, sl]))


def _pool_prompt_kernel(u_ref, wg_ref, ps_ref, y_ref, ext_ref, *, tiles_per_seq):
    i = pl.program_id(0)
    tm = u_ref.shape[0]
    it = i % tiles_per_seq

    @pl.when(it == 0)
    def _():
        ext_ref[0:HALO, :] = jnp.zeros((HALO, ext_ref.shape[1]), F32)

    @pl.when(it != 0)
    def _():
        ext_ref[0:HALO, :] = ext_ref[tm:tm + HALO, :]

    ext_ref[HALO:HALO + tm, :] = u_ref[...]
    pos = it * tm + lax.broadcasted_iota(jnp.int32, (tm, 1), 0)

    def store(sl, val):
        y_ref[:, sl] = val.astype(y_ref.dtype)

    _pool_groups(
        lambda k, sl: ext_ref[HALO - k:HALO - k + tm, sl],
        lambda sl: u_ref[:, sl],
        lambda w: jnp.minimum(pos + 1, w).astype(F32),
        wg_ref, ps_ref, store)


def _pool_prompt(u, wg, ps, rows, seq):
    width = wg.shape[0] * wg.shape[1]
    tm = ROW_TILE
    return pl.pallas_call(
        functools.partial(_pool_prompt_kernel, tiles_per_seq=seq // tm),
        grid=(rows // tm,),
        in_specs=[
            pl.BlockSpec((tm, width), lambda i: (i, 0)),
            pl.BlockSpec(wg.shape, lambda i: (0, 0, 0)),
            pl.BlockSpec((1, width), lambda i: (0, 0)),
        ],
        out_specs=pl.BlockSpec((tm, width), lambda i: (i, 0)),
        out_shape=jax.ShapeDtypeStruct((rows, width), BF16),
        scratch_shapes=[pltpu.VMEM((HALO + tm, width), F32)],
        compiler_params=_cparams("arbitrary"),
        name="pool_prompt",
    )(u, wg, ps)


def _pool_sample_kernel(u_ref, st_ref, wg_ref, ps_ref, y_ref, ext_ref, *, pos0):
    t_new = u_ref.shape[1]
    ext_ref[:, 0:HALO, :] = st_ref[...]
    ext_ref[:, HALO:HALO + t_new, :] = u_ref[...]
    pos = pos0 + lax.broadcasted_iota(jnp.int32, (1, t_new, 1), 1)

    def store(sl, val):
        y_ref[:, sl] = val.astype(y_ref.dtype)

    _pool_groups(
        lambda k, sl: ext_ref[:, HALO - k:HALO - k + t_new, sl],
        lambda sl: u_ref[:, :, sl],
        lambda w: jnp.minimum(pos + 1, w).astype(F32),
        wg_ref, ps_ref, store)


def _pool_sample(u3, st, wg, ps, pos0, seq0, ns=64):
    n_seq = st.shape[0]
    _, t_new, width = u3.shape
    blk0 = seq0 // ns
    return pl.pallas_call(
        functools.partial(_pool_sample_kernel, pos0=pos0),
        grid=(n_seq // ns,),
        in_specs=[
            pl.BlockSpec((ns, t_new, width), lambda i: (blk0 + i, 0, 0)),
            pl.BlockSpec((ns, HALO, width), lambda i: (i, 0, 0)),
            pl.BlockSpec(wg.shape, lambda i: (0, 0, 0)),
            pl.BlockSpec((1, width), lambda i: (0, 0)),
        ],
        out_specs=pl.BlockSpec((ns * t_new, width), lambda i: (i, 0)),
        out_shape=jax.ShapeDtypeStruct((n_seq * t_new, width), BF16),
        scratch_shapes=[pltpu.VMEM((ns, HALO + t_new, width), F32)],
        compiler_params=_cparams("arbitrary"),
        name="pool_sample",
    )(u3, st, wg, ps)


def _merge_kernel(h_ref, oap_ref, oas_ref, ypp_ref, yps_ref, wt_hbm, ba_ref, bb_ref, wa_ref, wp_ref, o_ref,
                  wf_ref, wb_ref, sem, *, row0, n_blocks, n_prompt):
    j, i = pl.program_id(0), pl.program_id(1)
    tn = o_ref.shape[1]
    n = tn * n_blocks

    @pl.when(i == 0)
    def _():
        cps = [pltpu.make_async_copy(wt_hbm.at[pl.ds(row0 + br * n + j * tn, tn)], wf_ref.at[br], sem.at[br])
               for br in range(2)]
        for cp in cps:
            cp.start()
        for cp in cps:
            cp.wait()
        wb_ref[...] = wf_ref[...].astype(BF16)

    h = h_ref[...]
    ga = jax.nn.sigmoid(_dot_nt(h, wb_ref[0]) + ba_ref[...])
    gb = jax.nn.sigmoid(_dot_nt(h, wb_ref[1]) + bb_ref[...])

    def finish(oa_ref, yp_ref):
        a = _dot(oa_ref[...], wa_ref[...])
        p = _dot(yp_ref[...], wp_ref[...])
        o_ref[...] = (ga * a + gb * p).astype(o_ref.dtype)

    @pl.when(i < n_prompt)
    def _():
        finish(oap_ref, ypp_ref)

    @pl.when(i >= n_prompt)
    def _():
        finish(oas_ref, yps_ref)


def _merge(h, oa_p, oa_s, yp_p, yp_s, w_t, row0, b_gates, wa, wp, tn=512):
    m, d = h.shape
    ka, kp = wa.shape[0], wp.shape[0]
    n = wa.shape[1]
    nb = n // tn
    n_prompt = oa_p.shape[0] // ROW_TILE
    pr, sr = _prompt_rows(n_prompt), _sample_rows(n_prompt)
    bg = b_gates.reshape(1, -1)
    return pl.pallas_call(
        functools.partial(_merge_kernel, row0=row0, n_blocks=nb, n_prompt=n_prompt),
        grid=(nb, m // ROW_TILE),
        in_specs=[
            pl.BlockSpec((ROW_TILE, d), lambda j, i: (i, 0)),
            pl.BlockSpec((ROW_TILE, ka), lambda j, i: (pr(i), 0)),
            pl.BlockSpec((ROW_TILE, ka), lambda j, i: (sr(i), 0)),
            pl.BlockSpec((ROW_TILE, kp), lambda j, i: (pr(i), 0)),
            pl.BlockSpec((ROW_TILE, kp), lambda j, i: (sr(i), 0)),
            pl.BlockSpec(memory_space=pl.ANY),
            pl.BlockSpec((1, tn), lambda j, i: (0, j)),
            pl.BlockSpec((1, tn), lambda j, i: (0, nb + j)),
            pl.BlockSpec((ka, tn), lambda j, i: (0, j)),
            pl.BlockSpec((kp, tn), lambda j, i: (0, j)),
        ],
        out_specs=pl.BlockSpec((ROW_TILE, tn), lambda j, i: (i, j)),
        out_shape=jax.ShapeDtypeStruct((m, n), BF16),
        scratch_shapes=[pltpu.VMEM((2, tn, d), F32), pltpu.VMEM((2, tn, d), BF16), pltpu.SemaphoreType.DMA((2,))],
        compiler_params=_cparams("arbitrary", "arbitrary"),
        name="merge",
    )(h, oa_p, oa_s, yp_p, yp_s, w_t, bg, bg, wa, wp)


def _ffn_kernel(*refs, sample, tiles_per_seq, t_new):
    if sample:
        (x_ref, g_ref, wa_ref, wv_ref, cw_ref, cb_ref, wd_ref, gf_ref, st_ref,
         y_ref, cs_ref, hn_ref, acc_ref) = refs
        assert CONV_W == 3
    else:
        (x_ref, g_ref, wa_ref, wv_ref, cw_ref, cb_ref, wd_ref, gf_ref,
         y_ref, tail_ref, hn_ref, acc_ref, carry_ref) = refs
    i, f = pl.program_id(0), pl.program_id(1)
    tm = x_ref.shape[0]

    @pl.when(f == 0)
    def _():
        hn_ref[...] = _rms(x_ref[...], g_ref[...]).astype(hn_ref.dtype)
        acc_ref[...] = jnp.zeros(acc_ref.shape, F32)

    if not sample:
        @pl.when((i == 0) & (f == 0))
        def _():
            carry_ref[...] = jnp.zeros(carry_ref.shape, F32)

    hn = hn_ref[...]
    tf = wa_ref.shape[1]
    sw = tf // FFN_SPLIT
    cols = [slice(k * sw, (k + 1) * sw) for k in range(FFN_SPLIT)]
    ups = [(_dot(hn, wa_ref[:, cs]), _dot(hn, wv_ref[:, cs])) for cs in cols]
    hids = []
    for cs, (a, v) in zip(cols, ups):
        r1, r2 = pltpu.roll(a, 1, 0), pltpu.roll(a, 2, 0)
        if sample:
            ns = tm // t_new
            a, v, r1, r2 = (z.reshape(ns, t_new, sw) for z in (a, v, r1, r2))
            t = lax.broadcasted_iota(jnp.int32, (1, t_new, 1), 1)
            st0, st1 = st_ref[:, 0:1, cs], st_ref[:, 1:2, cs]
            s1 = jnp.where(t >= 1, r1, st1)
            s2 = jnp.where(t >= 2, r2, jnp.where(t == 0, st0, st1))
            cs_ref[:, :, cs] = a[:, t_new - (CONV_W - 1):, :]
        else:
            first = i % tiles_per_seq == 0
            prev = jnp.where(first, 0.0, carry_ref[f, :, cs])
            row8 = lax.broadcasted_iota(jnp.int32, (SUBLANES, 1), 0)
            h1 = jnp.where(row8 < 1, pltpu.roll(prev, 1, 0), r1[:SUBLANES])
            h2 = jnp.where(row8 < 2, pltpu.roll(prev, 2, 0), r2[:SUBLANES])
            s1 = jnp.concatenate([h1, r1[SUBLANES:]], axis=0)
            s2 = jnp.concatenate([h2, r2[SUBLANES:]], axis=0)
            tail = a[tm - SUBLANES:, :]
            carry_ref[f, :, cs] = tail
            tail_ref[:, cs] = tail
        conv = cb_ref[:, cs] + cw_ref[0:1, cs] * s2
        conv = conv + cw_ref[1:2, cs] * s1
        conv = conv + cw_ref[2:3, cs] * a
        half = 0.5 * conv
        hids.append(((half * v) * (jnp.tanh(half) + 1.0)).reshape(tm, sw).astype(BF16))
    acc = acc_ref[...]
    for cs, hid in zip(cols, hids):
        acc = acc + _dot(hid, wd_ref[cs, :])
    acc_ref[...] = acc

    @pl.when(f == pl.num_programs(1) - 1)
    def _():
        x2 = x_ref[...] + acc_ref[...]
        y_ref[...] = _rms(x2, gf_ref[...])


def _ffn(x, row0, rows, g, w_up, conv_w, conv_b, w_down, gf, *, seq=None, prev=None, t_new=None, tf=512):
    d = x.shape[1]
    d_ff = w_down.shape[0]
    nf = d_ff // tf
    tm = ROW_TILE
    blk0 = row0 // tm
    sample = prev is not None
    in_specs = [
        pl.BlockSpec((tm, d), lambda i, f: (blk0 + i, 0)),
        pl.BlockSpec((1, d), lambda i, f: (0, 0)),
        pl.BlockSpec((d, tf), lambda i, f: (0, f)),
        pl.BlockSpec((d, tf), lambda i, f: (0, nf + f)),
        pl.BlockSpec((CONV_W, tf), lambda i, f: (0, f)),
        pl.BlockSpec((1, tf), lambda i, f: (0, f)),
        pl.BlockSpec((tf, d), lambda i, f: (f, 0)),
        pl.BlockSpec((1, d), lambda i, f: (0, 0)),
    ]
    ins = [x, g.reshape(1, d), w_up, w_up, conv_w, conv_b.reshape(1, d_ff), w_down, gf.reshape(1, d)]
    scratch = [pltpu.VMEM((tm, d), BF16), pltpu.VMEM((tm, d), F32)]
    y_spec = pl.BlockSpec((tm, d), lambda i, f: (i, 0))
    y_shape = jax.ShapeDtypeStruct((rows, d), F32)
    if sample:
        st_spec = pl.BlockSpec((tm // t_new, CONV_W - 1, tf), lambda i, f: (i, 0, f))
        in_specs.append(st_spec)
        ins.append(prev)
        out_specs = [y_spec, st_spec]
        out_shape = [y_shape, jax.ShapeDtypeStruct(prev.shape, F32)]
        tiles_per_seq = None
    else:
        out_specs = [y_spec, pl.BlockSpec((None, SUBLANES, tf), lambda i, f: (i, 0, f))]
        out_shape = [y_shape, jax.ShapeDtypeStruct((rows // tm, SUBLANES, d_ff), F32)]
        scratch.append(pltpu.VMEM((nf, SUBLANES, tf), F32))
        tiles_per_seq = seq // tm
    return pl.pallas_call(
        functools.partial(_ffn_kernel, sample=sample, tiles_per_seq=tiles_per_seq, t_new=t_new),
        grid=(rows // tm, nf),
        in_specs=in_specs,
        out_specs=out_specs,
        out_shape=out_shape,
        scratch_shapes=scratch,
        compiler_params=_cparams("arbitrary", "arbitrary"),
        name="ffn_sample" if sample else "ffn_prompt",
    )(*ins)


def _rope_table(pos):
    half = QK_ROPE // 2
    inv = ROPE_BASE ** (-jnp.arange(half, dtype=F32) / half)
    ang = pos[:, None] * inv[None, :]
    c, s = jnp.cos(ang), jnp.sin(ang)
    return jnp.concatenate([c, c, c, c, -s, s, -s, s], axis=1)


def _layer(xp, xs, dims, cache_c, cache_krt, page_table, state_pool, state_conv,
           norm_attn_g, w_in, b_gates, kv_norm_g, w_uk, w_uv, w_attn_out, w_pool_group, pool_scale,
           w_pool_out, w_o, norm_ffn_g, w_up, conv_w, conv_b, w_down, norm_final_g):
    batch, seq, n_seq, t_new, n_past = dims
    mp, ms = xp.shape[0], xs.shape[0]
    d = xp.shape[1]
    n_prompt = mp // ROW_TILE
    tiles_per_seq = seq // ROW_TILE
    q_cols = N_HEADS * (QK_NOPE + QK_ROPE)
    pool_w = w_pool_out.shape[0]
    c0, c2 = q_cols, q_cols + KV_LORA + QK_ROPE
    c3 = c2 + pool_w

    pos = jnp.concatenate([jnp.arange(seq, dtype=F32), jnp.tile(n_past + jnp.arange(t_new, dtype=F32), ROW_TILE // t_new)])
    tab = _rope_table(pos)

    def tab_row(i):
        return jnp.where(i < n_prompt, i % tiles_per_seq, tiles_per_seq)

    w_t = jnp.swapaxes(w_in, 0, 1)
    w_ukv = jnp.concatenate([w_uk.reshape(KV_LORA, -1), w_uv.reshape(KV_LORA, -1)], axis=1).astype(BF16)
    w_ukt = jnp.transpose(w_uk, (1, 2, 0)).astype(BF16)
    w_uv2 = w_uv.reshape(KV_LORA, -1).astype(BF16)

    h = _norm(xp, xs, norm_attn_g)
    q = _q_proj(h, w_t, tab, tab_row)
    ckv_p, ckv_s, kr_p, kr_s, u = _ckv_pool_proj(h, w_t, c0, pool_w, kv_norm_g, tab, tab_row, mp, ms)

    kp, vp = _kv_up(ckv_p, kr_p, w_ukv)
    o_p = _prompt_attn(q, kp, vp, batch, seq)
    qlat, qr = _q_latent(q, w_ukt, mp, ms)
    olat = _sample_attn(qlat, qr, ckv_s, kr_s, cache_c, cache_krt, page_table, t_new)
    o_s = _o_uv(olat, w_uv2)

    wpg = w_pool_group.astype(BF16)
    ps = pool_scale.reshape(1, pool_w)
    y_p = _pool_prompt(u, wpg, ps, mp, seq)
    u3 = u.reshape((mp + ms) // t_new, t_new, pool_w)
    st_pad = jnp.pad(state_pool, ((0, 0), (HALO - POOL_BUF, 0), (0, 0)))
    y_s = _pool_sample(u3, st_pad, wpg, ps, n_past, mp // t_new)

    merged = _merge(h, o_p, o_s, y_p, y_s, w_t, c3, b_gates, w_attn_out.astype(BF16), w_pool_out.astype(BF16))
    x1 = _mm(merged, w_o.astype(BF16), res=(xp, xs), n_prompt=n_prompt, tn=d, name="o_proj")

    w_up_b = w_up.astype(BF16)
    w_down_b = w_down.astype(BF16)
    y_prompt, tails = _ffn(x1, 0, mp, norm_ffn_g, w_up_b, conv_w, conv_b, w_down_b, norm_final_g, seq=seq)
    y_sample, conv_s = _ffn(x1, mp, ms, norm_ffn_g, w_up_b, conv_w, conv_b, w_down_b, norm_final_g,
                            prev=state_conv, t_new=t_new)

    conv_p = tails[tiles_per_seq - 1::tiles_per_seq, SUBLANES - (CONV_W - 1):]
    pool_p = jnp.stack([u[(b + 1) * seq - POOL_BUF:(b + 1) * seq] for b in range(batch)])
    u_s = u[mp:].reshape(n_seq, t_new, pool_w)
    pool_s = jnp.concatenate([state_pool, u_s], axis=1)[:, -POOL_BUF:]
    return (y_prompt, y_sample, ckv_p.reshape(batch, seq, KV_LORA), kr_p[:, :QK_ROPE].reshape(batch, seq, QK_ROPE),
            ckv_s.reshape(n_seq, t_new, KV_LORA), kr_s[:, :QK_ROPE].reshape(n_seq, t_new, QK_ROPE),
            pool_p, pool_s, conv_p, conv_s)


def kernel(x_prompt, x_sample, cache_kv_latent, cache_k_rope, page_table, state_pool, state_conv, norm_attn_g, w_in, b_gates, kv_norm_g, w_uk, w_uv, w_attn_out, w_pool_group, pool_scale, w_pool_out, w_o, norm_ffn_g, w_up, conv_w, conv_b, w_down, norm_final_g):
    batch, seq, d = x_prompt.shape
    n_seq, t_new, _ = x_sample.shape
    depth = w_in.shape[0]
    assert depth == 1, "the stacked-row pipeline below is written for a single layer"
    n_past = page_table.shape[1] * PAGE_SIZE
    dims = (batch, seq, n_seq, t_new, n_past)
    xp = x_prompt.reshape(batch * seq, d)
    xs = x_sample.reshape(n_seq * t_new, d)
    l = 0
    cache_c = cache_kv_latent.reshape(cache_kv_latent.shape[1:])
    cache_krt = jnp.swapaxes(cache_k_rope.reshape(cache_k_rope.shape[1:]), 1, 2)
    outs = _layer(xp, xs, dims, cache_c, cache_krt, page_table, state_pool[l], state_conv[l],
                  norm_attn_g[l], w_in[l], b_gates[l], kv_norm_g[l], w_uk[l], w_uv[l], w_attn_out[l],
                  w_pool_group[l], pool_scale[l], w_pool_out[l], w_o[l], norm_ffn_g[l], w_up[l], conv_w[l],
                  conv_b[l], w_down[l], norm_final_g)
    y_p, y_s = outs[0].reshape(batch, seq, d), outs[1].reshape(n_seq, t_new, d)
    return (y_p, y_s) + tuple(o[None] for o in outs[2:])
```

```python
import functools
import math

import jax
import jax.numpy as jnp
from jax import lax
from jax.experimental import pallas as pl
from jax.experimental.pallas import tpu as pltpu

F32 = jnp.float32
BF16 = jnp.bfloat16

N_HEADS = 16
QK_NOPE = 128
QK_ROPE = 64
V_DIM = 128
KV_LORA = 512
ROPE_BASE = 10000.0
ATTN_SCALE = (QK_NOPE + QK_ROPE) ** -0.5
EXP2_SCALE = ATTN_SCALE * math.log2(math.e)
PAGE_SIZE = 128
POOL_WINDOWS = (2, 4, 8, 16)
POOL_BUF = max(POOL_WINDOWS) - 1
CONV_W = 3
EPS = 1e-6

LANES = 128
SUBLANES = 8
HEAD_PAD = 2 * LANES
VMEM_LIMIT = 48 * 1024 * 1024

ROW_TILE = 512
FFN_SPLIT = 2


def _cparams(*sem):
    return pltpu.CompilerParams(dimension_semantics=sem, vmem_limit_bytes=VMEM_LIMIT)


def _dot(a, b):
    return jnp.dot(a, b, preferred_element_type=F32)


def _dot_nt(a, b):
    return lax.dot_general(a, b, (((1,), (1,)), ((), ())), preferred_element_type=F32)


def _rms(x, g):
    ms = jnp.mean(x * x, axis=-1, keepdims=True)
    return x * lax.rsqrt(ms + EPS) * g


def _rope128(x, ct, st):
    lane = lax.broadcasted_iota(jnp.int32, x.shape, 1)
    half = QK_ROPE // 2
    other = jnp.where(lane % QK_ROPE < half, pltpu.roll(x, LANES - half, 1), pltpu.roll(x, half, 1))
    return x * ct + other * st


def _softmax_step(s, vals, state):
    bm = jnp.max(s, axis=-1, keepdims=True)
    if state is None:
        m_new = bm
        p = jnp.exp2((s - m_new) * EXP2_SCALE)
        l_new = jnp.sum(p, axis=-1, keepdims=True)
        acc_new = _dot(p.astype(BF16), vals)
    else:
        m, l, acc = state
        m_new = jnp.maximum(m, bm)
        alpha = jnp.exp2((m - m_new) * EXP2_SCALE)
        p = jnp.exp2((s - m_new) * EXP2_SCALE)
        l_new = alpha * l + jnp.sum(p, axis=-1, keepdims=True)
        acc_new = alpha * acc + _dot(p.astype(BF16), vals)
    return m_new, l_new, acc_new


def _softmax_merge(states):
    m_all = states[0][0]
    for m, _, _ in states[1:]:
        m_all = jnp.maximum(m_all, m)
    l_all = acc_all = None
    for m, l, acc in states:
        w = jnp.exp2((m - m_all) * EXP2_SCALE)
        l_all = w * l if l_all is None else l_all + w * l
        acc_all = w * acc if acc_all is None else acc_all + w * acc
    return m_all, l_all, acc_all


def _prompt_rows(n_prompt):
    return lambda i: jnp.minimum(i, n_prompt - 1)


def _sample_rows(n_prompt):
    return lambda i: jnp.maximum(i - n_prompt, 0)


def _norm_kernel(xp_ref, xs_ref, g_ref, o_ref, *, n_prompt):
    i = pl.program_id(0)

    @pl.when(i < n_prompt)
    def _():
        o_ref[...] = _rms(xp_ref[...], g_ref[...]).astype(o_ref.dtype)

    @pl.when(i >= n_prompt)
    def _():
        o_ref[...] = _rms(xs_ref[...], g_ref[...]).astype(o_ref.dtype)


def _norm(xp, xs, g):
    d = xp.shape[1]
    n_prompt, n_sample = xp.shape[0] // ROW_TILE, xs.shape[0] // ROW_TILE
    pr, sr = _prompt_rows(n_prompt), _sample_rows(n_prompt)
    return pl.pallas_call(
        functools.partial(_norm_kernel, n_prompt=n_prompt),
        grid=(n_prompt + n_sample,),
        in_specs=[
            pl.BlockSpec((ROW_TILE, d), lambda i: (pr(i), 0)),
            pl.BlockSpec((ROW_TILE, d), lambda i: (sr(i), 0)),
            pl.BlockSpec((1, d), lambda i: (0, 0)),
        ],
        out_specs=pl.BlockSpec((ROW_TILE, d), lambda i: (i, 0)),
        out_shape=jax.ShapeDtypeStruct((xp.shape[0] + xs.shape[0], d), BF16),
        compiler_params=_cparams("arbitrary"),
        name="rmsnorm",
    )(xp, xs, g.reshape(1, d))


def _mm_kernel(*refs, n_prompt, has_res):
    h_ref, w_ref = refs[0], refs[1]
    o_ref = refs[-1]
    r = _dot(h_ref[...], w_ref[...])
    if has_res:
        i = pl.program_id(1)

        @pl.when(i < n_prompt)
        def _():
            o_ref[...] = refs[2][...] + r

        @pl.when(i >= n_prompt)
        def _():
            o_ref[...] = refs[3][...] + r
    else:
        o_ref[...] = r.astype(o_ref.dtype)


def _mm(h, w, *, res=None, n_prompt=None, out_dtype=F32, tn=512, name="mm"):
    m, kd = h.shape
    n = w.shape[1]
    ins = [h, w]
    specs = [pl.BlockSpec((ROW_TILE, kd), lambda j, i: (i, 0)), pl.BlockSpec((kd, tn), lambda j, i: (0, j))]
    if res is not None:
        pr, sr = _prompt_rows(n_prompt), _sample_rows(n_prompt)
        ins += list(res)
        specs += [pl.BlockSpec((ROW_TILE, tn), lambda j, i: (pr(i), j)), pl.BlockSpec((ROW_TILE, tn), lambda j, i: (sr(i), j))]
    return pl.pallas_call(
        functools.partial(_mm_kernel, n_prompt=n_prompt, has_res=res is not None),
        grid=(n // tn, m // ROW_TILE),
        in_specs=specs,
        out_specs=pl.BlockSpec((ROW_TILE, tn), lambda j, i: (i, j)),
        out_shape=jax.ShapeDtypeStruct((m, n), out_dtype),
        compiler_params=_cparams("arbitrary", "arbitrary"),
        name=name,
    )(*ins)


def _q_kernel(h_ref, wt_ref, tab_ref, o_ref, wb_ref):
    hd = QK_NOPE + QK_ROPE
    pair = 2 * hd
    n_pairs = wb_ref.shape[0] // pair

    @pl.when(pl.program_id(1) == 0)
    def _():
        for p in range(n_pairs):
            a0, b0 = 2 * p * hd, (2 * p + 1) * hd
            w0 = p * pair
            wb_ref[w0:w0 + QK_NOPE, :] = wt_ref[a0:a0 + QK_NOPE, :].astype(BF16)
            wb_ref[w0 + QK_NOPE:w0 + 2 * QK_NOPE, :] = wt_ref[b0:b0 + QK_NOPE, :].astype(BF16)
            wb_ref[w0 + 2 * QK_NOPE:w0 + 2 * QK_NOPE + QK_ROPE, :] = wt_ref[a0 + QK_NOPE:a0 + hd, :].astype(BF16)
            wb_ref[w0 + 2 * QK_NOPE + QK_ROPE:w0 + pair, :] = wt_ref[b0 + QK_NOPE:b0 + hd, :].astype(BF16)

    r = _dot_nt(h_ref[...], wb_ref[...])
    ct, st = tab_ref[:, :LANES], tab_ref[:, LANES:]
    lane = lax.broadcasted_iota(jnp.int32, (r.shape[0], LANES), 1)
    for p in range(n_pairs):
        r0 = p * pair
        oa, ob = 2 * p * HEAD_PAD, (2 * p + 1) * HEAD_PAD
        o_ref[:, oa:oa + LANES] = r[:, r0:r0 + LANES].astype(o_ref.dtype)
        o_ref[:, ob:ob + LANES] = r[:, r0 + LANES:r0 + 2 * LANES].astype(o_ref.dtype)
        rp = _rope128(r[:, r0 + 2 * LANES:r0 + 3 * LANES], ct, st)
        o_ref[:, oa + LANES:oa + HEAD_PAD] = jnp.where(lane < QK_ROPE, rp, 0.0).astype(o_ref.dtype)
        o_ref[:, ob + LANES:ob + HEAD_PAD] = jnp.where(lane < QK_ROPE, pltpu.roll(rp, QK_ROPE, 1), 0.0).astype(o_ref.dtype)


def _q_proj(h, w_t, tab, tab_row, heads_per_block=8):
    m, kd = h.shape
    hd = QK_NOPE + QK_ROPE
    tn = heads_per_block * HEAD_PAD
    return pl.pallas_call(
        _q_kernel,
        grid=(N_HEADS // heads_per_block, m // ROW_TILE),
        in_specs=[
            pl.BlockSpec((ROW_TILE, kd), lambda j, i: (i, 0)),
            pl.BlockSpec((heads_per_block * hd, kd), lambda j, i: (j, 0)),
            pl.BlockSpec((ROW_TILE, 2 * LANES), lambda j, i: (tab_row(i), 0)),
        ],
        out_specs=pl.BlockSpec((ROW_TILE, tn), lambda j, i: (i, j)),
        out_shape=jax.ShapeDtypeStruct((m, N_HEADS * HEAD_PAD), BF16),
        scratch_shapes=[pltpu.VMEM((heads_per_block * hd, kd), BF16)],
        compiler_params=_cparams("arbitrary", "arbitrary"),
        name="q_proj",
    )(h, w_t, tab)


def _ckv_kernel(h_ref, wt_hbm, g_ref, tab_ref, cp_ref, cs_ref, krp_ref, krs_ref, u_ref, wf_ref, wb_ref, sem,
                *, row0, n_prompt):
    i = pl.program_id(0)
    n_ck = KV_LORA + QK_ROPE
    n_ck_pad = KV_LORA + LANES

    @pl.when(i == 0)
    def _():
        cp = pltpu.make_async_copy(wt_hbm.at[pl.ds(row0, wf_ref.shape[0])], wf_ref, sem)
        cp.start()
        cp.wait()
        wb_ref[0:n_ck, :] = wf_ref[0:n_ck, :].astype(BF16)
        wb_ref[n_ck:n_ck_pad, :] = jnp.zeros((n_ck_pad - n_ck, wb_ref.shape[1]), BF16)
        wb_ref[n_ck_pad:, :] = wf_ref[n_ck:, :].astype(BF16)

    r = _dot_nt(h_ref[...], wb_ref[...])
    c = _rms(r[:, :KV_LORA], g_ref[...])
    kr = _rope128(r[:, KV_LORA:n_ck_pad], tab_ref[:, :LANES], tab_ref[:, LANES:])
    u_ref[...] = r[:, n_ck_pad:]

    @pl.when(i < n_prompt)
    def _():
        cp_ref[...] = c
        krp_ref[...] = kr

    @pl.when(i >= n_prompt)
    def _():
        cs_ref[...] = c
        krs_ref[...] = kr


def _ckv_pool_proj(h, w_t, row0, pool_w, g, tab, tab_row, mp, ms):
    m, kd = h.shape
    n_rows = KV_LORA + QK_ROPE + pool_w
    n_pad = KV_LORA + LANES + pool_w
    n_prompt = mp // ROW_TILE
    pr, sr = _prompt_rows(n_prompt), _sample_rows(n_prompt)
    return pl.pallas_call(
        functools.partial(_ckv_kernel, row0=row0, n_prompt=n_prompt),
        grid=(m // ROW_TILE,),
        in_specs=[
            pl.BlockSpec((ROW_TILE, kd), lambda i: (i, 0)),
            pl.BlockSpec(memory_space=pl.ANY),
            pl.BlockSpec((1, KV_LORA), lambda i: (0, 0)),
            pl.BlockSpec((ROW_TILE, 2 * LANES), lambda i: (tab_row(i), 0)),
        ],
        out_specs=[
            pl.BlockSpec((ROW_TILE, KV_LORA), lambda i: (pr(i), 0)),
            pl.BlockSpec((ROW_TILE, KV_LORA), lambda i: (sr(i), 0)),
            pl.BlockSpec((ROW_TILE, LANES), lambda i: (pr(i), 0)),
            pl.BlockSpec((ROW_TILE, LANES), lambda i: (sr(i), 0)),
            pl.BlockSpec((ROW_TILE, pool_w), lambda i: (i, 0)),
        ],
        out_shape=[
            jax.ShapeDtypeStruct((mp, KV_LORA), F32), jax.ShapeDtypeStruct((ms, KV_LORA), F32),
            jax.ShapeDtypeStruct((mp, LANES), F32), jax.ShapeDtypeStruct((ms, LANES), F32),
            jax.ShapeDtypeStruct((m, pool_w), F32),
        ],
        scratch_shapes=[pltpu.VMEM((n_rows, kd), F32), pltpu.VMEM((n_pad, kd), BF16), pltpu.SemaphoreType.DMA(())],
        compiler_params=_cparams("arbitrary"),
        name="ckv_pool_proj",
    )(h, w_t, g.reshape(1, KV_LORA), tab)


def _kvup_kernel(c_ref, kr_ref, w_ref, k_ref, v_ref):
    r = _dot(c_ref[...].astype(BF16), w_ref[...])
    nk = N_HEADS * QK_NOPE
    v_ref[...] = r[:, nk:].astype(v_ref.dtype)
    krb = kr_ref[...].astype(k_ref.dtype)
    for hh in range(N_HEADS):
        k_ref[:, hh * HEAD_PAD:hh * HEAD_PAD + LANES] = r[:, hh * QK_NOPE:(hh + 1) * QK_NOPE].astype(k_ref.dtype)
        k_ref[:, hh * HEAD_PAD + LANES:(hh + 1) * HEAD_PAD] = krb


def _kv_up(ckv, kr, w_ukv, tm=512):
    rows = ckv.shape[0]
    n = w_ukv.shape[1]
    return pl.pallas_call(
        _kvup_kernel,
        grid=(rows // tm,),
        in_specs=[
            pl.BlockSpec((tm, KV_LORA), lambda i: (i, 0)),
            pl.BlockSpec((tm, LANES), lambda i: (i, 0)),
            pl.BlockSpec((KV_LORA, n), lambda i: (0, 0)),
        ],
        out_specs=[pl.BlockSpec((tm, N_HEADS * HEAD_PAD), lambda i: (i, 0)), pl.BlockSpec((tm, N_HEADS * V_DIM), lambda i: (i, 0))],
        out_shape=[jax.ShapeDtypeStruct((rows, N_HEADS * HEAD_PAD), BF16), jax.ShapeDtypeStruct((rows, N_HEADS * V_DIM), BF16)],
        compiler_params=_cparams("arbitrary"),
        name="kv_up",
    )(ckv, kr, w_ukv)


def _pattn_kernel(q_ref, k_ref, v_ref, o_ref, *, tq):
    nq = q_ref.shape[0] // tq
    row = lax.broadcasted_iota(jnp.int32, (tq, tq), 0)
    col = lax.broadcasted_iota(jnp.int32, (tq, tq), 1)
    for qi in range(nq):
        q = q_ref[qi * tq:(qi + 1) * tq, :]
        state = None
        for ki in range(qi + 1):
            s = _dot_nt(q, k_ref[ki * tq:(ki + 1) * tq, :])
            if ki == qi:
                s = jnp.where(col <= row, s, -jnp.inf)
            state = _softmax_step(s, v_ref[ki * tq:(ki + 1) * tq, :], state)
        _, l, acc = state
        o_ref[qi * tq:(qi + 1) * tq, :] = (acc / l).astype(o_ref.dtype)


def _prompt_attn(q, k, v, batch, seq, tq=512):
    return pl.pallas_call(
        functools.partial(_pattn_kernel, tq=tq),
        grid=(batch, N_HEADS),
        in_specs=[
            pl.BlockSpec((seq, HEAD_PAD), lambda b, h: (b, h)),
            pl.BlockSpec((seq, HEAD_PAD), lambda b, h: (b, h)),
            pl.BlockSpec((seq, V_DIM), lambda b, h: (b, h)),
        ],
        out_specs=pl.BlockSpec((seq, V_DIM), lambda b, h: (b, h)),
        out_shape=jax.ShapeDtypeStruct((batch * seq, N_HEADS * V_DIM), BF16),
        compiler_params=_cparams("arbitrary", "arbitrary"),
        name="prompt_attn",
    )(q, k, v)


def _qlat_kernel(q_ref, w_ref, ql_ref, qr_ref):
    q = q_ref[...]
    ql_ref[...] = _dot(q[:, :QK_NOPE], w_ref[...])
    qr_ref[...] = q[:, LANES:].astype(F32)


def _q_latent(q, w_ukt, row0, rows):
    blk = row0 // rows
    return pl.pallas_call(
        _qlat_kernel,
        grid=(N_HEADS,),
        in_specs=[
            pl.BlockSpec((rows, HEAD_PAD), lambda h: (blk, h)),
            pl.BlockSpec((None, QK_NOPE, KV_LORA), lambda h: (h, 0, 0)),
        ],
        out_specs=[
            pl.BlockSpec((None, rows, KV_LORA), lambda h: (h, 0, 0)),
            pl.BlockSpec((None, rows, LANES), lambda h: (h, 0, 0)),
        ],
        out_shape=[jax.ShapeDtypeStruct((N_HEADS, rows, KV_LORA), F32), jax.ShapeDtypeStruct((N_HEADS, rows, LANES), F32)],
        compiler_params=_cparams("arbitrary"),
        name="q_latent",
    )(q, w_ukt)


KCAT = KV_LORA + LANES


def _sattn_kernel(pt_ref, ql_ref, qr_ref, cn_ref, krn_ref, cache_c, cache_krt, o_ref,
                  qcat, ncat, cbuf, krbuf, kcb, krtb, sem, *, cp, n_chunks, t_new):
    seq = pl.program_id(0)
    n_seq = pl.num_programs(0)
    rows = N_HEADS * t_new
    n_slots = cbuf.shape[0]
    assert n_chunks % n_slots == 0

    def copies(sq, chunk):
        slot = chunk % n_slots
        out = []
        for k in range(cp):
            page = pt_ref[sq, chunk * cp + k]
            out.append(pltpu.make_async_copy(cache_c.at[page], cbuf.at[slot, k], sem.at[slot]))
            out.append(pltpu.make_async_copy(cache_krt.at[page], krbuf.at[slot, k], sem.at[slot]))
        return out

    def start(sq, chunk):
        for n, c in enumerate(copies(sq, chunk)):
            c.start(priority=(n // 2) % 2)

    def fetch(chunk):
        slot, half = chunk % n_slots, chunk % 2
        for c in copies(seq, chunk):
            c.wait()
        for k in range(cp):
            kcb[half, k * PAGE_SIZE:(k + 1) * PAGE_SIZE, :] = cbuf[slot, k].astype(BF16)
            krtb[half, :, k * PAGE_SIZE:(k + 1) * PAGE_SIZE] = krbuf[slot, k].astype(BF16)
        ahead = chunk + n_slots
        if ahead < n_chunks:
            start(seq, ahead)
        else:
            @pl.when(seq + 1 < n_seq)
            def _():
                start(seq + 1, ahead - n_chunks)
        return _dot_nt(qcat[:, :KV_LORA], kcb[half]) + _dot(qcat[:, KV_LORA:KV_LORA + QK_ROPE], krtb[half])

    @pl.when(seq == 0)
    def _():
        for chunk in range(n_slots):
            start(seq, chunk)

    qcat[:, :KV_LORA] = ql_ref[...].reshape(rows, KV_LORA).astype(BF16)
    qcat[:, KV_LORA:] = qr_ref[...].reshape(rows, LANES).astype(BF16)
    ncat[...] = jnp.zeros(ncat.shape, BF16)
    ncat[0:t_new, :KV_LORA] = cn_ref[...].astype(BF16)
    ncat[0:t_new, KV_LORA:] = krn_ref[...].astype(BF16)
    s_new = _dot_nt(qcat[...], ncat[...])
    row = lax.broadcasted_iota(jnp.int32, s_new.shape, 0)
    col = lax.broadcasted_iota(jnp.int32, s_new.shape, 1)
    s_new = jnp.where(col <= row % t_new, s_new, -jnp.inf)
    state = _softmax_step(s_new, ncat[:, :KV_LORA], None)

    s_cur = fetch(0)
    for c in range(n_chunks):
        if c + 1 < n_chunks:
            s_next = fetch(c + 1)
        state = _softmax_merge([state, _softmax_step(s_cur, kcb[c % 2], None)])
        if c + 1 < n_chunks:
            s_cur = s_next

    _, l, acc = state
    o_ref[...] = (acc / l).reshape(o_ref.shape)


def _sample_attn(qlat, qr, ckv_s, kr_s, cache_c, cache_krt, page_table, t_new, cp=8, n_slots=4):
    n_seq, n_pages = page_table.shape
    rows = N_HEADS * t_new
    grid_spec = pltpu.PrefetchScalarGridSpec(
        num_scalar_prefetch=1,
        grid=(n_seq,),
        in_specs=[
            pl.BlockSpec((N_HEADS, t_new, KV_LORA), lambda s, pt: (0, s, 0)),
            pl.BlockSpec((N_HEADS, t_new, LANES), lambda s, pt: (0, s, 0)),
            pl.BlockSpec((t_new, KV_LORA), lambda s, pt: (s, 0)),
            pl.BlockSpec((t_new, LANES), lambda s, pt: (s, 0)),
            pl.BlockSpec(memory_space=pl.ANY),
            pl.BlockSpec(memory_space=pl.ANY),
        ],
        out_specs=pl.BlockSpec((N_HEADS, t_new, KV_LORA), lambda s, pt: (0, s, 0)),
        scratch_shapes=[
            pltpu.VMEM((rows, KCAT), BF16),
            pltpu.VMEM((PAGE_SIZE, KCAT), BF16),
            pltpu.VMEM((n_slots, cp, PAGE_SIZE, KV_LORA), F32),
            pltpu.VMEM((n_slots, cp, QK_ROPE, PAGE_SIZE), F32),
            pltpu.VMEM((2, cp * PAGE_SIZE, KV_LORA), BF16),
            pltpu.VMEM((2, QK_ROPE, cp * PAGE_SIZE), BF16),
            pltpu.SemaphoreType.DMA((n_slots,)),
        ],
    )
    return pl.pallas_call(
        functools.partial(_sattn_kernel, cp=cp, n_chunks=n_pages // cp, t_new=t_new),
        grid_spec=grid_spec,
        out_shape=jax.ShapeDtypeStruct((N_HEADS, n_seq * t_new, KV_LORA), F32),
        compiler_params=_cparams("arbitrary"),
        name="sample_attn",
    )(page_table, qlat, qr, ckv_s, kr_s, cache_c, cache_krt)


def _ouv_kernel(o_ref, w_ref, out_ref):
    out_ref[...] = _dot(o_ref[...].astype(BF16), w_ref[...]).astype(out_ref.dtype)


def _o_uv(olat, w_uv2):
    _, rows, _ = olat.shape
    return pl.pallas_call(
        _ouv_kernel,
        grid=(N_HEADS,),
        in_specs=[
            pl.BlockSpec((None, rows, KV_LORA), lambda h: (h, 0, 0)),
            pl.BlockSpec((KV_LORA, V_DIM), lambda h: (0, h)),
        ],
        out_specs=pl.BlockSpec((rows, V_DIM), lambda h: (0, h)),
        out_shape=jax.ShapeDtypeStruct((rows, N_HEADS * V_DIM), BF16),
        compiler_params=_cparams("arbitrary"),
        name="o_uv",
    )(olat, w_uv2)


HALO = 2 * SUBLANES


def _pool_groups(load, u_of, cnt_of, wg_ref, ps_ref, store):
    group = wg_ref.shape[1]
    for g, w in enumerate(POOL_WINDOWS):
        sl = slice(g * group, (g + 1) * group)
        ws = load(0, sl)
        for k in range(1, w):
            ws = ws + load(k, sl)
        pooled = ws / cnt_of(w) - u_of(sl)
        pooled = pooled.reshape(-1, group).astype(BF16)
        store(sl, (_dot(pooled, wg_ref[g]) * ps_ref[:, sl]))


def _pool_prompt_kernel(u_ref, wg_ref, ps_ref, y_ref, ext_ref, *, tiles_per_seq):
    i = pl.program_id(0)
    tm = u_ref.shape[0]
    it = i % tiles_per_seq

    @pl.when(it == 0)
    def _():
        ext_ref[0:HALO, :] = jnp.zeros((HALO, ext_ref.shape[1]), F32)

    @pl.when(it != 0)
    def _():
        ext_ref[0:HALO, :] = ext_ref[tm:tm + HALO, :]

    ext_ref[HALO:HALO + tm, :] = u_ref[...]
    pos = it * tm + lax.broadcasted_iota(jnp.int32, (tm, 1), 0)

    def store(sl, val):
        y_ref[:, sl] = val.astype(y_ref.dtype)

    _pool_groups(
        lambda k, sl: ext_ref[HALO - k:HALO - k + tm, sl],
        lambda sl: u_ref[:, sl],
        lambda w: jnp.minimum(pos + 1, w).astype(F32),
        wg_ref, ps_ref, store)


def _pool_prompt(u, wg, ps, rows, seq):
    width = wg.shape[0] * wg.shape[1]
    tm = ROW_TILE
    return pl.pallas_call(
        functools.partial(_pool_prompt_kernel, tiles_per_seq=seq // tm),
        grid=(rows // tm,),
        in_specs=[
            pl.BlockSpec((tm, width), lambda i: (i, 0)),
            pl.BlockSpec(wg.shape, lambda i: (0, 0, 0)),
            pl.BlockSpec((1, width), lambda i: (0, 0)),
        ],
        out_specs=pl.BlockSpec((tm, width), lambda i: (i, 0)),
        out_shape=jax.ShapeDtypeStruct((rows, width), BF16),
        scratch_shapes=[pltpu.VMEM((HALO + tm, width), F32)],
        compiler_params=_cparams("arbitrary"),
        name="pool_prompt",
    )(u, wg, ps)


def _pool_sample_kernel(u_ref, st_ref, wg_ref, ps_ref, y_ref, ext_ref, *, pos0):
    t_new = u_ref.shape[1]
    ext_ref[:, 0:HALO, :] = st_ref[...]
    ext_ref[:, HALO:HALO + t_new, :] = u_ref[...]
    pos = pos0 + lax.broadcasted_iota(jnp.int32, (1, t_new, 1), 1)

    def store(sl, val):
        y_ref[:, sl] = val.astype(y_ref.dtype)

    _pool_groups(
        lambda k, sl: ext_ref[:, HALO - k:HALO - k + t_new, sl],
        lambda sl: u_ref[:, :, sl],
        lambda w: jnp.minimum(pos + 1, w).astype(F32),
        wg_ref, ps_ref, store)


def _pool_sample(u3, st, wg, ps, pos0, seq0, ns=64):
    n_seq = st.shape[0]
    _, t_new, width = u3.shape
    blk0 = seq0 // ns
    return pl.pallas_call(
        functools.partial(_pool_sample_kernel, pos0=pos0),
        grid=(n_seq // ns,),
        in_specs=[
            pl.BlockSpec((ns, t_new, width), lambda i: (blk0 + i, 0, 0)),
            pl.BlockSpec((ns, HALO, width), lambda i: (i, 0, 0)),
            pl.BlockSpec(wg.shape, lambda i: (0, 0, 0)),
            pl.BlockSpec((1, width), lambda i: (0, 0)),
        ],
        out_specs=pl.BlockSpec((ns * t_new, width), lambda i: (i, 0)),
        out_shape=jax.ShapeDtypeStruct((n_seq * t_new, width), BF16),
        scratch_shapes=[pltpu.VMEM((ns, HALO + t_new, width), F32)],
        compiler_params=_cparams("arbitrary"),
        name="pool_sample",
    )(u3, st, wg, ps)


def _merge_kernel(h_ref, oap_ref, oas_ref, ypp_ref, yps_ref, wt_hbm, ba_ref, bb_ref, wa_ref, wp_ref, o_ref,
                  wf_ref, wb_ref, sem, *, row0, n_blocks, n_prompt):
    j, i = pl.program_id(0), pl.program_id(1)
    tn = o_ref.shape[1]
    n = tn * n_blocks

    @pl.when(i == 0)
    def _():
        cps = [pltpu.make_async_copy(wt_hbm.at[pl.ds(row0 + br * n + j * tn, tn)], wf_ref.at[br], sem.at[br])
               for br in range(2)]
        for cp in cps:
            cp.start()
        for cp in cps:
            cp.wait()
        wb_ref[...] = wf_ref[...].astype(BF16)

    h = h_ref[...]
    ga = jax.nn.sigmoid(_dot_nt(h, wb_ref[0]) + ba_ref[...])
    gb = jax.nn.sigmoid(_dot_nt(h, wb_ref[1]) + bb_ref[...])

    is_p = i < n_prompt
    oa = jnp.where(is_p, oap_ref[...], oas_ref[...])
    yp = jnp.where(is_p, ypp_ref[...], yps_ref[...])
    a = _dot(oa, wa_ref[...])
    p = _dot(yp, wp_ref[...])
    o_ref[...] = (ga * a + gb * p).astype(o_ref.dtype)


def _merge(h, oa_p, oa_s, yp_p, yp_s, w_t, row0, b_gates, wa, wp, tn=512):
    m, d = h.shape
    ka, kp = wa.shape[0], wp.shape[0]
    n = wa.shape[1]
    nb = n // tn
    n_prompt = oa_p.shape[0] // ROW_TILE
    pr, sr = _prompt_rows(n_prompt), _sample_rows(n_prompt)
    bg = b_gates.reshape(1, -1)
    return pl.pallas_call(
        functools.partial(_merge_kernel, row0=row0, n_blocks=nb, n_prompt=n_prompt),
        grid=(nb, m // ROW_TILE),
        in_specs=[
            pl.BlockSpec((ROW_TILE, d), lambda j, i: (i, 0)),
            pl.BlockSpec((ROW_TILE, ka), lambda j, i: (pr(i), 0)),
            pl.BlockSpec((ROW_TILE, ka), lambda j, i: (sr(i), 0)),
            pl.BlockSpec((ROW_TILE, kp), lambda j, i: (pr(i), 0)),
            pl.BlockSpec((ROW_TILE, kp), lambda j, i: (sr(i), 0)),
            pl.BlockSpec(memory_space=pl.ANY),
            pl.BlockSpec((1, tn), lambda j, i: (0, j)),
            pl.BlockSpec((1, tn), lambda j, i: (0, nb + j)),
            pl.BlockSpec((ka, tn), lambda j, i: (0, j)),
            pl.BlockSpec((kp, tn), lambda j, i: (0, j)),
        ],
        out_specs=pl.BlockSpec((ROW_TILE, tn), lambda j, i: (i, j)),
        out_shape=jax.ShapeDtypeStruct((m, n), BF16),
        scratch_shapes=[pltpu.VMEM((2, tn, d), F32), pltpu.VMEM((2, tn, d), BF16), pltpu.SemaphoreType.DMA((2,))],
        compiler_params=_cparams("arbitrary", "arbitrary"),
        name="merge",
    )(h, oa_p, oa_s, yp_p, yp_s, w_t, bg, bg, wa, wp)


def _ffn_kernel(*refs, sample, tiles_per_seq, t_new):
    if sample:
        (x_ref, g_ref, wa_ref, wv_ref, cw_ref, cb_ref, wd_ref, gf_ref, st_ref,
         y_ref, cs_ref, hn_ref, acc_ref) = refs
        assert CONV_W == 3
    else:
        (x_ref, g_ref, wa_ref, wv_ref, cw_ref, cb_ref, wd_ref, gf_ref,
         y_ref, tail_ref, hn_ref, acc_ref, carry_ref) = refs
    i, f = pl.program_id(0), pl.program_id(1)
    tm = x_ref.shape[0]

    @pl.when(f == 0)
    def _():
        hn_ref[...] = _rms(x_ref[...], g_ref[...]).astype(hn_ref.dtype)
        acc_ref[...] = jnp.zeros(acc_ref.shape, F32)

    if not sample:
        @pl.when((i == 0) & (f == 0))
        def _():
            carry_ref[...] = jnp.zeros(carry_ref.shape, F32)

    hn = hn_ref[...]
    tf = wa_ref.shape[1]
    sw = tf // FFN_SPLIT
    cols = [slice(k * sw, (k + 1) * sw) for k in range(FFN_SPLIT)]
    ups = [(_dot(hn, wa_ref[:, cs]), _dot(hn, wv_ref[:, cs])) for cs in cols]
    hids = []
    for cs, (a, v) in zip(cols, ups):
        r1, r2 = pltpu.roll(a, 1, 0), pltpu.roll(a, 2, 0)
        if sample:
            ns = tm // t_new
            a, v, r1, r2 = (z.reshape(ns, t_new, sw) for z in (a, v, r1, r2))
            t = lax.broadcasted_iota(jnp.int32, (1, t_new, 1), 1)
            st0, st1 = st_ref[:, 0:1, cs], st_ref[:, 1:2, cs]
            s1 = jnp.where(t >= 1, r1, st1)
            s2 = jnp.where(t >= 2, r2, jnp.where(t == 0, st0, st1))
            cs_ref[:, :, cs] = a[:, t_new - (CONV_W - 1):, :]
        else:
            first = i % tiles_per_seq == 0
            prev = jnp.where(first, 0.0, carry_ref[f, :, cs])
            row8 = lax.broadcasted_iota(jnp.int32, (SUBLANES, 1), 0)
            h1 = jnp.where(row8 < 1, pltpu.roll(prev, 1, 0), r1[:SUBLANES])
            h2 = jnp.where(row8 < 2, pltpu.roll(prev, 2, 0), r2[:SUBLANES])
            s1 = jnp.concatenate([h1, r1[SUBLANES:]], axis=0)
            s2 = jnp.concatenate([h2, r2[SUBLANES:]], axis=0)
            tail = a[tm - SUBLANES:, :]
            carry_ref[f, :, cs] = tail
            tail_ref[:, cs] = tail
        conv = cb_ref[:, cs] + cw_ref[0:1, cs] * s2
        conv = conv + cw_ref[1:2, cs] * s1
        conv = conv + cw_ref[2:3, cs] * a
        half = 0.5 * conv
        hids.append(((half * v) * (jnp.tanh(half) + 1.0)).reshape(tm, sw).astype(BF16))
    acc = acc_ref[...]
    for cs, hid in zip(cols, hids):
        acc = acc + _dot(hid, wd_ref[cs, :])
    acc_ref[...] = acc

    @pl.when(f == pl.num_programs(1) - 1)
    def _():
        x2 = x_ref[...] + acc_ref[...]
        y_ref[...] = _rms(x2, gf_ref[...])


def _ffn(x, row0, rows, g, w_up, conv_w, conv_b, w_down, gf, *, seq=None, prev=None, t_new=None, tf=512):
    d = x.shape[1]
    d_ff = w_down.shape[0]
    nf = d_ff // tf
    tm = ROW_TILE
    blk0 = row0 // tm
    sample = prev is not None
    in_specs = [
        pl.BlockSpec((tm, d), lambda i, f: (blk0 + i, 0)),
        pl.BlockSpec((1, d), lambda i, f: (0, 0)),
        pl.BlockSpec((d, tf), lambda i, f: (0, f)),
        pl.BlockSpec((d, tf), lambda i, f: (0, nf + f)),
        pl.BlockSpec((CONV_W, tf), lambda i, f: (0, f)),
        pl.BlockSpec((1, tf), lambda i, f: (0, f)),
        pl.BlockSpec((tf, d), lambda i, f: (f, 0)),
        pl.BlockSpec((1, d), lambda i, f: (0, 0)),
    ]
    ins = [x, g.reshape(1, d), w_up, w_up, conv_w, conv_b.reshape(1, d_ff), w_down, gf.reshape(1, d)]
    scratch = [pltpu.VMEM((tm, d), BF16), pltpu.VMEM((tm, d), F32)]
    y_spec = pl.BlockSpec((tm, d), lambda i, f: (i, 0))
    y_shape = jax.ShapeDtypeStruct((rows, d), F32)
    if sample:
        st_spec = pl.BlockSpec((tm // t_new, CONV_W - 1, tf), lambda i, f: (i, 0, f))
        in_specs.append(st_spec)
        ins.append(prev)
        out_specs = [y_spec, st_spec]
        out_shape = [y_shape, jax.ShapeDtypeStruct(prev.shape, F32)]
        tiles_per_seq = None
    else:
        out_specs = [y_spec, pl.BlockSpec((None, SUBLANES, tf), lambda i, f: (i, 0, f))]
        out_shape = [y_shape, jax.ShapeDtypeStruct((rows // tm, SUBLANES, d_ff), F32)]
        scratch.append(pltpu.VMEM((nf, SUBLANES, tf), F32))
        tiles_per_seq = seq // tm
    return pl.pallas_call(
        functools.partial(_ffn_kernel, sample=sample, tiles_per_seq=tiles_per_seq, t_new=t_new),
        grid=(rows // tm, nf),
        in_specs=in_specs,
        out_specs=out_specs,
        out_shape=out_shape,
        scratch_shapes=scratch,
        compiler_params=_cparams("arbitrary", "arbitrary"),
        name="ffn_sample" if sample else "ffn_prompt",
    )(*ins)


def _rope_table(pos):
    half = QK_ROPE // 2
    inv = ROPE_BASE ** (-jnp.arange(half, dtype=F32) / half)
    ang = pos[:, None] * inv[None, :]
    c, s = jnp.cos(ang), jnp.sin(ang)
    return jnp.concatenate([c, c, c, c, -s, s, -s, s], axis=1)


def _layer(xp, xs, dims, cache_c, cache_krt, page_table, state_pool, state_conv,
           norm_attn_g, w_in, b_gates, kv_norm_g, w_uk, w_uv, w_attn_out, w_pool_group, pool_scale,
           w_pool_out, w_o, norm_ffn_g, w_up, conv_w, conv_b, w_down, norm_final_g):
    batch, seq, n_seq, t_new, n_past = dims
    mp, ms = xp.shape[0], xs.shape[0]
    d = xp.shape[1]
    n_prompt = mp // ROW_TILE
    tiles_per_seq = seq // ROW_TILE
    q_cols = N_HEADS * (QK_NOPE + QK_ROPE)
    pool_w = w_pool_out.shape[0]
    c0, c2 = q_cols, q_cols + KV_LORA + QK_ROPE
    c3 = c2 + pool_w

    pos = jnp.concatenate([jnp.arange(seq, dtype=F32), jnp.tile(n_past + jnp.arange(t_new, dtype=F32), ROW_TILE // t_new)])
    tab = _rope_table(pos)

    def tab_row(i):
        return jnp.where(i < n_prompt, i % tiles_per_seq, tiles_per_seq)

    w_t = jnp.swapaxes(w_in, 0, 1)
    w_ukv = jnp.concatenate([w_uk.reshape(KV_LORA, -1), w_uv.reshape(KV_LORA, -1)], axis=1).astype(BF16)
    w_ukt = jnp.transpose(w_uk, (1, 2, 0)).astype(BF16)
    w_uv2 = w_uv.reshape(KV_LORA, -1).astype(BF16)

    h = _norm(xp, xs, norm_attn_g)
    q = _q_proj(h, w_t, tab, tab_row)
    ckv_p, ckv_s, kr_p, kr_s, u = _ckv_pool_proj(h, w_t, c0, pool_w, kv_norm_g, tab, tab_row, mp, ms)

    kp, vp = _kv_up(ckv_p, kr_p, w_ukv)
    o_p = _prompt_attn(q, kp, vp, batch, seq)
    qlat, qr = _q_latent(q, w_ukt, mp, ms)
    olat = _sample_attn(qlat, qr, ckv_s, kr_s, cache_c, cache_krt, page_table, t_new)
    o_s = _o_uv(olat, w_uv2)

    wpg = w_pool_group.astype(BF16)
    ps = pool_scale.reshape(1, pool_w)
    y_p = _pool_prompt(u, wpg, ps, mp, seq)
    u3 = u.reshape((mp + ms) // t_new, t_new, pool_w)
    st_pad = jnp.pad(state_pool, ((0, 0), (HALO - POOL_BUF, 0), (0, 0)))
    y_s = _pool_sample(u3, st_pad, wpg, ps, n_past, mp // t_new)

    merged = _merge(h, o_p, o_s, y_p, y_s, w_t, c3, b_gates, w_attn_out.astype(BF16), w_pool_out.astype(BF16))
    x1 = _mm(merged, w_o.astype(BF16), res=(xp, xs), n_prompt=n_prompt, tn=d, name="o_proj")

    w_up_b = w_up.astype(BF16)
    w_down_b = w_down.astype(BF16)
    y_prompt, tails = _ffn(x1, 0, mp, norm_ffn_g, w_up_b, conv_w, conv_b, w_down_b, norm_final_g, seq=seq)
    y_sample, conv_s = _ffn(x1, mp, ms, norm_ffn_g, w_up_b, conv_w, conv_b, w_down_b, norm_final_g,
                            prev=state_conv, t_new=t_new)

    conv_p = tails[tiles_per_seq - 1::tiles_per_seq, SUBLANES - (CONV_W - 1):]
    pool_p = jnp.stack([u[(b + 1) * seq - POOL_BUF:(b + 1) * seq] for b in range(batch)])
    u_s = u[mp:].reshape(n_seq, t_new, pool_w)
    pool_s = jnp.concatenate([state_pool, u_s], axis=1)[:, -POOL_BUF:]
    return (y_prompt, y_sample, ckv_p.reshape(batch, seq, KV_LORA), kr_p[:, :QK_ROPE].reshape(batch, seq, QK_ROPE),
            ckv_s.reshape(n_seq, t_new, KV_LORA), kr_s[:, :QK_ROPE].reshape(n_seq, t_new, QK_ROPE),
            pool_p, pool_s, conv_p, conv_s)


def kernel(x_prompt, x_sample, cache_kv_latent, cache_k_rope, page_table, state_pool, state_conv, norm_attn_g, w_in, b_gates, kv_norm_g, w_uk, w_uv, w_attn_out, w_pool_group, pool_scale, w_pool_out, w_o, norm_ffn_g, w_up, conv_w, conv_b, w_down, norm_final_g):
    batch, seq, d = x_prompt.shape
    n_seq, t_new, _ = x_sample.shape
    depth = w_in.shape[0]
    assert depth == 1, "the stacked-row pipeline below is written for a single layer"
    n_past = page_table.shape[1] * PAGE_SIZE
    dims = (batch, seq, n_seq, t_new, n_past)
    xp = x_prompt.reshape(batch * seq, d)
    xs = x_sample.reshape(n_seq * t_new, d)
    l = 0
    cache_c = cache_kv_latent.reshape(cache_kv_latent.shape[1:])
    cache_krt = jnp.swapaxes(cache_k_rope.reshape(cache_k_rope.shape[1:]), 1, 2)
    outs = _layer(xp, xs, dims, cache_c, cache_krt, page_table, state_pool[l], state_conv[l],
                  norm_attn_g[l], w_in[l], b_gates[l], kv_norm_g[l], w_uk[l], w_uv[l], w_attn_out[l],
                  w_pool_group[l], pool_scale[l], w_pool_out[l], w_o[l], norm_ffn_g[l], w_up[l], conv_w[l],
                  conv_b[l], w_down[l], norm_final_g)
    y_p, y_s = outs[0].reshape(batch, seq, d), outs[1].reshape(n_seq, t_new, d)
    return (y_p, y_s) + tuple(o[None] for o in outs[2:])
```
